```python
import math
import jax, jax.numpy as jnp
from jax import lax
import numpy as np

D_MODEL = 1024
BATCH = 16
SEQ = 2048
DEPTH = 1

D_HY = D_MODEL
HY_ORDER = 2
HY_SHORT_CONV = 3
HY_EMB_DIM = 33
HY_FILTER_HIDDEN = 64
HY_FAST_DECAY_PCT = 0.3
HY_SLOW_DECAY_PCT = 1.5
HY_DECAY_TARGET = 1e-2
HY_FILTER_OUT_STD = 0.01
HY_COLS = (HY_ORDER + 1) * D_HY

SSD_EXPAND = 2
D_SSM = SSD_EXPAND * D_MODEL
SSD_HEADDIM = 64
SSD_HEADS = D_SSM // SSD_HEADDIM
SSD_GROUPS = 8
SSD_STATE = 128
SSD_CONV = 5
SSD_CHUNK = 128
SSD_CONV_DIM = D_SSM + 2 * SSD_GROUPS * SSD_STATE
SSD_DT_MIN = 1e-3
SSD_DT_MAX = 1e-1
SSD_COLS = D_SSM + SSD_CONV_DIM + 2 * SSD_HEADS

N_BRANCH = 2
GATE_COLS = N_BRANCH * D_MODEL
IN_COLS = HY_COLS + SSD_COLS + GATE_COLS

N_EXPERTS = 64
TOP_K = 8
N_EXPERT_GROUPS = 8
TOPK_GROUPS = 4
D_EXPERT = 256
D_SHARED = 256
ROUTED_SCALE = 2.5
MOE_BLOCK = 128

DN_ALPHA = (2.0 * DEPTH) ** 0.25
DN_BETA = (8.0 * DEPTH) ** -0.25
LN_EPS = 1e-5
RMS_EPS = 1e-5

kernel_name = "hyena_ssd_gated_moe_deepnorm_block"


def layer_norm(x, g, b):
    xf = x.astype(jnp.float32)
    mu = jnp.mean(xf, axis=-1, keepdims=True)
    var = jnp.mean(jnp.square(xf - mu), axis=-1, keepdims=True)
    return ((xf - mu) * lax.rsqrt(var + LN_EPS) * g + b).astype(x.dtype)


def centred_dwconv(x, w, b):
    k_w = w.shape[0]
    seq = x.shape[1]
    pad = k_w // 2
    xp = jnp.pad(x, ((0, 0), (pad, pad), (0, 0)))
    y = b
    for k in range(k_w):
        y = y + xp[:, k:k + seq] * w[k]
    return y


def hyena_pos_features(seq):
    t = jnp.linspace(0.0, 1.0, seq, dtype=jnp.float32)[:, None]
    bands = (HY_EMB_DIM - 1) // 2
    w = 2.0 * math.pi * jnp.arange(seq, dtype=jnp.float32)[:, None] / seq
    f = jnp.linspace(1e-4, bands - 1, bands, dtype=jnp.float32)[None, :]
    return jnp.concatenate([t, jnp.cos(f * w), -jnp.sin(f * w)], axis=-1)


def hyena_decay(seq):
    t = jnp.linspace(0.0, 1.0, seq, dtype=jnp.float32)[:, None]
    max_decay = math.log(HY_DECAY_TARGET) / HY_FAST_DECAY_PCT
    min_decay = math.log(HY_DECAY_TARGET) / HY_SLOW_DECAY_PCT
    deltas = jnp.linspace(min_decay, max_decay, D_HY, dtype=jnp.float32)[None, :]
    return jnp.exp(-t * jnp.abs(deltas))


def hyena_filters(seq, w1, b1, fr1, w2, b2, fr2, w3, b3, fr3, w4):
    f32 = jnp.float32
    z = hyena_pos_features(seq)
    h = jnp.sin(fr1.astype(f32) * (z @ w1.astype(f32) + b1.astype(f32)))
    h = jnp.sin(fr2.astype(f32) * (h @ w2.astype(f32) + b2.astype(f32)))
    h = jnp.sin(fr3.astype(f32) * (h @ w3.astype(f32) + b3.astype(f32)))
    k = (h @ w4.astype(f32)).reshape(seq, HY_ORDER, 2, D_HY) * hyena_decay(seq)[:, None, None, :]
    kf, kb = k[:, :, 0], k[:, :, 1]
    k_full = jnp.concatenate([kf[:1] + kb[:1], kf[1:], jnp.zeros_like(kf[:1]), kb[:0:-1]], axis=0)
    return jnp.fft.rfft(k_full, axis=0)


def fft_long_conv(u, k_spec, bias):
    seq = u.shape[1]
    uf32 = u.astype(jnp.float32)
    uf = jnp.fft.rfft(uf32, n=2 * seq, axis=1)
    y = jnp.fft.irfft(uf * k_spec[None], n=2 * seq, axis=1)[:, :seq]
    return (y + uf32 * bias.astype(jnp.float32)).astype(u.dtype)


def hyena_mixer(p_hy, conv_w, conv_b, k_spec, hy_bias):
    hy = centred_dwconv(p_hy, conv_w, conv_b)
    g1, g2, v = jnp.split(hy, HY_ORDER + 1, axis=-1)
    z = g1 * fft_long_conv(v, k_spec[:, 0], hy_bias[0])
    return g2 * fft_long_conv(z, k_spec[:, 1], hy_bias[1])


def ssd_chunked(xdt, a, bm, cm):
    b, l, h, p = xdt.shape
    g, n = bm.shape[-2:]
    r = h // g
    c = l // SSD_CHUNK
    q = SSD_CHUNK
    X = xdt.reshape(b, c, q, g, r, p)
    A_cs = jnp.cumsum(a.reshape(b, c, q, g, r), axis=2)
    Bc = bm.reshape(b, c, q, g, n)
    Cc = cm.reshape(b, c, q, g, n)
    seg = A_cs[:, :, :, None] - A_cs[:, :, None, :]
    tril = jnp.tril(jnp.ones((q, q), dtype=bool))[:, :, None, None]
    decay = jnp.exp(jnp.where(tril, seg, -jnp.inf))
    CB = jnp.einsum("bclgn,bcsgn->bclsg", Cc, Bc)
    y_diag = jnp.einsum("bclsg,bclsgr,bcsgrp->bclgrp", CB, decay, X)
    to_end = jnp.exp(A_cs[:, :, -1:] - A_cs)
    states = jnp.einsum("bcsgn,bcsgr,bcsgrp->bcgrpn", Bc, to_end, X)
    chunk_decay = jnp.exp(A_cs[:, :, -1])

    def step(carry, inp):
        st, dec = inp
        return carry * dec[..., None, None] + st, carry

    init = jnp.zeros((b, g, r, p, n), states.dtype)
    _, prev = lax.scan(step, init, (jnp.moveaxis(states, 1, 0), jnp.moveaxis(chunk_decay, 1, 0)))
    prev = jnp.moveaxis(prev, 0, 1)
    y_off = jnp.einsum("bclgn,bcgrpn,bclgr->bclgrp", Cc, prev, jnp.exp(A_cs))
    return (y_diag + y_off).reshape(b, l, h, p)


def ssd_mixer(p_z, p_xbc, p_dt, conv_w, conv_b, dt_bias, a_log, d_skip, norm_w):
    f32 = jnp.float32
    bsz, seq, _ = p_z.shape
    xbc = jax.nn.silu(centred_dwconv(p_xbc, conv_w, conv_b))
    xs, bm, cm = jnp.split(xbc, [D_SSM, D_SSM + SSD_GROUPS * SSD_STATE], axis=-1)
    xs = xs.reshape(bsz, seq, SSD_HEADS, SSD_HEADDIM)
    bm = bm.reshape(bsz, seq, SSD_GROUPS, SSD_STATE)
    cm = cm.reshape(bsz, seq, SSD_GROUPS, SSD_STATE)
    dt = jax.nn.softplus(p_dt.astype(f32).reshape(bsz, seq, 2, SSD_HEADS) + dt_bias.astype(f32))
    a = dt * -jnp.exp(a_log.astype(f32))
    flip = lambda t: jnp.flip(t, axis=1)
    y_fwd = ssd_chunked(xs * dt[:, :, 0, :, None], a[:, :, 0], bm, cm)
    y_bwd = flip(ssd_chunked(flip(xs * dt[:, :, 1, :, None]), flip(a[:, :, 1]), flip(bm), flip(cm)))
    y = y_fwd + y_bwd + xs * d_skip[:, None]
    y = y.reshape(bsz, seq, D_SSM) * jax.nn.silu(p_z)
    yg = y.astype(f32).reshape(bsz, seq, SSD_GROUPS, D_SSM // SSD_GROUPS)
    yg = yg * lax.rsqrt(jnp.mean(jnp.square(yg), axis=-1, keepdims=True) + RMS_EPS)
    return (yg.reshape(bsz, seq, D_SSM) * norm_w).astype(p_z.dtype)


def mixer_sublayer(u, w_in, b_in, hy_conv_w, hy_conv_b, hy_f_w1, hy_f_b1, hy_f_freq1,
                   hy_f_w2, hy_f_b2, hy_f_freq2, hy_f_w3, hy_f_b3, hy_f_freq3, hy_f_w4, hy_bias,
                   ssd_conv_w, ssd_conv_b, ssd_dt_bias, ssd_a_log, ssd_d, ssd_norm_w,
                   w_hy_branch, w_ssd_branch, w_out, b_out):
    proj = u @ w_in + b_in
    splits = np.cumsum([HY_COLS, D_SSM, SSD_CONV_DIM, 2 * SSD_HEADS]).tolist()
    p_hy, p_z, p_xbc, p_dt, p_gate = jnp.split(proj, splits, axis=-1)
    k_spec = hyena_filters(u.shape[1], hy_f_w1, hy_f_b1, hy_f_freq1, hy_f_w2, hy_f_b2, hy_f_freq2,
                           hy_f_w3, hy_f_b3, hy_f_freq3, hy_f_w4)
    y_hy = hyena_mixer(p_hy, hy_conv_w, hy_conv_b, k_spec, hy_bias) @ w_hy_branch
    y_ssd = ssd_mixer(p_z, p_xbc, p_dt, ssd_conv_w, ssd_conv_b, ssd_dt_bias, ssd_a_log,
                      ssd_d, ssd_norm_w) @ w_ssd_branch
    g_hy, g_ssd = jnp.split(jax.nn.sigmoid(p_gate), N_BRANCH, axis=-1)
    return (g_hy * y_hy + g_ssd * y_ssd) @ w_out + b_out


def route(h2d, router_w, router_bias):
    n_tok = h2d.shape[0]
    scores = jax.nn.sigmoid((h2d @ router_w).astype(jnp.float32))
    biased = scores + router_bias.astype(jnp.float32)
    per_group = N_EXPERTS // N_EXPERT_GROUPS
    grp_score = lax.top_k(biased.reshape(n_tok, N_EXPERT_GROUPS, per_group), 2)[0].sum(-1)
    _, top_grp = lax.top_k(grp_score, TOPK_GROUPS)
    grp_mask = jnp.any(top_grp[..., None] == jnp.arange(N_EXPERT_GROUPS), axis=-2)
    masked = jnp.where(jnp.repeat(grp_mask, per_group, axis=-1), biased, -jnp.inf)
    _, idx = lax.top_k(masked, TOP_K)
    w = jnp.take_along_axis(scores, idx, axis=-1)
    w = w / jnp.sum(w, axis=-1, keepdims=True) * ROUTED_SCALE
    return idx, w


def routed_experts(h2d, idx, w, w_gate, w_up, w_down):
    n_tok, d = h2d.shape
    n_assign = n_tok * TOP_K
    flat_e = idx.reshape(n_assign)
    flat_tok = jnp.repeat(jnp.arange(n_tok, dtype=jnp.int32), TOP_K)
    flat_w = w.reshape(n_assign)
    order = jnp.argsort(flat_e)
    e_sorted = flat_e[order]
    counts = jnp.bincount(flat_e, length=N_EXPERTS)
    padded = (counts + MOE_BLOCK - 1) // MOE_BLOCK * MOE_BLOCK
    pad_end = jnp.cumsum(padded)
    pad_start = pad_end - padded
    sorted_start = jnp.cumsum(counts) - counts
    dest = pad_start[e_sorted] + jnp.arange(n_assign) - sorted_start[e_sorted]
    n_blocks = -(-n_assign // MOE_BLOCK) + N_EXPERTS
    n_slots = n_blocks * MOE_BLOCK
    slot_tok = jnp.full((n_slots,), n_tok, jnp.int32).at[dest].set(flat_tok[order])
    slot_w = jnp.zeros((n_slots,), jnp.float32).at[dest].set(flat_w[order])
    block_e = jnp.minimum(jnp.searchsorted(pad_end, jnp.arange(n_blocks) * MOE_BLOCK, side="right"),
                          N_EXPERTS - 1)
    h_pad = jnp.concatenate([h2d, jnp.zeros((1, d), h2d.dtype)], axis=0)

    def step(acc, blk):
        tok, wt, e = blk
        xb = h_pad[tok]
        hb = jax.nn.silu(xb @ w_gate[e]) * (xb @ w_up[e])
        yb = (hb @ w_down[e]) * wt[:, None]
        return acc.at[tok].add(yb.astype(acc.dtype)), None

    acc, _ = lax.scan(step, jnp.zeros_like(h_pad),
                      (slot_tok.reshape(n_blocks, MOE_BLOCK), slot_w.reshape(n_blocks, MOE_BLOCK), block_e))
    return acc[:n_tok]


def moe_sublayer(h, router_w, router_bias, exp_w_gate, exp_w_up, exp_w_down,
                 sh_w_gate, sh_w_up, sh_w_down):
    bsz, seq, d = h.shape
    h2d = h.reshape(bsz * seq, d)
    idx, w = route(h2d, router_w, router_bias)
    routed = routed_experts(h2d, idx, w, exp_w_gate, exp_w_up, exp_w_down)
    shared = (jax.nn.silu(h2d @ sh_w_gate) * (h2d @ sh_w_up)) @ sh_w_down
    return (routed + shared).reshape(bsz, seq, d)


def setup_inputs(seed: int = 0) -> dict:
    key = jax.random.key(seed)
    ks = iter(jax.random.split(key, 64))
    f32 = jnp.float32
    Ld = DEPTH

    def nrm(shape, std):
        return std * jax.random.normal(next(ks), shape, f32)

    def xavier(fan_in, fan_out, shape, gain=1.0):
        return nrm(shape, gain * math.sqrt(2.0 / (fan_in + fan_out)))

    def ones_noise(shape):
        return 1.0 + nrm(shape, 0.01)

    u_dt = jax.random.uniform(next(ks), (Ld, 2, SSD_HEADS), f32)
    dt0 = jnp.exp(u_dt * (math.log(SSD_DT_MAX) - math.log(SSD_DT_MIN)) + math.log(SSD_DT_MIN))
    dt_bias = dt0 + jnp.log(-jnp.expm1(-dt0))
    a_log = jnp.log(jax.random.uniform(next(ks), (Ld, 2, SSD_HEADS), f32, 1.0, 16.0))
    FH = HY_FILTER_HIDDEN
    return {
        "x": nrm((BATCH, SEQ, D_MODEL), 1.0),
        "w_in": nrm((Ld, D_MODEL, IN_COLS), D_MODEL ** -0.5),
        "b_in": nrm((Ld, IN_COLS), 0.01),
        "hy_conv_w": nrm((Ld, HY_SHORT_CONV, HY_COLS), HY_SHORT_CONV ** -0.5),
        "hy_conv_b": nrm((Ld, HY_COLS), 0.01),
        "hy_f_w1": nrm((Ld, HY_EMB_DIM, FH), HY_EMB_DIM ** -0.5),
        "hy_f_b1": nrm((Ld, FH), 0.01),
        "hy_f_freq1": ones_noise((Ld, FH)),
        "hy_f_w2": nrm((Ld, FH, FH), FH ** -0.5),
        "hy_f_b2": nrm((Ld, FH), 0.01),
        "hy_f_freq2": ones_noise((Ld, FH)),
        "hy_f_w3": nrm((Ld, FH, FH), FH ** -0.5),
        "hy_f_b3": nrm((Ld, FH), 0.01),
        "hy_f_freq3": ones_noise((Ld, FH)),
        "hy_f_w4": nrm((Ld, FH, HY_ORDER * 2 * D_HY), HY_FILTER_OUT_STD),
        "hy_bias": nrm((Ld, HY_ORDER, D_HY), 0.1),
        "ssd_conv_w": nrm((Ld, SSD_CONV, SSD_CONV_DIM), SSD_CONV ** -0.5),
        "ssd_conv_b": nrm((Ld, SSD_CONV_DIM), 0.01),
        "ssd_dt_bias": dt_bias,
        "ssd_a_log": a_log,
        "ssd_d": ones_noise((Ld, SSD_HEADS)),
        "ssd_norm_w": ones_noise((Ld, D_SSM)),
        "w_hy_branch": xavier(D_HY, D_MODEL, (Ld, D_HY, D_MODEL), DN_BETA),
        "w_ssd_branch": xavier(D_SSM, D_MODEL, (Ld, D_SSM, D_MODEL), DN_BETA),
        "w_out": xavier(D_MODEL, D_MODEL, (Ld, D_MODEL, D_MODEL), DN_BETA),
        "b_out": nrm((Ld, D_MODEL), 0.01),
        "ln1_g": ones_noise((Ld, D_MODEL)),
        "ln1_b": nrm((Ld, D_MODEL), 0.01),
        "router_w": nrm((Ld, D_MODEL, N_EXPERTS), D_MODEL ** -0.5),
        "router_bias": nrm((Ld, N_EXPERTS), 0.01),
        "exp_w_gate": nrm((Ld, N_EXPERTS, D_MODEL, D_EXPERT), D_MODEL ** -0.5),
        "exp_w_up": nrm((Ld, N_EXPERTS, D_MODEL, D_EXPERT), D_MODEL ** -0.5),
        "exp_w_down": xavier(D_EXPERT, D_MODEL, (Ld, N_EXPERTS, D_EXPERT, D_MODEL), DN_BETA),
        "sh_w_gate": nrm((Ld, D_MODEL, D_SHARED), D_MODEL ** -0.5),
        "sh_w_up": nrm((Ld, D_MODEL, D_SHARED), D_MODEL ** -0.5),
        "sh_w_down": xavier(D_SHARED, D_MODEL, (Ld, D_SHARED, D_MODEL), DN_BETA),
        "ln2_g": ones_noise((Ld, D_MODEL)),
        "ln2_b": nrm((Ld, D_MODEL), 0.01),
    }


def reference(x, w_in, b_in, hy_conv_w, hy_conv_b, hy_f_w1, hy_f_b1, hy_f_freq1,
              hy_f_w2, hy_f_b2, hy_f_freq2, hy_f_w3, hy_f_b3, hy_f_freq3, hy_f_w4, hy_bias,
              ssd_conv_w, ssd_conv_b, ssd_dt_bias, ssd_a_log, ssd_d, ssd_norm_w,
              w_hy_branch, w_ssd_branch, w_out, b_out, ln1_g, ln1_b,
              router_w, router_bias, exp_w_gate, exp_w_up, exp_w_down,
              sh_w_gate, sh_w_up, sh_w_down, ln2_g, ln2_b):
    h = x
    for i in range(DEPTH):
        mix = mixer_sublayer(h, w_in[i], b_in[i], hy_conv_w[i], hy_conv_b[i],
                             hy_f_w1[i], hy_f_b1[i], hy_f_freq1[i], hy_f_w2[i], hy_f_b2[i], hy_f_freq2[i],
                             hy_f_w3[i], hy_f_b3[i], hy_f_freq3[i], hy_f_w4[i], hy_bias[i],
                             ssd_conv_w[i], ssd_conv_b[i], ssd_dt_bias[i], ssd_a_log[i], ssd_d[i],
                             ssd_norm_w[i], w_hy_branch[i], w_ssd_branch[i], w_out[i], b_out[i])
        h = layer_norm(DN_ALPHA * h + mix, ln1_g[i], ln1_b[i])
        ffn = moe_sublayer(h, router_w[i], router_bias[i], exp_w_gate[i], exp_w_up[i], exp_w_down[i],
                           sh_w_gate[i], sh_w_up[i], sh_w_down[i])
        h = layer_norm(DN_ALPHA * h + ffn, ln2_g[i], ln2_b[i])
    return h
```

```python
import functools
import math

import jax
import jax.numpy as jnp
from jax import lax
from jax.experimental import pallas as pl
from jax.experimental.pallas import tpu as pltpu

F32 = jnp.float32
BF16 = jnp.bfloat16
HI = lax.Precision.HIGHEST

HY_ORDER = 2
HY_EMB_DIM = 33
HY_BANDS = (HY_EMB_DIM - 1) // 2
HY_FAST_DECAY_PCT = 0.3
HY_SLOW_DECAY_PCT = 1.5
HY_DECAY_TARGET = 1e-2
SSD_HEADDIM = 64
SSD_GROUPS = 8
SSD_STATE = 128
SSD_CONV = 5
SSD_CHUNK = 128
N_EXPERT_GROUPS = 8
TOPK_GROUPS = 4
TOP_K = 8
ROUTED_SCALE = 2.5
DEPTH = 1
DN_ALPHA = (2.0 * DEPTH) ** 0.25
LN_EPS = 1e-5
RMS_EPS = 1e-5

V7X_VMEM_LIMIT = 56 * 1024 * 1024


def _params(*sem):
    return pltpu.CompilerParams(dimension_semantics=sem, vmem_limit_bytes=V7X_VMEM_LIMIT)


def _mm_bias_kernel(a_ref, b_ref, bias_ref, o_ref):
    acc = jnp.dot(a_ref[...], b_ref[...], preferred_element_type=F32)
    o_ref[...] = (acc + bias_ref[...]).astype(o_ref.dtype)


def _matmul_bias(a, b, bias, out_dtype, tm, tn):
    m, k = a.shape
    n = b.shape[1]
    return pl.pallas_call(
        _mm_bias_kernel,
        grid=(m // tm, n // tn),
        in_specs=[pl.BlockSpec((tm, k), lambda i, j: (i, 0)),
                  pl.BlockSpec((k, tn), lambda i, j: (0, j)),
                  pl.BlockSpec((1, tn), lambda i, j: (0, j))],
        out_specs=pl.BlockSpec((tm, tn), lambda i, j: (i, j)),
        out_shape=jax.ShapeDtypeStruct((m, n), out_dtype),
        compiler_params=_params("parallel", "parallel"),
        name="proj_rows",
    )(a, b, bias)


def _mm_nt_kernel(w_ref, x_ref, bias_ref, o_ref):
    acc = lax.dot_general(w_ref[...], x_ref[0], (((1,), (1,)), ((), ())),
                          preferred_element_type=F32)
    o_ref[0] = (acc + bias_ref[...]).astype(o_ref.dtype)


def _matmul_channels_major(w_t, x, bias_col, out_dtype, tc, tl):
    c, k = w_t.shape
    bsz, seq, _ = x.shape
    return pl.pallas_call(
        _mm_nt_kernel,
        grid=(bsz, c // tc, seq // tl),
        in_specs=[pl.BlockSpec((tc, k), lambda b, i, j: (i, 0)),
                  pl.BlockSpec((1, tl, k), lambda b, i, j: (b, j, 0)),
                  pl.BlockSpec((tc, 1), lambda b, i, j: (i, 0))],
        out_specs=pl.BlockSpec((1, tc, tl), lambda b, i, j: (b, i, j)),
        out_shape=jax.ShapeDtypeStruct((bsz, c, seq), out_dtype),
        compiler_params=_params("parallel", "parallel", "parallel"),
        name="proj_channels",
    )(w_t, x, bias_col)


def _hy_mlp_kernel(w1t_ref, b1_ref, f1_ref, w2t_ref, b2_ref, f2_ref, w3t_ref, b3_ref, f3_ref,
                   o_ref, *, seq, emb_rows):
    shape = (emb_rows, 2 * seq)
    lag = lax.broadcasted_iota(jnp.int32, shape, 1) - seq
    pos = jnp.abs(lag).astype(F32)
    row = lax.broadcasted_iota(jnp.int32, shape, 0)
    t = pos / (seq - 1)
    w = (2.0 * math.pi) * pos / seq
    band = jnp.where(row <= HY_BANDS, row - 1, row - 1 - HY_BANDS).astype(F32)
    freq = 1e-4 + band * ((HY_BANDS - 1 - 1e-4) / (HY_BANDS - 1))
    ang = freq * w
    z = jnp.where(row == 0, t,
                  jnp.where(row <= HY_BANDS, jnp.cos(ang),
                            jnp.where(row <= 2 * HY_BANDS, -jnp.sin(ang), 0.0)))
    h = jnp.sin(f1_ref[...] * (jnp.dot(w1t_ref[...], z, precision=HI, preferred_element_type=F32)
                               + b1_ref[...]))
    h = jnp.sin(f2_ref[...] * (jnp.dot(w2t_ref[...], h, precision=HI, preferred_element_type=F32)
                               + b2_ref[...]))
    h = jnp.sin(f3_ref[...] * (jnp.dot(w3t_ref[...], h, precision=HI, preferred_element_type=F32)
                               + b3_ref[...]))
    o_ref[...] = h


def _dft_matrices(p):
    e = lax.broadcasted_iota(jnp.int32, (p, 2 * p), 0)
    j = lax.broadcasted_iota(jnp.int32, (p, 2 * p), 1)
    k = jnp.where(j < p, j, j - p)
    theta = ((e * k) % (2 * p)).astype(F32) * (math.pi / p)
    sign_e = jnp.where(e % 2 == 0, 1.0, -1.0)
    fwd = jnp.where(j < p, jnp.cos(theta), jnp.where(j == p, sign_e, -jnp.sin(theta)))
    jj = lax.broadcasted_iota(jnp.int32, (2 * p, p), 0)
    tt = lax.broadcasted_iota(jnp.int32, (2 * p, p), 1)
    kk = jnp.where(jj < p, jj, jj - p)
    th = ((tt * kk) % (2 * p)).astype(F32) * (math.pi / p)
    sign_t = jnp.where(tt % 2 == 0, 1.0, -1.0)
    n = 2.0 * p
    inv = jnp.where(jj == 0, 1.0 / n,
                    jnp.where(jj < p, (2.0 / n) * jnp.cos(th),
                              jnp.where(jj == p, sign_t / n, (-2.0 / n) * jnp.sin(th))))
    return fwd.astype(F32), inv.astype(F32)


def _hy_spec_kernel(h_ref, w4f_ref, w4b_ref, fwd_ref, o_ref, *, seq, p, d_hy, rows):
    nb = seq // p
    h = h_ref[...]
    kpos = jnp.dot(w4f_ref[...], h[:, seq:], precision=HI, preferred_element_type=F32)
    kneg = jnp.dot(w4b_ref[...], h[:, :seq], precision=HI, preferred_element_type=F32)
    kb0 = jnp.dot(w4b_ref[...], h[:, seq:seq + 128], precision=HI, preferred_element_type=F32)
    lane128 = lax.broadcasted_iota(jnp.int32, kb0.shape, 1)
    kb0 = jnp.sum(jnp.where(lane128 == 0, kb0, 0.0), axis=1, keepdims=True)

    r0 = pl.program_id(0) * rows
    ch = (r0 + lax.broadcasted_iota(jnp.int32, (rows, 1), 0)) % d_hy
    max_decay = math.log(HY_DECAY_TARGET) / HY_FAST_DECAY_PCT
    min_decay = math.log(HY_DECAY_TARGET) / HY_SLOW_DECAY_PCT
    delta = jnp.abs(min_decay + ch.astype(F32) * ((max_decay - min_decay) / (d_hy - 1)))
    i = lax.broadcasted_iota(jnp.int32, (1, seq), 1)
    t_pos = i.astype(F32) / (seq - 1)
    t_neg = (seq - i).astype(F32) / (seq - 1)
    kpos = (kpos + jnp.where(i == 0, kb0, 0.0)) * jnp.exp(-t_pos * delta)
    kneg = jnp.where(i == 0, 0.0, kneg * jnp.exp(-t_neg * delta))
    klag = jnp.concatenate([kneg, kpos], axis=1)

    fwd = fwd_ref[...]
    lane = lax.broadcasted_iota(jnp.int32, (1, 2 * p), 1)
    sign = jnp.where(lane % 2 == 0, 1.0, -1.0)
    spec = []
    first = []
    for m in range(2 * nb):
        blk = klag[:, m * p:(m + 1) * p]
        spec.append(jnp.dot(blk, fwd, precision=HI, preferred_element_type=F32))
        first.append(blk[:, 0:1])
    half = lax.broadcasted_iota(jnp.int32, (1, p), 1)
    for m in range(1, 2 * nb):
        tail = spec[m - 1] - jnp.where(lane <= p, first[m - 1], 0.0)
        full = spec[m] + sign * tail
        re, im = full[:, :p], full[:, p:]
        o_ref[m - 1, 0] = re.astype(o_ref.dtype)
        o_ref[m - 1, 1] = jnp.where(half == 0, 0.0, im).astype(o_ref.dtype)
        o_ref[m - 1, 2] = jnp.where(half == 0, im, re).astype(o_ref.dtype)


def _hyena_spectra(seq, p, w1, b1, fr1, w2, b2, fr2, w3, b3, fr3, w4, d_hy, rows, spec_dtype):
    hidden = w1.shape[1]
    emb_rows = -(-HY_EMB_DIM // 8) * 8
    w1t = jnp.zeros((hidden, emb_rows), F32).at[:, :HY_EMB_DIM].set(w1.T.astype(F32))
    col = lambda v: v.astype(F32).reshape(hidden, 1)
    h = pl.pallas_call(
        functools.partial(_hy_mlp_kernel, seq=seq, emb_rows=emb_rows),
        out_shape=jax.ShapeDtypeStruct((hidden, 2 * seq), F32),
        name="hyena_filter_mlp",
    )(w1t, col(b1), col(fr1), w2.T.astype(F32), col(b2), col(fr2), w3.T.astype(F32), col(b3), col(fr3))
    w4r = w4.astype(F32).reshape(hidden, HY_ORDER, 2, d_hy)
    w4f = w4r[:, :, 0].reshape(hidden, HY_ORDER * d_hy).T
    w4b = w4r[:, :, 1].reshape(hidden, HY_ORDER * d_hy).T
    fwd, _ = _dft_matrices(p)
    nb = seq // p
    n_rows = HY_ORDER * d_hy
    return pl.pallas_call(
        functools.partial(_hy_spec_kernel, seq=seq, p=p, d_hy=d_hy, rows=rows),
        grid=(n_rows // rows,),
        in_specs=[pl.BlockSpec((hidden, 2 * seq), lambda r: (0, 0)),
                  pl.BlockSpec((rows, hidden), lambda r: (r, 0)),
                  pl.BlockSpec((rows, hidden), lambda r: (r, 0)),
                  pl.BlockSpec((p, 2 * p), lambda r: (0, 0))],
        out_specs=pl.BlockSpec((2 * nb - 1, 3, rows, p), lambda r: (0, 0, r, 0)),
        out_shape=jax.ShapeDtypeStruct((2 * nb - 1, 3, n_rows, p), spec_dtype),
        compiler_params=_params("parallel"),
        name="hyena_filter_spectra",
    )(h, w4f, w4b, fwd)


def _hy_conv_kernel(g1_ref, g2_ref, v_ref, w1_ref, w2_ref, wv_ref, b1_ref, b2_ref, bv_ref,
                    k0_ref, k1_ref, s0_ref, s1_ref, fwd_ref, inv_ref, o_ref, *, p):
    bt, r, seq = v_ref.shape
    nb = seq // p
    rows = bt * r
    lane = lax.broadcasted_iota(jnp.int32, (1, 1, seq), 2)

    def short_conv(x_ref, w_ref, b_ref):
        x = x_ref[...].astype(F32)
        x2 = x.reshape(rows, seq)
        prev = jnp.where(lane == 0, 0.0, pltpu.roll(x2, 1, axis=1).reshape(bt, r, seq))
        nxt = jnp.where(lane == seq - 1, 0.0, pltpu.roll(x2, seq - 1, axis=1).reshape(bt, r, seq))
        w = w_ref[...]
        return b_ref[...] + w[:, 0:1] * prev + w[:, 1:2] * x + w[:, 2:3] * nxt

    fwd = fwd_ref[...]
    inv = inv_ref[...]

    def long_conv(u, k_ref):
        ub = u.astype(BF16).reshape(rows, seq)
        spec = [jnp.dot(ub[:, j * p:(j + 1) * p], fwd, preferred_element_type=F32).reshape(bt, r, 2 * p)
                for j in range(nb)]
        outs = []
        for i in range(nb):
            acc_re = jnp.zeros((bt, r, p), F32)
            acc_im = jnp.zeros((bt, r, p), F32)
            for j in range(nb):
                m = i - j + nb - 1
                ka = k_ref[m, 0].astype(F32)
                kb = k_ref[m, 1].astype(F32)
                kc = k_ref[m, 2].astype(F32)
                u_re = spec[j][:, :, :p]
                u_im = spec[j][:, :, p:]
                acc_re = acc_re + ka * u_re - kb * u_im
                acc_im = acc_im + kc * u_im + kb * u_re
            y_spec = jnp.concatenate([acc_re, acc_im], axis=2).astype(BF16).reshape(rows, 2 * p)
            outs.append(jnp.dot(y_spec, inv, preferred_element_type=F32).reshape(bt, r, p))
        return jnp.concatenate(outs, axis=2)

    v = short_conv(v_ref, wv_ref, bv_ref)
    z = short_conv(g1_ref, w1_ref, b1_ref) * (long_conv(v, k0_ref) + v * s0_ref[...])
    y = short_conv(g2_ref, w2_ref, b2_ref) * (long_conv(z, k1_ref) + z * s1_ref[...])
    for b in range(bt):
        o_ref[b] = y[b].T.astype(o_ref.dtype)


def _hyena_mixer(p_t, conv_w_t, conv_b_col, spectra, skip_col, d_hy, p, r, bt, out_dtype):
    bsz, _, seq = p_t.shape
    nb = seq // p
    nblk = d_hy // r
    fwd, inv = _dft_matrices(p)
    x_spec = lambda off: pl.BlockSpec((bt, r, seq), lambda j, b: (b, off * nblk + j, 0))
    w_spec = lambda off: pl.BlockSpec((r, conv_w_t.shape[1]), lambda j, b: (off * nblk + j, 0))
    c_spec = lambda off: pl.BlockSpec((r, 1), lambda j, b: (off * nblk + j, 0))
    k_spec = lambda off: pl.BlockSpec((2 * nb - 1, 3, r, p), lambda j, b: (0, 0, off * nblk + j, 0))
    return pl.pallas_call(
        functools.partial(_hy_conv_kernel, p=p),
        grid=(nblk, bsz // bt),
        in_specs=[x_spec(0), x_spec(1), x_spec(2), w_spec(0), w_spec(1), w_spec(2),
                  c_spec(0), c_spec(1), c_spec(2), k_spec(0), k_spec(1), c_spec(0), c_spec(1),
                  pl.BlockSpec((p, 2 * p), lambda j, b: (0, 0)),
                  pl.BlockSpec((2 * p, p), lambda j, b: (0, 0))],
        out_specs=pl.BlockSpec((bt, seq, r), lambda j, b: (b, 0, j)),
        out_shape=jax.ShapeDtypeStruct((bsz, seq, d_hy), out_dtype),
        compiler_params=_params("parallel", "arbitrary"),
        name="hyena_mixer",
    )(p_t, p_t, p_t, conv_w_t, conv_w_t, conv_w_t, conv_b_col, conv_b_col, conv_b_col,
      spectra, spectra, skip_col, skip_col, fwd.astype(BF16), inv.astype(BF16))


def _ssd_kernel(z_ref, x_ref, b_ref, c_ref, dt_ref, wx_ref, wb_ref, wc_ref, bx_ref, bb_ref, bc_ref,
                dtb_ref, a_ref, dskip_ref, nw_ref, o_ref,
                xs_s, xdt_s, bs_s, cs_s, a_s, y_s, st_s, *, heads):
    seq, gw = x_ref.shape[1], x_ref.shape[2]
    r = gw // SSD_HEADDIM
    q = SSD_CHUNK
    nc = seq // q
    g = pl.program_id(1)
    row = lax.broadcasted_iota(jnp.int32, (seq, 1), 0)

    def conv_silu(ref, w_ref, bias_ref):
        x = ref[0].astype(F32)
        w = w_ref[...]
        half = SSD_CONV // 2
        acc = bias_ref[...] + w[half:half + 1] * x
        for k in range(SSD_CONV):
            sh = k - half
            if sh == 0:
                continue
            shifted = pltpu.roll(x, (-sh) % seq, axis=0)
            valid = jnp.logical_and(row + sh >= 0, row + sh < seq)
            acc = acc + w[k:k + 1] * jnp.where(valid, shifted, 0.0)
        return acc * jax.nn.sigmoid(acc)

    xs = conv_silu(x_ref, wx_ref, bx_ref)
    xs_s[...] = xs
    bs_s[...] = conv_silu(b_ref, wb_ref, bb_ref).astype(BF16)
    cs_s[...] = conv_silu(c_ref, wc_ref, bc_ref).astype(BF16)

    src = lax.broadcasted_iota(jnp.int32, (128, 128), 0)
    dst = lax.broadcasted_iota(jnp.int32, (128, 128), 1)
    pick = jnp.logical_or(jnp.logical_and(dst < r, src == g * r + dst),
                          jnp.logical_and(jnp.logical_and(dst >= r, dst < 2 * r),
                                          src == heads + g * r + dst - r))
    pick = jnp.where(pick, 1.0, 0.0).astype(F32)
    e_src = lax.broadcasted_iota(jnp.int32, (128, 2 * gw), 0)
    e_dst = lax.broadcasted_iota(jnp.int32, (128, 2 * gw), 1)
    expand = jnp.where(e_src == e_dst // SSD_HEADDIM, 1.0, 0.0).astype(F32)

    raw = dt_ref[0] + dtb_ref[...]
    dt = jnp.maximum(raw, 0.0) + jnp.log1p(jnp.exp(-jnp.abs(raw)))
    a_all = dt * a_ref[...]
    dt_g = jnp.dot(dt, pick, precision=HI, preferred_element_type=F32)
    a_s[...] = jnp.dot(a_all, pick, precision=HI, preferred_element_type=F32)
    dt_x = jnp.dot(dt_g, expand, precision=HI, preferred_element_type=F32)
    xdt_s[0] = xs * dt_x[:, :gw]
    xdt_s[1] = xs * dt_x[:, gw:]
    y_s[...] = xs * dskip_ref[...]
    st_s[...] = jnp.zeros_like(st_s)

    li = lax.broadcasted_iota(jnp.int32, (q, q), 0)
    si = lax.broadcasted_iota(jnp.int32, (q, q), 1)
    lower = jnp.where(li >= si, 1.0, 0.0).astype(F32)
    upper = jnp.where(li <= si, 1.0, 0.0).astype(F32)
    lane = lax.broadcasted_iota(jnp.int32, (1, 128), 1)
    col_head = lax.broadcasted_iota(jnp.int32, (1, gw), 1) // SSD_HEADDIM
    neg_inf = jnp.float32(-jnp.inf)

    def chunk_step(i, carry):
        offs = (pl.multiple_of(i * q, q), pl.multiple_of((nc - 1 - i) * q, q))
        cum_f = jnp.dot(lower, a_s[pl.ds(offs[0], q), :], precision=HI, preferred_element_type=F32)
        cum_b = jnp.dot(upper, a_s[pl.ds(offs[1], q), :], precision=HI, preferred_element_type=F32)
        cum = jnp.where(lane < r, cum_f, cum_b)
        total = jnp.where(lane < r, cum[q - 1:q, :], cum[0:1, :])
        cum_t = cum.T
        in_x = jnp.dot(jnp.exp(cum), expand, precision=HI, preferred_element_type=F32)
        out_x = jnp.dot(jnp.exp(total - cum), expand, precision=HI, preferred_element_type=F32)
        tot_x = jnp.dot(jnp.broadcast_to(jnp.exp(total), (8, 128)), expand, precision=HI,
                        preferred_element_type=F32)[0:1]
        for d in range(2):
            off = offs[d]
            bc = bs_s[pl.ds(off, q), :]
            cc = cs_s[pl.ds(off, q), :]
            xdt = xdt_s[d, pl.ds(off, q), :]
            cb = lax.dot_general(cc, bc, (((1,), (1,)), ((), ())), preferred_element_type=F32)
            tri = (li >= si) if d == 0 else (li <= si)
            mats = []
            vals = []
            for h in range(r):
                ln = d * r + h
                seg = cum[:, ln:ln + 1] - cum_t[ln:ln + 1, :]
                mats.append((cb * jnp.exp(jnp.where(tri, seg, neg_inf))).astype(BF16))
                vals.append(jnp.where(col_head == h, xdt, 0.0).astype(BF16))
            y_diag = jnp.dot(jnp.concatenate(mats, axis=1), jnp.concatenate(vals, axis=0),
                             preferred_element_type=F32)
            sl = slice(d * gw, (d + 1) * gw)
            state = st_s[d]
            y_off = jnp.dot(cc, state.astype(BF16), preferred_element_type=F32) * in_x[:, sl]
            y_s[pl.ds(off, q), :] += y_diag + y_off
            upd = lax.dot_general(bc, (xdt * out_x[:, sl]).astype(BF16), (((0,), (0,)), ((), ())),
                                  preferred_element_type=F32)
            st_s[d] = state * tot_x[:, sl] + upd
        return carry

    lax.fori_loop(0, nc, chunk_step, 0)

    zf = z_ref[0].astype(F32)
    y = y_s[...] * (zf * jax.nn.sigmoid(zf))
    y = y * lax.rsqrt(jnp.mean(jnp.square(y), axis=-1, keepdims=True) + RMS_EPS)
    o_ref[0] = (y * nw_ref[...]).astype(o_ref.dtype)


def _ssd_mixer(pm, p_dt, conv_w, conv_b, dt_bias, a_neg, dskip_cols, norm_w, d_ssm, heads, out_dtype):
    bsz, seq, _ = pm.shape
    gw = d_ssm // SSD_GROUPS
    n = SSD_STATE
    xo, bo, co = d_ssm // gw, 2 * d_ssm // n, (2 * d_ssm + SSD_GROUPS * n) // n
    cw = lambda width, off: pl.BlockSpec((SSD_CONV, width), lambda b, g: (0, off + g))
    cb = lambda width, off: pl.BlockSpec((1, width), lambda b, g: (0, off + g))
    full = lambda: pl.BlockSpec((1, 128), lambda b, g: (0, 0))
    return pl.pallas_call(
        functools.partial(_ssd_kernel, heads=heads),
        grid=(bsz, SSD_GROUPS),
        in_specs=[pl.BlockSpec((1, seq, gw), lambda b, g: (b, 0, g)),
                  pl.BlockSpec((1, seq, gw), lambda b, g: (b, 0, xo + g)),
                  pl.BlockSpec((1, seq, n), lambda b, g: (b, 0, bo + g)),
                  pl.BlockSpec((1, seq, n), lambda b, g: (b, 0, co + g)),
                  pl.BlockSpec((1, seq, 128), lambda b, g: (b, 0, 0)),
                  cw(gw, 0), cw(n, d_ssm // n), cw(n, d_ssm // n + SSD_GROUPS),
                  cb(gw, 0), cb(n, d_ssm // n), cb(n, d_ssm // n + SSD_GROUPS),
                  full(), full(), cb(gw, 0), cb(gw, 0)],
        out_specs=pl.BlockSpec((1, seq, gw), lambda b, g: (b, 0, g)),
        out_shape=jax.ShapeDtypeStruct((bsz, seq, d_ssm), out_dtype),
        scratch_shapes=[pltpu.VMEM((seq, gw), F32), pltpu.VMEM((2, seq, gw), F32),
                        pltpu.VMEM((seq, n), BF16), pltpu.VMEM((seq, n), BF16),
                        pltpu.VMEM((seq, 128), F32), pltpu.VMEM((seq, gw), F32),
                        pltpu.VMEM((2, n, gw), F32)],
        compiler_params=_params("parallel", "parallel"),
        name="ssd_mixer",
    )(pm, pm, pm, pm, p_dt, conv_w, conv_w, conv_w, conv_b, conv_b, conv_b,
      dt_bias, a_neg, dskip_cols, norm_w)


def _layer_norm(v, g, b):
    mu = jnp.mean(v, axis=-1, keepdims=True)
    c = v - mu
    var = jnp.mean(jnp.square(c), axis=-1, keepdims=True)
    return c * lax.rsqrt(var + LN_EPS) * g + b


def _merge_kernel(yh_ref, ys_ref, gh_ref, gs_ref, x_ref, whb_ref, wsb_ref, wo_ref, bo_ref,
                  lg_ref, lb_ref, rwt_ref, h_ref, hb_ref, lt_ref):
    a = jnp.dot(yh_ref[...], whb_ref[...], preferred_element_type=F32)
    b = jnp.dot(ys_ref[...], wsb_ref[...], preferred_element_type=F32)
    m = jax.nn.sigmoid(gh_ref[...].astype(F32)) * a + jax.nn.sigmoid(gs_ref[...].astype(F32)) * b
    mix = jnp.dot(m.astype(BF16), wo_ref[...], preferred_element_type=F32) + bo_ref[...]
    h = _layer_norm(DN_ALPHA * x_ref[...] + mix, lg_ref[...], lb_ref[...])
    h_ref[...] = h
    hb_ref[...] = h.astype(BF16)
    lt_ref[...] = lax.dot_general(rwt_ref[...], h, (((1,), (1,)), ((), ())), precision=HI,
                                  preferred_element_type=F32)


def _merge(yhy, yssd, pm, gate_off, x2d, whb, wsb, wo, bo, lg, lb, rwt, tm):
    t, d = x2d.shape
    c = yhy.shape[1]
    d_ssm = yssd.shape[1]
    e = rwt.shape[0]
    row = lambda w: pl.BlockSpec((tm, w), lambda i: (i, 0))
    whole = lambda a: pl.BlockSpec(a.shape, lambda i: (0,) * a.ndim)
    return pl.pallas_call(
        _merge_kernel,
        grid=(t // tm,),
        in_specs=[row(c), row(d_ssm),
                  pl.BlockSpec((tm, d), lambda i: (i, gate_off)),
                  pl.BlockSpec((tm, d), lambda i: (i, gate_off + 1)),
                  row(d), whole(whb), whole(wsb), whole(wo), whole(bo), whole(lg), whole(lb), whole(rwt)],
        out_specs=[row(d), row(d), pl.BlockSpec((e, tm), lambda i: (0, i))],
        out_shape=[jax.ShapeDtypeStruct((t, d), F32), jax.ShapeDtypeStruct((t, d), BF16),
                   jax.ShapeDtypeStruct((e, t), F32)],
        compiler_params=_params("parallel"),
        name="merge_ln",
    )(yhy, yssd, pm, pm, x2d, whb, wsb, wo, bo, lg, lb, rwt)


def _first_max(v, axis, size):
    idx = lax.broadcasted_iota(jnp.int32, v.shape, axis)
    top = jnp.max(v, axis=axis, keepdims=True)
    first = jnp.min(jnp.where(v == top, idx, size), axis=axis, keepdims=True)
    return idx == first, top


def _route_kernel(lt_ref, bias_ref, w_ref):
    n_e, tn = lt_ref.shape
    per = n_e // N_EXPERT_GROUPS
    scores = jax.nn.sigmoid(lt_ref[...])
    biased = scores + bias_ref[...]
    b3 = biased.reshape(N_EXPERT_GROUPS, per, tn)
    hit, top1 = _first_max(b3, 1, per)
    top2 = jnp.max(jnp.where(hit, -jnp.inf, b3), axis=1, keepdims=True)
    cur = (top1 + top2).reshape(N_EXPERT_GROUPS, tn)
    grp = jnp.zeros(cur.shape, jnp.bool_)
    for _ in range(TOPK_GROUPS):
        hit, _ = _first_max(cur, 0, N_EXPERT_GROUPS)
        grp = jnp.logical_or(grp, hit)
        cur = jnp.where(hit, -jnp.inf, cur)
    cur = jnp.where(grp.reshape(N_EXPERT_GROUPS, 1, tn), b3, -jnp.inf).reshape(n_e, tn)
    chosen = jnp.zeros(cur.shape, jnp.bool_)
    for _ in range(TOP_K):
        hit, _ = _first_max(cur, 0, n_e)
        chosen = jnp.logical_or(chosen, hit)
        cur = jnp.where(hit, -jnp.inf, cur)
    w = jnp.where(chosen, scores, 0.0)
    w = w / jnp.sum(w, axis=0, keepdims=True) * ROUTED_SCALE
    w_ref[...] = w.T


def _route(logits_t, bias_col, tn):
    e, t = logits_t.shape
    return pl.pallas_call(
        _route_kernel,
        grid=(t // tn,),
        in_specs=[pl.BlockSpec((e, tn), lambda i: (0, i)), pl.BlockSpec((e, 1), lambda i: (0, 0))],
        out_specs=pl.BlockSpec((tn, e), lambda i: (i, 0)),
        out_shape=jax.ShapeDtypeStruct((t, e), F32),
        compiler_params=_params("parallel"),
        name="route_topk",
    )(logits_t, bias_col)


def _swiglu(xb, wg, wu):
    gate = jnp.dot(xb, wg, preferred_element_type=F32)
    up = jnp.dot(xb, wu, preferred_element_type=F32)
    return gate * jax.nn.sigmoid(gate) * up


def _moe_kernel(hb_ref, h_ref, wc_ref, wg_ref, wu_ref, wd_ref, sg_ref, su_ref, sd_ref,
                lg_ref, lb_ref, o_ref, acc_ref):
    e = pl.program_id(1)
    xb = hb_ref[...]

    @pl.when(e == 0)
    def _():
        mid = _swiglu(xb, sg_ref[...], su_ref[...]).astype(BF16)
        acc_ref[...] = jnp.dot(mid, sd_ref[...], preferred_element_type=F32)

    wc = wc_ref[...]
    lane = lax.broadcasted_iota(jnp.int32, wc.shape, 1)
    col = jnp.sum(jnp.where(lane == e, wc, 0.0), axis=1, keepdims=True)
    mid = (_swiglu(xb, wg_ref[0], wu_ref[0]) * col).astype(BF16)
    acc_ref[...] += jnp.dot(mid, wd_ref[0], preferred_element_type=F32)

    @pl.when(e == pl.num_programs(1) - 1)
    def _():
        o_ref[...] = _layer_norm(DN_ALPHA * h_ref[...] + acc_ref[...], lg_ref[...], lb_ref[...])


def _moe(hb, h, wc, wg, wu, wd, sg, su, sd, lg, lb, tm):
    t, d = h.shape
    n_e, _, f = wg.shape
    row = lambda w: pl.BlockSpec((tm, w), lambda i, e: (i, 0))
    whole = lambda a: pl.BlockSpec(a.shape, lambda i, e: (0,) * a.ndim)
    return pl.pallas_call(
        _moe_kernel,
        grid=(t // tm, n_e),
        in_specs=[row(d), row(d), row(n_e),
                  pl.BlockSpec((1, d, f), lambda i, e: (e, 0, 0)),
                  pl.BlockSpec((1, d, f), lambda i, e: (e, 0, 0)),
                  pl.BlockSpec((1, f, d), lambda i, e: (e, 0, 0)),
                  whole(sg), whole(su), whole(sd), whole(lg), whole(lb)],
        out_specs=row(d),
        out_shape=jax.ShapeDtypeStruct((t, d), F32),
        scratch_shapes=[pltpu.VMEM((tm, d), F32)],
        compiler_params=_params("parallel", "arbitrary"),
        name="moe_ln",
    )(hb, h, wc, wg, wu, wd, sg, su, sd, lg, lb)


def _pick(n, *cands):
    for c in cands:
        if n % c == 0:
            return c
    return n


def _layer(h, w_in, b_in, hy_conv_w, hy_conv_b, hy_f_w1, hy_f_b1, hy_f_freq1, hy_f_w2, hy_f_b2,
           hy_f_freq2, hy_f_w3, hy_f_b3, hy_f_freq3, hy_f_w4, hy_bias, ssd_conv_w, ssd_conv_b,
           ssd_dt_bias, ssd_a_log, ssd_d, ssd_norm_w, w_hy_branch, w_ssd_branch, w_out, b_out,
           ln1_g, ln1_b, router_w, router_bias, exp_w_gate, exp_w_up, exp_w_down,
           sh_w_gate, sh_w_up, sh_w_down, ln2_g, ln2_b):
    bsz, seq, d = h.shape
    t = bsz * seq
    d_hy = hy_bias.shape[-1]
    d_ssm = ssd_norm_w.shape[-1]
    conv_dim = ssd_conv_w.shape[-1]
    heads = ssd_d.shape[-1]
    hy_cols = (HY_ORDER + 1) * d_hy
    o_z, o_dt, o_gate = hy_cols, hy_cols + d_ssm + conv_dim, hy_cols + d_ssm + conv_dim + 2 * heads
    row = lambda v: v.astype(F32).reshape(1, -1)
    col = lambda v: v.astype(F32).reshape(-1, 1)

    xb = h.astype(BF16)
    x2d = xb.reshape(t, d)
    tm = _pick(t, 1024, 512, 256, 128)

    p_t = _matmul_channels_major(w_in[:, :o_z].T.astype(BF16), xb, col(b_in[:o_z]), BF16,
                                 _pick(hy_cols, 1024, 512, 384, 128), _pick(seq, 512, 256, 128))
    w_main = jnp.concatenate([w_in[:, o_z:o_dt], w_in[:, o_gate:]], axis=1).astype(BF16)
    b_main = jnp.concatenate([b_in[o_z:o_dt], b_in[o_gate:]]).astype(F32).reshape(1, -1)
    pm = _matmul_bias(x2d, w_main, b_main, BF16, tm, _pick(w_main.shape[1], 1024, 512, 256, 128))
    pad = 128 - 2 * heads
    w_dt = jnp.pad(w_in[:, o_dt:o_gate], ((0, 0), (0, pad))).astype(BF16)
    b_dt = jnp.pad(b_in[o_dt:o_gate], (0, pad)).astype(F32).reshape(1, -1)
    p_dt = _matmul_bias(x2d, w_dt, b_dt, F32, tm, 128)

    p = _pick(seq, 512, 256, 128, 64)
    r = _pick(d_hy, 128, 64)
    spectra = _hyena_spectra(seq, p, hy_f_w1, hy_f_b1, hy_f_freq1, hy_f_w2, hy_f_b2, hy_f_freq2,
                             hy_f_w3, hy_f_b3, hy_f_freq3, hy_f_w4, d_hy, r, BF16)
    y_hy = _hyena_mixer(p_t, hy_conv_w.T.astype(F32), col(hy_conv_b), spectra, col(hy_bias),
                        d_hy, p, r, _pick(bsz, 2, 1), BF16)

    y_ssd = _ssd_mixer(pm.reshape(bsz, seq, -1), p_dt.reshape(bsz, seq, 128), ssd_conv_w.astype(F32),
                       row(ssd_conv_b), jnp.pad(row(ssd_dt_bias), ((0, 0), (0, pad))),
                       jnp.pad(-jnp.exp(row(ssd_a_log)), ((0, 0), (0, pad))),
                       row(jnp.repeat(ssd_d, SSD_HEADDIM)), row(ssd_norm_w), d_ssm, heads, BF16)

    h1, h1b, logits_t = _merge(y_hy.reshape(t, d_hy), y_ssd.reshape(t, d_ssm), pm,
                               (d_ssm + conv_dim) // d, h.reshape(t, d).astype(F32),
                               w_hy_branch.astype(BF16), w_ssd_branch.astype(BF16), w_out.astype(BF16),
                               row(b_out), row(ln1_g), row(ln1_b), router_w.T.astype(F32),
                               _pick(t, 512, 256, 128))
    wc = _route(logits_t, col(router_bias), _pick(t, 512, 256, 128))
    out = _moe(h1b, h1, wc, exp_w_gate.astype(BF16), exp_w_up.astype(BF16), exp_w_down.astype(BF16),
               sh_w_gate.astype(BF16), sh_w_up.astype(BF16), sh_w_down.astype(BF16),
               row(ln2_g), row(ln2_b), tm)
    return out.reshape(bsz, seq, d)


def kernel(x, w_in, b_in, hy_conv_w, hy_conv_b, hy_f_w1, hy_f_b1, hy_f_freq1, hy_f_w2, hy_f_b2,
           hy_f_freq2, hy_f_w3, hy_f_b3, hy_f_freq3, hy_f_w4, hy_bias, ssd_conv_w, ssd_conv_b,
           ssd_dt_bias, ssd_a_log, ssd_d, ssd_norm_w, w_hy_branch, w_ssd_branch, w_out, b_out,
           ln1_g, ln1_b, router_w, router_bias, exp_w_gate, exp_w_up, exp_w_down,
           sh_w_gate, sh_w_up, sh_w_down, ln2_g, ln2_b):
    params = (w_in, b_in, hy_conv_w, hy_conv_b, hy_f_w1, hy_f_b1, hy_f_freq1, hy_f_w2, hy_f_b2,
              hy_f_freq2, hy_f_w3, hy_f_b3, hy_f_freq3, hy_f_w4, hy_bias, ssd_conv_w, ssd_conv_b,
              ssd_dt_bias, ssd_a_log, ssd_d, ssd_norm_w, w_hy_branch, w_ssd_branch, w_out, b_out,
              ln1_g, ln1_b, router_w, router_bias, exp_w_gate, exp_w_up, exp_w_down,
              sh_w_gate, sh_w_up, sh_w_down, ln2_g, ln2_b)
    h = x
    for i in range(DEPTH):
        h = _layer(h, *(p[i] for p in params))
    return h
```

```python
import functools
import math

import jax
import jax.numpy as jnp
from jax import lax
from jax.experimental import pallas as pl
from jax.experimental.pallas import tpu as pltpu

F32 = jnp.float32
BF16 = jnp.bfloat16
HI = lax.Precision.HIGHEST

HY_ORDER = 2
HY_EMB_DIM = 33
HY_BANDS = (HY_EMB_DIM - 1) // 2
HY_FAST_DECAY_PCT = 0.3
HY_SLOW_DECAY_PCT = 1.5
HY_DECAY_TARGET = 1e-2
SSD_HEADDIM = 64
SSD_GROUPS = 8
SSD_STATE = 128
SSD_CONV = 5
SSD_CHUNK = 128
N_EXPERT_GROUPS = 8
TOPK_GROUPS = 4
TOP_K = 8
ROUTED_SCALE = 2.5
DEPTH = 1
DN_ALPHA = (2.0 * DEPTH) ** 0.25
LN_EPS = 1e-5
RMS_EPS = 1e-5

V7X_VMEM_LIMIT = 56 * 1024 * 1024


def _params(*sem):
    return pltpu.CompilerParams(dimension_semantics=sem, vmem_limit_bytes=V7X_VMEM_LIMIT)


def _mm_bias_kernel(a_ref, b_ref, bias_ref, o_ref):
    acc = jnp.dot(a_ref[...], b_ref[...], preferred_element_type=F32)
    o_ref[...] = (acc + bias_ref[...]).astype(o_ref.dtype)


def _matmul_bias(a, b, bias, out_dtype, tm, tn):
    m, k = a.shape
    n = b.shape[1]
    return pl.pallas_call(
        _mm_bias_kernel,
        grid=(m // tm, n // tn),
        in_specs=[pl.BlockSpec((tm, k), lambda i, j: (i, 0)),
                  pl.BlockSpec((k, tn), lambda i, j: (0, j)),
                  pl.BlockSpec((1, tn), lambda i, j: (0, j))],
        out_specs=pl.BlockSpec((tm, tn), lambda i, j: (i, j)),
        out_shape=jax.ShapeDtypeStruct((m, n), out_dtype),
        compiler_params=_params("parallel", "parallel"),
        name="proj_rows",
    )(a, b, bias)


def _mm_nt_kernel(w_ref, x_ref, bias_ref, o_ref):
    acc = lax.dot_general(w_ref[...], x_ref[0], (((1,), (1,)), ((), ())),
                          preferred_element_type=F32)
    o_ref[0] = (acc + bias_ref[...]).astype(o_ref.dtype)


def _matmul_channels_major(w_t, x, bias_col, out_dtype, tc, tl):
    c, k = w_t.shape
    bsz, seq, _ = x.shape
    return pl.pallas_call(
        _mm_nt_kernel,
        grid=(bsz, c // tc, seq // tl),
        in_specs=[pl.BlockSpec((tc, k), lambda b, i, j: (i, 0)),
                  pl.BlockSpec((1, tl, k), lambda b, i, j: (b, j, 0)),
                  pl.BlockSpec((tc, 1), lambda b, i, j: (i, 0))],
        out_specs=pl.BlockSpec((1, tc, tl), lambda b, i, j: (b, i, j)),
        out_shape=jax.ShapeDtypeStruct((bsz, c, seq), out_dtype),
        compiler_params=_params("parallel", "parallel", "parallel"),
        name="proj_channels",
    )(w_t, x, bias_col)


def _hy_mlp_kernel(w1t_ref, b1_ref, f1_ref, w2t_ref, b2_ref, f2_ref, w3t_ref, b3_ref, f3_ref,
                   o_ref, *, seq, emb_rows):
    shape = (emb_rows, 2 * seq)
    lag = lax.broadcasted_iota(jnp.int32, shape, 1) - seq
    pos = jnp.abs(lag).astype(F32)
    row = lax.broadcasted_iota(jnp.int32, shape, 0)
    t = pos / (seq - 1)
    w = (2.0 * math.pi) * pos / seq
    band = jnp.where(row <= HY_BANDS, row - 1, row - 1 - HY_BANDS).astype(F32)
    freq = 1e-4 + band * ((HY_BANDS - 1 - 1e-4) / (HY_BANDS - 1))
    ang = freq * w
    z = jnp.where(row == 0, t,
                  jnp.where(row <= HY_BANDS, jnp.cos(ang),
                            jnp.where(row <= 2 * HY_BANDS, -jnp.sin(ang), 0.0)))
    h = jnp.sin(f1_ref[...] * (jnp.dot(w1t_ref[...], z, precision=HI, preferred_element_type=F32)
                               + b1_ref[...]))
    h = jnp.sin(f2_ref[...] * (jnp.dot(w2t_ref[...], h, precision=HI, preferred_element_type=F32)
                               + b2_ref[...]))
    h = jnp.sin(f3_ref[...] * (jnp.dot(w3t_ref[...], h, precision=HI, preferred_element_type=F32)
                               + b3_ref[...]))
    o_ref[...] = h


def _dft_matrices(p):
    e = lax.broadcasted_iota(jnp.int32, (p, 2 * p), 0)
    j = lax.broadcasted_iota(jnp.int32, (p, 2 * p), 1)
    k = jnp.where(j < p, j, j - p)
    theta = ((e * k) % (2 * p)).astype(F32) * (math.pi / p)
    sign_e = jnp.where(e % 2 == 0, 1.0, -1.0)
    fwd = jnp.where(j < p, jnp.cos(theta), jnp.where(j == p, sign_e, -jnp.sin(theta)))
    jj = lax.broadcasted_iota(jnp.int32, (2 * p, p), 0)
    tt = lax.broadcasted_iota(jnp.int32, (2 * p, p), 1)
    kk = jnp.where(jj < p, jj, jj - p)
    th = ((tt * kk) % (2 * p)).astype(F32) * (math.pi / p)
    sign_t = jnp.where(tt % 2 == 0, 1.0, -1.0)
    n = 2.0 * p
    inv = jnp.where(jj == 0, 1.0 / n,
                    jnp.where(jj < p, (2.0 / n) * jnp.cos(th),
                              jnp.where(jj == p, sign_t / n, (-2.0 / n) * jnp.sin(th))))
    return fwd.astype(F32), inv.astype(F32)


def _hy_spec_kernel(h_ref, w4f_ref, w4b_ref, fwd_ref, o_ref, *, seq, p, d_hy, rows):
    nb = seq // p
    h = h_ref[...]
    kpos = jnp.dot(w4f_ref[...], h[:, seq:], precision=HI, preferred_element_type=F32)
    kneg = jnp.dot(w4b_ref[...], h[:, :seq], precision=HI, preferred_element_type=F32)
    kb0 = jnp.dot(w4b_ref[...], h[:, seq:seq + 128], precision=HI, preferred_element_type=F32)
    lane128 = lax.broadcasted_iota(jnp.int32, kb0.shape, 1)
    kb0 = jnp.sum(jnp.where(lane128 == 0, kb0, 0.0), axis=1, keepdims=True)

    r0 = pl.program_id(0) * rows
    ch = (r0 + lax.broadcasted_iota(jnp.int32, (rows, 1), 0)) % d_hy
    max_decay = math.log(HY_DECAY_TARGET) / HY_FAST_DECAY_PCT
    min_decay = math.log(HY_DECAY_TARGET) / HY_SLOW_DECAY_PCT
    delta = jnp.abs(min_decay + ch.astype(F32) * ((max_decay - min_decay) / (d_hy - 1)))
    i = lax.broadcasted_iota(jnp.int32, (1, seq), 1)
    t_pos = i.astype(F32) / (seq - 1)
    t_neg = (seq - i).astype(F32) / (seq - 1)
    kpos = (kpos + jnp.where(i == 0, kb0, 0.0)) * jnp.exp(-t_pos * delta)
    kneg = jnp.where(i == 0, 0.0, kneg * jnp.exp(-t_neg * delta))
    klag = jnp.concatenate([kneg, kpos], axis=1)

    fwd = fwd_ref[...]
    lane = lax.broadcasted_iota(jnp.int32, (1, 2 * p), 1)
    sign = jnp.where(lane % 2 == 0, 1.0, -1.0)
    spec = []
    first = []
    for m in range(2 * nb):
        blk = klag[:, m * p:(m + 1) * p]
        spec.append(jnp.dot(blk, fwd, precision=HI, preferred_element_type=F32))
        first.append(blk[:, 0:1])
    half = lax.broadcasted_iota(jnp.int32, (1, p), 1)
    for m in range(1, 2 * nb):
        tail = spec[m - 1] - jnp.where(lane <= p, first[m - 1], 0.0)
        full = spec[m] + sign * tail
        re, im = full[:, :p], full[:, p:]
        o_ref[m - 1, 0] = re.astype(o_ref.dtype)
        o_ref[m - 1, 1] = jnp.where(half == 0, 0.0, im).astype(o_ref.dtype)
        o_ref[m - 1, 2] = jnp.where(half == 0, im, re).astype(o_ref.dtype)


def _hyena_spectra(seq, p, w1, b1, fr1, w2, b2, fr2, w3, b3, fr3, w4, d_hy, rows, spec_dtype):
    hidden = w1.shape[1]
    emb_rows = -(-HY_EMB_DIM // 8) * 8
    w1t = jnp.zeros((hidden, emb_rows), F32).at[:, :HY_EMB_DIM].set(w1.T.astype(F32))
    col = lambda v: v.astype(F32).reshape(hidden, 1)
    h = pl.pallas_call(
        functools.partial(_hy_mlp_kernel, seq=seq, emb_rows=emb_rows),
        out_shape=jax.ShapeDtypeStruct((hidden, 2 * seq), F32),
        name="hyena_filter_mlp",
    )(w1t, col(b1), col(fr1), w2.T.astype(F32), col(b2), col(fr2), w3.T.astype(F32), col(b3), col(fr3))
    w4r = w4.astype(F32).reshape(hidden, HY_ORDER, 2, d_hy)
    w4f = w4r[:, :, 0].reshape(hidden, HY_ORDER * d_hy).T
    w4b = w4r[:, :, 1].reshape(hidden, HY_ORDER * d_hy).T
    fwd, _ = _dft_matrices(p)
    nb = seq // p
    n_rows = HY_ORDER * d_hy
    return pl.pallas_call(
        functools.partial(_hy_spec_kernel, seq=seq, p=p, d_hy=d_hy, rows=rows),
        grid=(n_rows // rows,),
        in_specs=[pl.BlockSpec((hidden, 2 * seq), lambda r: (0, 0)),
                  pl.BlockSpec((rows, hidden), lambda r: (r, 0)),
                  pl.BlockSpec((rows, hidden), lambda r: (r, 0)),
                  pl.BlockSpec((p, 2 * p), lambda r: (0, 0))],
        out_specs=pl.BlockSpec((2 * nb - 1, 3, rows, p), lambda r: (0, 0, r, 0)),
        out_shape=jax.ShapeDtypeStruct((2 * nb - 1, 3, n_rows, p), spec_dtype),
        compiler_params=_params("parallel"),
        name="hyena_filter_spectra",
    )(h, w4f, w4b, fwd)


def _hy_conv_kernel(g1_ref, g2_ref, v_ref, w1_ref, w2_ref, wv_ref, b1_ref, b2_ref, bv_ref,
                    k0_ref, k1_ref, s0_ref, s1_ref, fwd_ref, inv_ref, o_ref, *, p):
    bt, r, seq = v_ref.shape
    nb = seq // p
    rows = bt * r
    lane = lax.broadcasted_iota(jnp.int32, (1, 1, seq), 2)

    def short_conv(x_ref, w_ref, b_ref):
        x = x_ref[...].astype(F32)
        x2 = x.reshape(rows, seq)
        prev = jnp.where(lane == 0, 0.0, pltpu.roll(x2, 1, axis=1).reshape(bt, r, seq))
        nxt = jnp.where(lane == seq - 1, 0.0, pltpu.roll(x2, seq - 1, axis=1).reshape(bt, r, seq))
        w = w_ref[...]
        return b_ref[...] + w[:, 0:1] * prev + w[:, 1:2] * x + w[:, 2:3] * nxt

    fwd = fwd_ref[...]
    inv = inv_ref[...]

    def long_conv(u, k_ref):
        ub = u.astype(BF16).reshape(rows, seq)
        spec = [jnp.dot(ub[:, j * p:(j + 1) * p], fwd, preferred_element_type=F32).reshape(bt, r, 2 * p)
                for j in range(nb)]
        outs = []
        for i in range(nb):
            acc_re = jnp.zeros((bt, r, p), F32)
            acc_im = jnp.zeros((bt, r, p), F32)
            for j in range(nb):
                m = i - j + nb - 1
                ka = k_ref[m, 0].astype(F32)
                kb = k_ref[m, 1].astype(F32)
                kc = k_ref[m, 2].astype(F32)
                u_re = spec[j][:, :, :p]
                u_im = spec[j][:, :, p:]
                acc_re = acc_re + ka * u_re - kb * u_im
                acc_im = acc_im + kc * u_im + kb * u_re
            y_spec = jnp.concatenate([acc_re, acc_im], axis=2).astype(BF16).reshape(rows, 2 * p)
            outs.append(jnp.dot(y_spec, inv, preferred_element_type=F32).reshape(bt, r, p))
        return jnp.concatenate(outs, axis=2)

    v = short_conv(v_ref, wv_ref, bv_ref)
    z = short_conv(g1_ref, w1_ref, b1_ref) * (long_conv(v, k0_ref) + v * s0_ref[...])
    y = short_conv(g2_ref, w2_ref, b2_ref) * (long_conv(z, k1_ref) + z * s1_ref[...])
    for b in range(bt):
        o_ref[b] = y[b].T.astype(o_ref.dtype)


def _hyena_mixer(p_t, conv_w_t, conv_b_col, spectra, skip_col, d_hy, p, r, bt, out_dtype):
    bsz, _, seq = p_t.shape
    nb = seq // p
    nblk = d_hy // r
    fwd, inv = _dft_matrices(p)
    x_spec = lambda off: pl.BlockSpec((bt, r, seq), lambda j, b: (b, off * nblk + j, 0))
    w_spec = lambda off: pl.BlockSpec((r, conv_w_t.shape[1]), lambda j, b: (off * nblk + j, 0))
    c_spec = lambda off: pl.BlockSpec((r, 1), lambda j, b: (off * nblk + j, 0))
    k_spec = lambda off: pl.BlockSpec((2 * nb - 1, 3, r, p), lambda j, b: (0, 0, off * nblk + j, 0))
    return pl.pallas_call(
        functools.partial(_hy_conv_kernel, p=p),
        grid=(nblk, bsz // bt),
        in_specs=[x_spec(0), x_spec(1), x_spec(2), w_spec(0), w_spec(1), w_spec(2),
                  c_spec(0), c_spec(1), c_spec(2), k_spec(0), k_spec(1), c_spec(0), c_spec(1),
                  pl.BlockSpec((p, 2 * p), lambda j, b: (0, 0)),
                  pl.BlockSpec((2 * p, p), lambda j, b: (0, 0))],
        out_specs=pl.BlockSpec((bt, seq, r), lambda j, b: (b, 0, j)),
        out_shape=jax.ShapeDtypeStruct((bsz, seq, d_hy), out_dtype),
        compiler_params=_params("parallel", "arbitrary"),
        name="hyena_mixer",
    )(p_t, p_t, p_t, conv_w_t, conv_w_t, conv_w_t, conv_b_col, conv_b_col, conv_b_col,
      spectra, spectra, skip_col, skip_col, fwd.astype(BF16), inv.astype(BF16))


def _ssd_prep_kernel(dt_ref, dtb_ref, a_ref, cum_ref, dtt_ref, *, r):
    seq = dt_ref.shape[1]
    q = SSD_CHUNK
    raw = dt_ref[0] + dtb_ref[...]
    dt = jnp.maximum(raw, 0.0) + jnp.log1p(jnp.exp(-jnp.abs(raw)))
    a = dt * a_ref[...]
    li = lax.broadcasted_iota(jnp.int32, (q, q), 0)
    si = lax.broadcasted_iota(jnp.int32, (q, q), 1)
    lower = jnp.where(li >= si, 1.0, 0.0).astype(F32)
    upper = jnp.where(li <= si, 1.0, 0.0).astype(F32)
    fwd_col = lax.broadcasted_iota(jnp.int32, (1, 128), 1) % (2 * r) < r
    for c in range(seq // q):
        ac = a[c * q:(c + 1) * q]
        cum = jnp.where(fwd_col,
                        jnp.dot(lower, ac, precision=HI, preferred_element_type=F32),
                        jnp.dot(upper, ac, precision=HI, preferred_element_type=F32))
        cum_ref[0, :, c * q:(c + 1) * q] = cum.T
        dtt_ref[0, :, c * q:(c + 1) * q] = dt[c * q:(c + 1) * q].T


def _ssd_prep(p_dt, dt_bias, a_neg, r):
    bsz, seq, _ = p_dt.shape
    vec = pl.BlockSpec((1, 128), lambda b: (0, 0))
    out = pl.BlockSpec((1, 128, seq), lambda b: (b, 0, 0))
    return pl.pallas_call(
        functools.partial(_ssd_prep_kernel, r=r),
        grid=(bsz,),
        in_specs=[pl.BlockSpec((1, seq, 128), lambda b: (b, 0, 0)), vec, vec],
        out_specs=[out, out],
        out_shape=[jax.ShapeDtypeStruct((bsz, 128, seq), F32)] * 2,
        compiler_params=_params("parallel"),
        name="ssd_prep",
    )(p_dt, dt_bias, a_neg)


def _ssd_kernel(z_ref, x_ref, bt_ref, c_ref, cum_ref, dtt_ref, wx_ref, wbt_ref, wc_ref,
                bx_ref, bbt_ref, bc_ref, dskip_ref, nw_ref, o_ref, xm_s, bt_s, cs_s, y_s, st_s):
    seq, gw = x_ref.shape[1], x_ref.shape[2]
    n = bt_ref.shape[1]
    r = gw // SSD_HEADDIM
    q = SSD_CHUNK
    nc = seq // q
    half = SSD_CONV // 2
    shifts = [k - half for k in range(SSD_CONV) if k != half]
    halo_r, halo_l = 16, 128

    k_r = 2 * q
    sr = lax.broadcasted_iota(jnp.int32, (len(shifts) * q, k_r), 0)
    se = lax.broadcasted_iota(jnp.int32, (len(shifts) * q, k_r), 1)
    band_r = jnp.zeros(sr.shape, jnp.bool_)
    for j, sh in enumerate(shifts):
        band_r = jnp.logical_or(band_r, jnp.logical_and(sr // q == j, se == halo_r + sr % q + sh))
    band_r = jnp.where(band_r, 1.0, 0.0).astype(BF16)
    k_l = q + 2 * halo_l
    le = lax.broadcasted_iota(jnp.int32, (k_l, len(shifts) * q), 0)
    lc = lax.broadcasted_iota(jnp.int32, (k_l, len(shifts) * q), 1)
    band_l = jnp.zeros(le.shape, jnp.bool_)
    for j, sh in enumerate(shifts):
        band_l = jnp.logical_or(band_l, jnp.logical_and(lc // q == j, le == halo_l + lc % q + sh))
    band_l = jnp.where(band_l, 1.0, 0.0).astype(BF16)

    col_head = lax.broadcasted_iota(jnp.int32, (1, gw), 1) // SSD_HEADDIM
    wx, wc, wbt = wx_ref[...], wc_ref[...], wbt_ref[...]

    def conv_rows(ref, c, off, w, bias):
        width = ref.shape[2]
        cur = ref[0, pl.ds(off, q), :]
        prev = ref[0, pl.ds(pl.multiple_of(jnp.maximum(off - halo_r, 0), halo_r), halo_r), :]
        nxt = ref[0, pl.ds(pl.multiple_of(jnp.minimum(off + q, seq - halo_r), halo_r), halo_r), :]
        prev = jnp.where(c > 0, prev, jnp.zeros_like(prev))
        nxt = jnp.where(c < nc - 1, nxt, jnp.zeros_like(nxt))
        ext = jnp.concatenate([prev, cur, nxt, jnp.zeros((k_r - q - 2 * halo_r, width), BF16)], axis=0)
        moved = jnp.dot(band_r, ext, preferred_element_type=F32)
        acc = bias + w[half:half + 1] * cur.astype(F32)
        for j, sh in enumerate(shifts):
            acc = acc + w[half + sh:half + sh + 1] * moved[j * q:(j + 1) * q]
        return acc * jax.nn.sigmoid(acc)

    def conv_step(c, carry):
        off = pl.multiple_of(c * q, q)
        xs = conv_rows(x_ref, c, off, wx, bx_ref[...])
        y_s[0, pl.ds(off, q), :] = xs * dskip_ref[...]
        for h in range(r):
            xm_s[h, pl.ds(off, q), :] = jnp.where(col_head == h, xs, 0.0).astype(BF16)
        cs_s[pl.ds(off, q), :] = conv_rows(c_ref, c, off, wc, bc_ref[...]).astype(BF16)
        cur = bt_ref[0, :, pl.ds(off, q)]
        prev = bt_ref[0, :, pl.ds(pl.multiple_of(jnp.maximum(off - halo_l, 0), halo_l), halo_l)]
        nxt = bt_ref[0, :, pl.ds(pl.multiple_of(jnp.minimum(off + q, seq - halo_l), halo_l), halo_l)]
        prev = jnp.where(c > 0, prev, jnp.zeros_like(prev))
        nxt = jnp.where(c < nc - 1, nxt, jnp.zeros_like(nxt))
        moved = jnp.dot(jnp.concatenate([prev, cur, nxt], axis=1), band_l, preferred_element_type=F32)
        acc = bbt_ref[...] + wbt[:, half:half + 1] * cur.astype(F32)
        for j, sh in enumerate(shifts):
            acc = acc + wbt[:, half + sh:half + sh + 1] * moved[:, j * q:(j + 1) * q]
        bt_s[:, pl.ds(off, q)] = (acc * jax.nn.sigmoid(acc)).astype(BF16)
        return carry

    lax.fori_loop(0, nc, conv_step, 0)
    st_s[...] = jnp.zeros_like(st_s)

    li = lax.broadcasted_iota(jnp.int32, (q, q), 0)
    si = lax.broadcasted_iota(jnp.int32, (q, q), 1)
    diag = li == si
    sub = lax.broadcasted_iota(jnp.int32, (2 * r, 1), 0)
    neg_inf = jnp.float32(-jnp.inf)
    nh = 2 * r
    assert 3 * nh + 16 <= q
    ones_rows = jnp.ones((8, q), F32)
    pad_rows = jnp.zeros((q - 3 * nh - 16, q), F32)
    pk = lax.broadcasted_iota(jnp.int32, (3 * nh + 8, nh * q), 0)
    pj = lax.broadcasted_iota(jnp.int32, (3 * nh + 8, nh * q), 1) // q
    seg_top = jnp.where(jnp.logical_and(pk < 3 * nh, pk % nh == pj), 1.0, 0.0).astype(BF16)
    seg_bottom = jnp.zeros((q - 3 * nh - 8 - 16, nh * q), BF16)

    def split3(v):
        hi = v.astype(BF16).astype(F32)
        mid = (v - hi).astype(BF16).astype(F32)
        lo = (v - hi - mid).astype(BF16).astype(F32)
        return hi, mid, lo

    def flat(v):
        return jnp.concatenate([v[j:j + 1, :] for j in range(nh)], axis=1)

    def chunk_step(i, carry):
        offs = (pl.multiple_of(i * q, q), pl.multiple_of((nc - 1 - i) * q, q))
        cum = jnp.where(sub < r, cum_ref[0, :, pl.ds(offs[0], q)], cum_ref[0, :, pl.ds(offs[1], q)])
        dt = jnp.where(sub < r, dtt_ref[0, :, pl.ds(offs[0], q)], dtt_ref[0, :, pl.ds(offs[1], q)])
        total = jnp.where(sub < r, cum[:, q - 1:q], cum[:, 0:1])
        carry_decay = jnp.exp(total)
        in_decay = jnp.exp(cum)
        out_w = jnp.exp(total - cum) * dt
        hi, mid, lo = split3(cum)
        left = jnp.concatenate([hi, mid, lo, jnp.zeros((8, q), F32), ones_rows, pad_rows], axis=0)
        left = left.T.astype(BF16)
        minus = jnp.concatenate([-flat(hi), -flat(mid), -flat(lo), jnp.zeros((13, nh * q), F32)], axis=0)
        right = jnp.concatenate([seg_top, minus.astype(BF16), seg_bottom], axis=0)
        seg_all = jnp.dot(left, right, preferred_element_type=F32)
        for d in range(2):
            off = offs[d]
            btc = bt_s[:, pl.ds(off, q)]
            cc = cs_s[pl.ds(off, q), :]
            btf = btc.astype(F32)
            cb = jnp.dot(cc, btc, preferred_element_type=F32)
            tri = (li >= si) if d == 0 else (li <= si)
            state = st_s[d]
            c_state = jnp.dot(cc, state.astype(BF16), preferred_element_type=F32)
            lhs, rhs_x, rhs_s, b_scaled = [], [], [], []
            state_decay = jnp.zeros((1, gw), F32)
            for h in range(r):
                ln = d * r + h
                seg = seg_all[:, ln * q:(ln + 1) * q]
                lhs.append((cb * jnp.exp(jnp.where(tri, seg, neg_inf)) * dt[ln:ln + 1, :]).astype(BF16))
                rhs_x.append(xm_s[h, pl.ds(off, q), :])
                rhs_s.append(jnp.where(col_head == h, c_state, 0.0).astype(BF16))
                b_scaled.append((btf * out_w[ln:ln + 1, :]).astype(BF16))
                state_decay = jnp.where(col_head == h, carry_decay[ln:ln + 1, :], state_decay)
            for h in range(r):
                ln = d * r + h
                lhs.append(jnp.where(diag, in_decay[ln:ln + 1, :], 0.0).astype(BF16))
            y_s[1 + d, pl.ds(off, q), :] = jnp.dot(jnp.concatenate(lhs, axis=1),
                                                   jnp.concatenate(rhs_x + rhs_s, axis=0),
                                                   preferred_element_type=F32)
            upd = jnp.dot(jnp.concatenate(b_scaled, axis=1), jnp.concatenate(rhs_x, axis=0),
                          preferred_element_type=F32)
            st_s[d] = state * state_decay + upd
        return carry

    lax.fori_loop(0, nc, chunk_step, 0, unroll=2)

    def gate_step(c, carry):
        off = pl.multiple_of(c * q, q)
        zf = z_ref[0, pl.ds(off, q), :].astype(F32)
        y = y_s[0, pl.ds(off, q), :] + y_s[1, pl.ds(off, q), :] + y_s[2, pl.ds(off, q), :]
        y = y * (zf * jax.nn.sigmoid(zf))
        y = y * lax.rsqrt(jnp.mean(jnp.square(y), axis=-1, keepdims=True) + RMS_EPS)
        o_ref[0, pl.ds(off, q), :] = (y * nw_ref[...]).astype(o_ref.dtype)
        return carry

    lax.fori_loop(0, nc, gate_step, 0)


def _ssd_mixer(pm, p_t, bt_off, cum_t, dt_t, conv_w, conv_b, dskip_cols, norm_w, d_ssm, out_dtype):
    bsz, seq, _ = pm.shape
    gw = d_ssm // SSD_GROUPS
    r = gw // SSD_HEADDIM
    n = SSD_STATE
    xo, co = d_ssm // gw, 2 * d_ssm // n
    cw = lambda width, off: pl.BlockSpec((SSD_CONV, width), lambda b, g: (0, off + g))
    cb = lambda width, off: pl.BlockSpec((1, width), lambda b, g: (0, off + g))
    heads_rows = pl.BlockSpec((1, 2 * r, seq), lambda b, g: (b, g, 0))
    return pl.pallas_call(
        _ssd_kernel,
        grid=(bsz, SSD_GROUPS),
        in_specs=[pl.BlockSpec((1, seq, gw), lambda b, g: (b, 0, g)),
                  pl.BlockSpec((1, seq, gw), lambda b, g: (b, 0, xo + g)),
                  pl.BlockSpec((1, n, seq), lambda b, g: (b, bt_off + g, 0)),
                  pl.BlockSpec((1, seq, n), lambda b, g: (b, 0, co + g)),
                  heads_rows, heads_rows,
                  cw(gw, 0),
                  pl.BlockSpec((n, SSD_CONV), lambda b, g: (d_ssm // n + g, 0)),
                  cw(n, d_ssm // n + SSD_GROUPS),
                  cb(gw, 0),
                  pl.BlockSpec((n, 1), lambda b, g: (d_ssm // n + g, 0)),
                  cb(n, d_ssm // n + SSD_GROUPS),
                  cb(gw, 0), cb(gw, 0)],
        out_specs=pl.BlockSpec((1, seq, gw), lambda b, g: (b, 0, g)),
        out_shape=jax.ShapeDtypeStruct((bsz, seq, d_ssm), out_dtype),
        scratch_shapes=[pltpu.VMEM((r, seq, gw), BF16),
                        pltpu.VMEM((n, seq), BF16), pltpu.VMEM((seq, n), BF16),
                        pltpu.VMEM((3, seq, gw), F32), pltpu.VMEM((2, n, gw), F32)],
        compiler_params=_params("parallel", "parallel"),
        name="ssd_mixer",
    )(pm, pm, p_t, pm, cum_t, dt_t, conv_w, conv_w.T, conv_w, conv_b, conv_b.reshape(-1, 1), conv_b,
      dskip_cols, norm_w)


def _layer_norm(v, g, b):
    mu = jnp.mean(v, axis=-1, keepdims=True)
    c = v - mu
    var = jnp.mean(jnp.square(c), axis=-1, keepdims=True)
    return c * lax.rsqrt(var + LN_EPS) * g + b


def _merge_kernel(yh_ref, ys_ref, gh_ref, gs_ref, x_ref, whb_ref, wsb_ref, wo_ref, bo_ref,
                  lg_ref, lb_ref, rwt_ref, h_ref, hb_ref, lt_ref):
    a = jnp.dot(yh_ref[...], whb_ref[...], preferred_element_type=F32)
    b = jnp.dot(ys_ref[...], wsb_ref[...], preferred_element_type=F32)
    m = jax.nn.sigmoid(gh_ref[...].astype(F32)) * a + jax.nn.sigmoid(gs_ref[...].astype(F32)) * b
    mix = jnp.dot(m.astype(BF16), wo_ref[...], preferred_element_type=F32) + bo_ref[...]
    h = _layer_norm(DN_ALPHA * x_ref[...] + mix, lg_ref[...], lb_ref[...])
    h_ref[...] = h
    hb_ref[...] = h.astype(BF16)
    lt_ref[...] = lax.dot_general(rwt_ref[...], h, (((1,), (1,)), ((), ())), precision=HI,
                                  preferred_element_type=F32)


def _merge(yhy, yssd, pm, gate_off, x2d, whb, wsb, wo, bo, lg, lb, rwt, tm):
    t, d = x2d.shape
    c = yhy.shape[1]
    d_ssm = yssd.shape[1]
    e = rwt.shape[0]
    row = lambda w: pl.BlockSpec((tm, w), lambda i: (i, 0))
    whole = lambda a: pl.BlockSpec(a.shape, lambda i: (0,) * a.ndim)
    return pl.pallas_call(
        _merge_kernel,
        grid=(t // tm,),
        in_specs=[row(c), row(d_ssm),
                  pl.BlockSpec((tm, d), lambda i: (i, gate_off)),
                  pl.BlockSpec((tm, d), lambda i: (i, gate_off + 1)),
                  row(d), whole(whb), whole(wsb), whole(wo), whole(bo), whole(lg), whole(lb), whole(rwt)],
        out_specs=[row(d), row(d), pl.BlockSpec((e, tm), lambda i: (0, i))],
        out_shape=[jax.ShapeDtypeStruct((t, d), F32), jax.ShapeDtypeStruct((t, d), BF16),
                   jax.ShapeDtypeStruct((e, t), F32)],
        compiler_params=_params("parallel"),
        name="merge_ln",
    )(yhy, yssd, pm, pm, x2d, whb, wsb, wo, bo, lg, lb, rwt)


def _first_max(v, axis, size):
    idx = lax.broadcasted_iota(jnp.int32, v.shape, axis)
    top = jnp.max(v, axis=axis, keepdims=True)
    first = jnp.min(jnp.where(v == top, idx, size), axis=axis, keepdims=True)
    return idx == first, top


def _route_kernel(lt_ref, bias_ref, w_ref):
    n_e, tn = lt_ref.shape
    per = n_e // N_EXPERT_GROUPS
    scores = jax.nn.sigmoid(lt_ref[...])
    biased = scores + bias_ref[...]
    b3 = biased.reshape(N_EXPERT_GROUPS, per, tn)
    hit, top1 = _first_max(b3, 1, per)
    top2 = jnp.max(jnp.where(hit, -jnp.inf, b3), axis=1, keepdims=True)
    cur = (top1 + top2).reshape(N_EXPERT_GROUPS, tn)
    grp = jnp.zeros(cur.shape, jnp.bool_)
    for _ in range(TOPK_GROUPS):
        hit, _ = _first_max(cur, 0, N_EXPERT_GROUPS)
        grp = jnp.logical_or(grp, hit)
        cur = jnp.where(hit, -jnp.inf, cur)
    cur = jnp.where(grp.reshape(N_EXPERT_GROUPS, 1, tn), b3, -jnp.inf).reshape(n_e, tn)
    chosen = jnp.zeros(cur.shape, jnp.bool_)
    for _ in range(TOP_K):
        hit, _ = _first_max(cur, 0, n_e)
        chosen = jnp.logical_or(chosen, hit)
        cur = jnp.where(hit, -jnp.inf, cur)
    w = jnp.where(chosen, scores, 0.0)
    w = w / jnp.sum(w, axis=0, keepdims=True) * ROUTED_SCALE
    w_ref[...] = w.T


def _route(logits_t, bias_col, tn):
    e, t = logits_t.shape
    return pl.pallas_call(
        _route_kernel,
        grid=(t // tn,),
        in_specs=[pl.BlockSpec((e, tn), lambda i: (0, i)), pl.BlockSpec((e, 1), lambda i: (0, 0))],
        out_specs=pl.BlockSpec((tn, e), lambda i: (i, 0)),
        out_shape=jax.ShapeDtypeStruct((t, e), F32),
        compiler_params=_params("parallel"),
        name="route_topk",
    )(logits_t, bias_col)


def _swiglu(xb, wg, wu):
    gate = jnp.dot(xb, wg, preferred_element_type=F32)
    up = jnp.dot(xb, wu, preferred_element_type=F32)
    return gate * jax.nn.sigmoid(gate) * up


def _moe_kernel(hb_ref, h_ref, wc_ref, wg_ref, wu_ref, wd_ref, sg_ref, su_ref, sd_ref,
                lg_ref, lb_ref, o_ref, acc_ref):
    e = pl.program_id(1)
    xb = hb_ref[...]

    @pl.when(e == 0)
    def _():
        mid = _swiglu(xb, sg_ref[...], su_ref[...]).astype(BF16)
        acc_ref[...] = jnp.dot(mid, sd_ref[...], preferred_element_type=F32)

    wc = wc_ref[...]
    lane = lax.broadcasted_iota(jnp.int32, wc.shape, 1)
    col = jnp.sum(jnp.where(lane == e, wc, 0.0), axis=1, keepdims=True)
    mid = (_swiglu(xb, wg_ref[0], wu_ref[0]) * col).astype(BF16)
    acc_ref[...] += jnp.dot(mid, wd_ref[0], preferred_element_type=F32)

    @pl.when(e == pl.num_programs(1) - 1)
    def _():
        o_ref[...] = _layer_norm(DN_ALPHA * h_ref[...] + acc_ref[...], lg_ref[...], lb_ref[...])


def _moe(hb, h, wc, wg, wu, wd, sg, su, sd, lg, lb, tm):
    t, d = h.shape
    n_e, _, f = wg.shape
    row = lambda w: pl.BlockSpec((tm, w), lambda i, e: (i, 0))
    whole = lambda a: pl.BlockSpec(a.shape, lambda i, e: (0,) * a.ndim)
    return pl.pallas_call(
        _moe_kernel,
        grid=(t // tm, n_e),
        in_specs=[row(d), row(d), row(n_e),
                  pl.BlockSpec((1, d, f), lambda i, e: (e, 0, 0)),
                  pl.BlockSpec((1, d, f), lambda i, e: (e, 0, 0)),
                  pl.BlockSpec((1, f, d), lambda i, e: (e, 0, 0)),
                  whole(sg), whole(su), whole(sd), whole(lg), whole(lb)],
        out_specs=row(d),
        out_shape=jax.ShapeDtypeStruct((t, d), F32),
        scratch_shapes=[pltpu.VMEM((tm, d), F32)],
        compiler_params=_params("parallel", "arbitrary"),
        name="moe_ln",
    )(hb, h, wc, wg, wu, wd, sg, su, sd, lg, lb)


def _pick(n, *cands):
    for c in cands:
        if n % c == 0:
            return c
    return n


def _layer(h, w_in, b_in, hy_conv_w, hy_conv_b, hy_f_w1, hy_f_b1, hy_f_freq1, hy_f_w2, hy_f_b2,
           hy_f_freq2, hy_f_w3, hy_f_b3, hy_f_freq3, hy_f_w4, hy_bias, ssd_conv_w, ssd_conv_b,
           ssd_dt_bias, ssd_a_log, ssd_d, ssd_norm_w, w_hy_branch, w_ssd_branch, w_out, b_out,
           ln1_g, ln1_b, router_w, router_bias, exp_w_gate, exp_w_up, exp_w_down,
           sh_w_gate, sh_w_up, sh_w_down, ln2_g, ln2_b):
    bsz, seq, d = h.shape
    t = bsz * seq
    d_hy = hy_bias.shape[-1]
    d_ssm = ssd_norm_w.shape[-1]
    conv_dim = ssd_conv_w.shape[-1]
    heads = ssd_d.shape[-1]
    hy_cols = (HY_ORDER + 1) * d_hy
    o_z, o_dt, o_gate = hy_cols, hy_cols + d_ssm + conv_dim, hy_cols + d_ssm + conv_dim + 2 * heads
    o_b = o_z + 2 * d_ssm
    o_c = o_b + SSD_GROUPS * SSD_STATE
    row = lambda v: v.astype(F32).reshape(1, -1)
    col = lambda v: v.astype(F32).reshape(-1, 1)

    xb = h.astype(BF16)
    x2d = xb.reshape(t, d)
    tm = _pick(t, 1024, 512, 256, 128)

    ch = lambda v: jnp.concatenate([v[..., :o_z], v[..., o_b:o_c]], axis=-1)
    p_t = _matmul_channels_major(ch(w_in).T.astype(BF16), xb, col(ch(b_in)), BF16,
                                 _pick(hy_cols + o_c - o_b, 1024, 512, 384, 128), _pick(seq, 512, 256, 128))
    tmaj = lambda v: jnp.concatenate([v[..., o_z:o_b], v[..., o_c:o_dt], v[..., o_gate:]], axis=-1)
    w_main = tmaj(w_in).astype(BF16)
    pm = _matmul_bias(x2d, w_main, row(tmaj(b_in)), BF16, tm, _pick(w_main.shape[1], 1024, 512, 256, 128))
    pad = 128 - 2 * heads
    hpg = heads // SSD_GROUPS
    perm = [dr * heads + g * hpg + hh for g in range(SSD_GROUPS) for dr in range(2) for hh in range(hpg)]
    by_group = lambda v: jnp.pad(v.astype(F32).reshape(-1)[jnp.array(perm)], (0, pad)).reshape(1, -1)
    w_dt = jnp.pad(w_in[:, o_dt:o_gate][:, jnp.array(perm)], ((0, 0), (0, pad))).astype(BF16)
    p_dt = _matmul_bias(x2d, w_dt, by_group(b_in[o_dt:o_gate]), F32, tm, 128)

    p = _pick(seq, 512, 256, 128, 64)
    r = _pick(d_hy, 128, 64)
    spectra = _hyena_spectra(seq, p, hy_f_w1, hy_f_b1, hy_f_freq1, hy_f_w2, hy_f_b2, hy_f_freq2,
                             hy_f_w3, hy_f_b3, hy_f_freq3, hy_f_w4, d_hy, r, BF16)
    y_hy = _hyena_mixer(p_t, hy_conv_w.T.astype(F32), col(hy_conv_b), spectra, col(hy_bias),
                        d_hy, p, r, _pick(bsz, 2, 1), BF16)

    cum_t, dt_t = _ssd_prep(p_dt.reshape(bsz, seq, 128), by_group(ssd_dt_bias),
                            by_group(-jnp.exp(ssd_a_log.astype(F32))), hpg)
    y_ssd = _ssd_mixer(pm.reshape(bsz, seq, -1), p_t, hy_cols // SSD_STATE, cum_t, dt_t,
                       ssd_conv_w.astype(F32), row(ssd_conv_b),
                       row(jnp.repeat(ssd_d, SSD_HEADDIM)), row(ssd_norm_w), d_ssm, BF16)

    h1, h1b, logits_t = _merge(y_hy.reshape(t, d_hy), y_ssd.reshape(t, d_ssm), pm,
                               (o_dt - o_z - (o_c - o_b)) // d, h.reshape(t, d).astype(F32),
                               w_hy_branch.astype(BF16), w_ssd_branch.astype(BF16), w_out.astype(BF16),
                               row(b_out), row(ln1_g), row(ln1_b), router_w.T.astype(F32),
                               _pick(t, 512, 256, 128))
    wc = _route(logits_t, col(router_bias), _pick(t, 512, 256, 128))
    out = _moe(h1b, h1, wc, exp_w_gate.astype(BF16), exp_w_up.astype(BF16), exp_w_down.astype(BF16),
               sh_w_gate.astype(BF16), sh_w_up.astype(BF16), sh_w_down.astype(BF16),
               row(ln2_g), row(ln2_b), tm)
    return out.reshape(bsz, seq, d)


def kernel(x, w_in, b_in, hy_conv_w, hy_conv_b, hy_f_w1, hy_f_b1, hy_f_freq1, hy_f_w2, hy_f_b2,
           hy_f_freq2, hy_f_w3, hy_f_b3, hy_f_freq3, hy_f_w4, hy_bias, ssd_conv_w, ssd_conv_b,
           ssd_dt_bias, ssd_a_log, ssd_d, ssd_norm_w, w_hy_branch, w_ssd_branch, w_out, b_out,
           ln1_g, ln1_b, router_w, router_bias, exp_w_gate, exp_w_up, exp_w_down,
           sh_w_gate, sh_w_up, sh_w_down, ln2_g, ln2_b):
    params = (w_in, b_in, hy_conv_w, hy_conv_b, hy_f_w1, hy_f_b1, hy_f_freq1, hy_f_w2, hy_f_b2,
              hy_f_freq2, hy_f_w3, hy_f_b3, hy_f_freq3, hy_f_w4, hy_bias, ssd_conv_w, ssd_conv_b,
              ssd_dt_bias, ssd_a_log, ssd_d, ssd_norm_w, w_hy_branch, w_ssd_branch, w_out, b_out,
              ln1_g, ln1_b, router_w, router_bias, exp_w_gate, exp_w_up, exp_w_down,
              sh_w_gate, sh_w_up, sh_w_down, ln2_g, ln2_b)
    h = x
    for i in range(DEPTH):
        h = _layer(h, *(p[i] for p in params))
    return h
```

```python
import functools
import math

import jax
import jax.numpy as jnp
from jax import lax
from jax.experimental import pallas as pl
from jax.experimental.pallas import tpu as pltpu
from jax.experimental.pallas import tpu_sc as plsc

F32 = jnp.float32
BF16 = jnp.bfloat16
HI = lax.Precision.HIGHEST

HY_ORDER = 2
HY_EMB_DIM = 33
HY_BANDS = (HY_EMB_DIM - 1) // 2
HY_FAST_DECAY_PCT = 0.3
HY_SLOW_DECAY_PCT = 1.5
HY_DECAY_TARGET = 1e-2
SSD_HEADDIM = 64
SSD_GROUPS = 8
SSD_STATE = 128
SSD_CONV = 5
SSD_CHUNK = 128
N_EXPERT_GROUPS = 8
TOPK_GROUPS = 4
TOP_K = 8
ROUTED_SCALE = 2.5
DEPTH = 1
DN_ALPHA = (2.0 * DEPTH) ** 0.25
LN_EPS = 1e-5
RMS_EPS = 1e-5

V7X_VMEM_LIMIT = 56 * 1024 * 1024


def _params(*sem):
    return pltpu.CompilerParams(dimension_semantics=sem, vmem_limit_bytes=V7X_VMEM_LIMIT)


def _mm_bias_kernel(a_ref, b_ref, bias_ref, o_ref):
    acc = jnp.dot(a_ref[...], b_ref[...], preferred_element_type=F32)
    o_ref[...] = (acc + bias_ref[...]).astype(o_ref.dtype)


def _matmul_bias(a, b, bias, out_dtype, tm, tn):
    m, k = a.shape
    n = b.shape[1]
    return pl.pallas_call(
        _mm_bias_kernel,
        grid=(m // tm, n // tn),
        in_specs=[pl.BlockSpec((tm, k), lambda i, j: (i, 0)),
                  pl.BlockSpec((k, tn), lambda i, j: (0, j)),
                  pl.BlockSpec((1, tn), lambda i, j: (0, j))],
        out_specs=pl.BlockSpec((tm, tn), lambda i, j: (i, j)),
        out_shape=jax.ShapeDtypeStruct((m, n), out_dtype),
        compiler_params=_params("parallel", "parallel"),
        name="proj_rows",
    )(a, b, bias)


def _mm_nt_kernel(w_ref, x_ref, bias_ref, o_ref):
    acc = lax.dot_general(w_ref[...], x_ref[0], (((1,), (1,)), ((), ())),
                          preferred_element_type=F32)
    o_ref[0] = (acc + bias_ref[...]).astype(o_ref.dtype)


def _matmul_channels_major(w_t, x, bias_col, out_dtype, tc, tl):
    c, k = w_t.shape
    bsz, seq, _ = x.shape
    return pl.pallas_call(
        _mm_nt_kernel,
        grid=(bsz, c // tc, seq // tl),
        in_specs=[pl.BlockSpec((tc, k), lambda b, i, j: (i, 0)),
                  pl.BlockSpec((1, tl, k), lambda b, i, j: (b, j, 0)),
                  pl.BlockSpec((tc, 1), lambda b, i, j: (i, 0))],
        out_specs=pl.BlockSpec((1, tc, tl), lambda b, i, j: (b, i, j)),
        out_shape=jax.ShapeDtypeStruct((bsz, c, seq), out_dtype),
        compiler_params=_params("parallel", "parallel", "parallel"),
        name="proj_channels",
    )(w_t, x, bias_col)


def _hy_mlp_kernel(w1t_ref, b1_ref, f1_ref, w2t_ref, b2_ref, f2_ref, w3t_ref, b3_ref, f3_ref,
                   o_ref, *, seq, emb_rows):
    shape = (emb_rows, 2 * seq)
    lag = lax.broadcasted_iota(jnp.int32, shape, 1) - seq
    pos = jnp.abs(lag).astype(F32)
    row = lax.broadcasted_iota(jnp.int32, shape, 0)
    t = pos / (seq - 1)
    w = (2.0 * math.pi) * pos / seq
    band = jnp.where(row <= HY_BANDS, row - 1, row - 1 - HY_BANDS).astype(F32)
    freq = 1e-4 + band * ((HY_BANDS - 1 - 1e-4) / (HY_BANDS - 1))
    ang = freq * w
    z = jnp.where(row == 0, t,
                  jnp.where(row <= HY_BANDS, jnp.cos(ang),
                            jnp.where(row <= 2 * HY_BANDS, -jnp.sin(ang), 0.0)))
    h = jnp.sin(f1_ref[...] * (jnp.dot(w1t_ref[...], z, precision=HI, preferred_element_type=F32)
                               + b1_ref[...]))
    h = jnp.sin(f2_ref[...] * (jnp.dot(w2t_ref[...], h, precision=HI, preferred_element_type=F32)
                               + b2_ref[...]))
    h = jnp.sin(f3_ref[...] * (jnp.dot(w3t_ref[...], h, precision=HI, preferred_element_type=F32)
                               + b3_ref[...]))
    o_ref[...] = h


def _dft_matrices(p):
    e = lax.broadcasted_iota(jnp.int32, (p, 2 * p), 0)
    j = lax.broadcasted_iota(jnp.int32, (p, 2 * p), 1)
    k = jnp.where(j < p, j, j - p)
    theta = ((e * k) % (2 * p)).astype(F32) * (math.pi / p)
    sign_e = jnp.where(e % 2 == 0, 1.0, -1.0)
    fwd = jnp.where(j < p, jnp.cos(theta), jnp.where(j == p, sign_e, -jnp.sin(theta)))
    jj = lax.broadcasted_iota(jnp.int32, (2 * p, p), 0)
    tt = lax.broadcasted_iota(jnp.int32, (2 * p, p), 1)
    kk = jnp.where(jj < p, jj, jj - p)
    th = ((tt * kk) % (2 * p)).astype(F32) * (math.pi / p)
    sign_t = jnp.where(tt % 2 == 0, 1.0, -1.0)
    n = 2.0 * p
    inv = jnp.where(jj == 0, 1.0 / n,
                    jnp.where(jj < p, (2.0 / n) * jnp.cos(th),
                              jnp.where(jj == p, sign_t / n, (-2.0 / n) * jnp.sin(th))))
    return fwd.astype(F32), inv.astype(F32)


def _hy_spec_kernel(h_ref, w4f_ref, w4b_ref, fwd_ref, o_ref, *, seq, p, d_hy, rows):
    nb = seq // p
    h = h_ref[...]
    kpos = jnp.dot(w4f_ref[...], h[:, seq:], precision=HI, preferred_element_type=F32)
    kneg = jnp.dot(w4b_ref[...], h[:, :seq], precision=HI, preferred_element_type=F32)
    kb0 = jnp.dot(w4b_ref[...], h[:, seq:seq + 128], precision=HI, preferred_element_type=F32)
    lane128 = lax.broadcasted_iota(jnp.int32, kb0.shape, 1)
    kb0 = jnp.sum(jnp.where(lane128 == 0, kb0, 0.0), axis=1, keepdims=True)

    r0 = pl.program_id(0) * rows
    ch = (r0 + lax.broadcasted_iota(jnp.int32, (rows, 1), 0)) % d_hy
    max_decay = math.log(HY_DECAY_TARGET) / HY_FAST_DECAY_PCT
    min_decay = math.log(HY_DECAY_TARGET) / HY_SLOW_DECAY_PCT
    delta = jnp.abs(min_decay + ch.astype(F32) * ((max_decay - min_decay) / (d_hy - 1)))
    i = lax.broadcasted_iota(jnp.int32, (1, seq), 1)
    t_pos = i.astype(F32) / (seq - 1)
    t_neg = (seq - i).astype(F32) / (seq - 1)
    kpos = (kpos + jnp.where(i == 0, kb0, 0.0)) * jnp.exp(-t_pos * delta)
    kneg = jnp.where(i == 0, 0.0, kneg * jnp.exp(-t_neg * delta))
    klag = jnp.concatenate([kneg, kpos], axis=1)

    fwd = fwd_ref[...]
    lane = lax.broadcasted_iota(jnp.int32, (1, 2 * p), 1)
    sign = jnp.where(lane % 2 == 0, 1.0, -1.0)
    spec = []
    first = []
    for m in range(2 * nb):
        blk = klag[:, m * p:(m + 1) * p]
        spec.append(jnp.dot(blk, fwd, precision=HI, preferred_element_type=F32))
        first.append(blk[:, 0:1])
    half = lax.broadcasted_iota(jnp.int32, (1, p), 1)
    for m in range(1, 2 * nb):
        tail = spec[m - 1] - jnp.where(lane <= p, first[m - 1], 0.0)
        full = spec[m] + sign * tail
        re, im = full[:, :p], full[:, p:]
        o_ref[m - 1, 0] = re.astype(o_ref.dtype)
        o_ref[m - 1, 1] = jnp.where(half == 0, 0.0, im).astype(o_ref.dtype)
        o_ref[m - 1, 2] = jnp.where(half == 0, im, re).astype(o_ref.dtype)


def _hyena_spectra(seq, p, w1, b1, fr1, w2, b2, fr2, w3, b3, fr3, w4, d_hy, rows, spec_dtype):
    hidden = w1.shape[1]
    emb_rows = -(-HY_EMB_DIM // 8) * 8
    w1t = jnp.zeros((hidden, emb_rows), F32).at[:, :HY_EMB_DIM].set(w1.T.astype(F32))
    col = lambda v: v.astype(F32).reshape(hidden, 1)
    h = pl.pallas_call(
        functools.partial(_hy_mlp_kernel, seq=seq, emb_rows=emb_rows),
        out_shape=jax.ShapeDtypeStruct((hidden, 2 * seq), F32),
        name="hyena_filter_mlp",
    )(w1t, col(b1), col(fr1), w2.T.astype(F32), col(b2), col(fr2), w3.T.astype(F32), col(b3), col(fr3))
    w4r = w4.astype(F32).reshape(hidden, HY_ORDER, 2, d_hy)
    w4f = w4r[:, :, 0].reshape(hidden, HY_ORDER * d_hy).T
    w4b = w4r[:, :, 1].reshape(hidden, HY_ORDER * d_hy).T
    fwd, _ = _dft_matrices(p)
    nb = seq // p
    n_rows = HY_ORDER * d_hy
    return pl.pallas_call(
        functools.partial(_hy_spec_kernel, seq=seq, p=p, d_hy=d_hy, rows=rows),
        grid=(n_rows // rows,),
        in_specs=[pl.BlockSpec((hidden, 2 * seq), lambda r: (0, 0)),
                  pl.BlockSpec((rows, hidden), lambda r: (r, 0)),
                  pl.BlockSpec((rows, hidden), lambda r: (r, 0)),
                  pl.BlockSpec((p, 2 * p), lambda r: (0, 0))],
        out_specs=pl.BlockSpec((2 * nb - 1, 3, rows, p), lambda r: (0, 0, r, 0)),
        out_shape=jax.ShapeDtypeStruct((2 * nb - 1, 3, n_rows, p), spec_dtype),
        compiler_params=_params("parallel"),
        name="hyena_filter_spectra",
    )(h, w4f, w4b, fwd)


def _hy_conv_kernel(g1_ref, g2_ref, v_ref, w1_ref, w2_ref, wv_ref, b1_ref, b2_ref, bv_ref,
                    k0_ref, k1_ref, s0_ref, s1_ref, fwd_ref, inv_ref, o_ref, *, p):
    bt, r, seq = v_ref.shape
    nb = seq // p
    rows = bt * r
    lane = lax.broadcasted_iota(jnp.int32, (1, 1, seq), 2)

    def short_conv(x_ref, w_ref, b_ref):
        x = x_ref[...].astype(F32)
        x2 = x.reshape(rows, seq)
        prev = jnp.where(lane == 0, 0.0, pltpu.roll(x2, 1, axis=1).reshape(bt, r, seq))
        nxt = jnp.where(lane == seq - 1, 0.0, pltpu.roll(x2, seq - 1, axis=1).reshape(bt, r, seq))
        w = w_ref[...]
        return b_ref[...] + w[:, 0:1] * prev + w[:, 1:2] * x + w[:, 2:3] * nxt

    fwd = fwd_ref[...]
    inv = inv_ref[...]

    def long_conv(u, k_ref):
        ub = u.astype(BF16).reshape(rows, seq)
        spec = [jnp.dot(ub[:, j * p:(j + 1) * p], fwd, preferred_element_type=F32).reshape(bt, r, 2 * p)
                for j in range(nb)]
        outs = []
        for i in range(nb):
            acc_re = jnp.zeros((bt, r, p), F32)
            acc_im = jnp.zeros((bt, r, p), F32)
            for j in range(nb):
                m = i - j + nb - 1
                ka = k_ref[m, 0].astype(F32)
                kb = k_ref[m, 1].astype(F32)
                kc = k_ref[m, 2].astype(F32)
                u_re = spec[j][:, :, :p]
                u_im = spec[j][:, :, p:]
                acc_re = acc_re + ka * u_re - kb * u_im
                acc_im = acc_im + kc * u_im + kb * u_re
            y_spec = jnp.concatenate([acc_re, acc_im], axis=2).astype(BF16).reshape(rows, 2 * p)
            outs.append(jnp.dot(y_spec, inv, preferred_element_type=F32).reshape(bt, r, p))
        return jnp.concatenate(outs, axis=2)

    v = short_conv(v_ref, wv_ref, bv_ref)
    z = short_conv(g1_ref, w1_ref, b1_ref) * (long_conv(v, k0_ref) + v * s0_ref[...])
    y = short_conv(g2_ref, w2_ref, b2_ref) * (long_conv(z, k1_ref) + z * s1_ref[...])
    for b in range(bt):
        o_ref[b] = y[b].T.astype(o_ref.dtype)


def _hyena_mixer(p_t, conv_w_t, conv_b_col, spectra, skip_col, d_hy, p, r, bt, out_dtype):
    bsz, _, seq = p_t.shape
    nb = seq // p
    nblk = d_hy // r
    fwd, inv = _dft_matrices(p)
    x_spec = lambda off: pl.BlockSpec((bt, r, seq), lambda j, b: (b, off * nblk + j, 0))
    w_spec = lambda off: pl.BlockSpec((r, conv_w_t.shape[1]), lambda j, b: (off * nblk + j, 0))
    c_spec = lambda off: pl.BlockSpec((r, 1), lambda j, b: (off * nblk + j, 0))
    k_spec = lambda off: pl.BlockSpec((2 * nb - 1, 3, r, p), lambda j, b: (0, 0, off * nblk + j, 0))
    return pl.pallas_call(
        functools.partial(_hy_conv_kernel, p=p),
        grid=(nblk, bsz // bt),
        in_specs=[x_spec(0), x_spec(1), x_spec(2), w_spec(0), w_spec(1), w_spec(2),
                  c_spec(0), c_spec(1), c_spec(2), k_spec(0), k_spec(1), c_spec(0), c_spec(1),
                  pl.BlockSpec((p, 2 * p), lambda j, b: (0, 0)),
                  pl.BlockSpec((2 * p, p), lambda j, b: (0, 0))],
        out_specs=pl.BlockSpec((bt, seq, r), lambda j, b: (b, 0, j)),
        out_shape=jax.ShapeDtypeStruct((bsz, seq, d_hy), out_dtype),
        compiler_params=_params("parallel", "arbitrary"),
        name="hyena_mixer",
    )(p_t, p_t, p_t, conv_w_t, conv_w_t, conv_w_t, conv_b_col, conv_b_col, conv_b_col,
      spectra, spectra, skip_col, skip_col, fwd.astype(BF16), inv.astype(BF16))


def _ssd_prep_kernel(dt_ref, dtb_ref, a_ref, cum_ref, dtt_ref, *, r):
    seq = dt_ref.shape[1]
    q = SSD_CHUNK
    raw = dt_ref[0] + dtb_ref[...]
    dt = jnp.maximum(raw, 0.0) + jnp.log1p(jnp.exp(-jnp.abs(raw)))
    a = dt * a_ref[...]
    li = lax.broadcasted_iota(jnp.int32, (q, q), 0)
    si = lax.broadcasted_iota(jnp.int32, (q, q), 1)
    lower = jnp.where(li >= si, 1.0, 0.0).astype(F32)
    upper = jnp.where(li <= si, 1.0, 0.0).astype(F32)
    fwd_col = lax.broadcasted_iota(jnp.int32, (1, 128), 1) % (2 * r) < r
    for c in range(seq // q):
        ac = a[c * q:(c + 1) * q]
        cum = jnp.where(fwd_col,
                        jnp.dot(lower, ac, precision=HI, preferred_element_type=F32),
                        jnp.dot(upper, ac, precision=HI, preferred_element_type=F32))
        cum_ref[0, :, c * q:(c + 1) * q] = cum.T
        dtt_ref[0, :, c * q:(c + 1) * q] = dt[c * q:(c + 1) * q].T


def _ssd_prep(p_dt, dt_bias, a_neg, r):
    bsz, seq, _ = p_dt.shape
    vec = pl.BlockSpec((1, 128), lambda b: (0, 0))
    out = pl.BlockSpec((1, 128, seq), lambda b: (b, 0, 0))
    return pl.pallas_call(
        functools.partial(_ssd_prep_kernel, r=r),
        grid=(bsz,),
        in_specs=[pl.BlockSpec((1, seq, 128), lambda b: (b, 0, 0)), vec, vec],
        out_specs=[out, out],
        out_shape=[jax.ShapeDtypeStruct((bsz, 128, seq), F32)] * 2,
        compiler_params=_params("parallel"),
        name="ssd_prep",
    )(p_dt, dt_bias, a_neg)


def _ssd_kernel(z_ref, x_ref, bt_ref, c_ref, cum_ref, dtt_ref, wx_ref, wbt_ref, wc_ref,
                bx_ref, bbt_ref, bc_ref, dskip_ref, nw_ref, o_ref, xm_s, bt_s, cs_s, y_s, st_s):
    seq, gw = x_ref.shape[1], x_ref.shape[2]
    n = bt_ref.shape[1]
    r = gw // SSD_HEADDIM
    q = SSD_CHUNK
    nc = seq // q
    half = SSD_CONV // 2
    shifts = [k - half for k in range(SSD_CONV) if k != half]
    halo_r, halo_l = 16, 128

    k_r = 2 * q
    sr = lax.broadcasted_iota(jnp.int32, (len(shifts) * q, k_r), 0)
    se = lax.broadcasted_iota(jnp.int32, (len(shifts) * q, k_r), 1)
    band_r = jnp.zeros(sr.shape, jnp.bool_)
    for j, sh in enumerate(shifts):
        band_r = jnp.logical_or(band_r, jnp.logical_and(sr // q == j, se == halo_r + sr % q + sh))
    band_r = jnp.where(band_r, 1.0, 0.0).astype(BF16)
    k_l = q + 2 * halo_l
    le = lax.broadcasted_iota(jnp.int32, (k_l, len(shifts) * q), 0)
    lc = lax.broadcasted_iota(jnp.int32, (k_l, len(shifts) * q), 1)
    band_l = jnp.zeros(le.shape, jnp.bool_)
    for j, sh in enumerate(shifts):
        band_l = jnp.logical_or(band_l, jnp.logical_and(lc // q == j, le == halo_l + lc % q + sh))
    band_l = jnp.where(band_l, 1.0, 0.0).astype(BF16)

    col_head = lax.broadcasted_iota(jnp.int32, (1, gw), 1) // SSD_HEADDIM
    wx, wc, wbt = wx_ref[...], wc_ref[...], wbt_ref[...]

    def conv_rows(ref, c, off, w, bias):
        width = ref.shape[2]
        cur = ref[0, pl.ds(off, q), :]
        prev = ref[0, pl.ds(pl.multiple_of(jnp.maximum(off - halo_r, 0), halo_r), halo_r), :]
        nxt = ref[0, pl.ds(pl.multiple_of(jnp.minimum(off + q, seq - halo_r), halo_r), halo_r), :]
        prev = jnp.where(c > 0, prev, jnp.zeros_like(prev))
        nxt = jnp.where(c < nc - 1, nxt, jnp.zeros_like(nxt))
        ext = jnp.concatenate([prev, cur, nxt, jnp.zeros((k_r - q - 2 * halo_r, width), BF16)], axis=0)
        moved = jnp.dot(band_r, ext, preferred_element_type=F32)
        acc = bias + w[half:half + 1] * cur.astype(F32)
        for j, sh in enumerate(shifts):
            acc = acc + w[half + sh:half + sh + 1] * moved[j * q:(j + 1) * q]
        return acc * jax.nn.sigmoid(acc)

    def conv_step(c, carry):
        off = pl.multiple_of(c * q, q)
        xs = conv_rows(x_ref, c, off, wx, bx_ref[...])
        y_s[0, pl.ds(off, q), :] = xs * dskip_ref[...]
        for h in range(r):
            xm_s[h, pl.ds(off, q), :] = jnp.where(col_head == h, xs, 0.0).astype(BF16)
        cs_s[pl.ds(off, q), :] = conv_rows(c_ref, c, off, wc, bc_ref[...]).astype(BF16)
        cur = bt_ref[0, :, pl.ds(off, q)]
        prev = bt_ref[0, :, pl.ds(pl.multiple_of(jnp.maximum(off - halo_l, 0), halo_l), halo_l)]
        nxt = bt_ref[0, :, pl.ds(pl.multiple_of(jnp.minimum(off + q, seq - halo_l), halo_l), halo_l)]
        prev = jnp.where(c > 0, prev, jnp.zeros_like(prev))
        nxt = jnp.where(c < nc - 1, nxt, jnp.zeros_like(nxt))
        moved = jnp.dot(jnp.concatenate([prev, cur, nxt], axis=1), band_l, preferred_element_type=F32)
        acc = bbt_ref[...] + wbt[:, half:half + 1] * cur.astype(F32)
        for j, sh in enumerate(shifts):
            acc = acc + wbt[:, half + sh:half + sh + 1] * moved[:, j * q:(j + 1) * q]
        bt_s[:, pl.ds(off, q)] = (acc * jax.nn.sigmoid(acc)).astype(BF16)
        return carry

    lax.fori_loop(0, nc, conv_step, 0)
    st_s[...] = jnp.zeros_like(st_s)

    li = lax.broadcasted_iota(jnp.int32, (q, q), 0)
    si = lax.broadcasted_iota(jnp.int32, (q, q), 1)
    diag = li == si
    sub = lax.broadcasted_iota(jnp.int32, (2 * r, 1), 0)
    neg_inf = jnp.float32(-jnp.inf)
    nh = 2 * r
    assert 3 * nh + 16 <= q
    ones_rows = jnp.ones((8, q), F32)
    pad_rows = jnp.zeros((q - 3 * nh - 16, q), F32)
    pk = lax.broadcasted_iota(jnp.int32, (3 * nh + 8, nh * q), 0)
    pj = lax.broadcasted_iota(jnp.int32, (3 * nh + 8, nh * q), 1) // q
    seg_top = jnp.where(jnp.logical_and(pk < 3 * nh, pk % nh == pj), 1.0, 0.0).astype(BF16)
    seg_bottom = jnp.zeros((q - 3 * nh - 8 - 16, nh * q), BF16)

    def split3(v):
        hi = v.astype(BF16).astype(F32)
        mid = (v - hi).astype(BF16).astype(F32)
        lo = (v - hi - mid).astype(BF16).astype(F32)
        return hi, mid, lo

    def flat(v):
        return jnp.concatenate([v[j:j + 1, :] for j in range(nh)], axis=1)

    def chunk_step(i, carry):
        offs = (pl.multiple_of(i * q, q), pl.multiple_of((nc - 1 - i) * q, q))
        cum = jnp.where(sub < r, cum_ref[0, :, pl.ds(offs[0], q)], cum_ref[0, :, pl.ds(offs[1], q)])
        dt = jnp.where(sub < r, dtt_ref[0, :, pl.ds(offs[0], q)], dtt_ref[0, :, pl.ds(offs[1], q)])
        total = jnp.where(sub < r, cum[:, q - 1:q], cum[:, 0:1])
        carry_decay = jnp.exp(total)
        in_decay = jnp.exp(cum)
        out_w = jnp.exp(total - cum) * dt
        hi, mid, lo = split3(cum)
        left = jnp.concatenate([hi, mid, lo, jnp.zeros((8, q), F32), ones_rows, pad_rows], axis=0)
        left = left.T.astype(BF16)
        minus = jnp.concatenate([-flat(hi), -flat(mid), -flat(lo), jnp.zeros((13, nh * q), F32)], axis=0)
        right = jnp.concatenate([seg_top, minus.astype(BF16), seg_bottom], axis=0)
        seg_all = jnp.dot(left, right, preferred_element_type=F32)
        for d in range(2):
            off = offs[d]
            btc = bt_s[:, pl.ds(off, q)]
            cc = cs_s[pl.ds(off, q), :]
            btf = btc.astype(F32)
            cb = jnp.dot(cc, btc, preferred_element_type=F32)
            tri = (li >= si) if d == 0 else (li <= si)
            state = st_s[d]
            c_state = jnp.dot(cc, state.astype(BF16), preferred_element_type=F32)
            lhs, rhs_x, rhs_s, b_scaled = [], [], [], []
            state_decay = jnp.zeros((1, gw), F32)
            for h in range(r):
                ln = d * r + h
                seg = seg_all[:, ln * q:(ln + 1) * q]
                lhs.append((cb * jnp.exp(jnp.where(tri, seg, neg_inf)) * dt[ln:ln + 1, :]).astype(BF16))
                rhs_x.append(xm_s[h, pl.ds(off, q), :])
                rhs_s.append(jnp.where(col_head == h, c_state, 0.0).astype(BF16))
                b_scaled.append((btf * out_w[ln:ln + 1, :]).astype(BF16))
                state_decay = jnp.where(col_head == h, carry_decay[ln:ln + 1, :], state_decay)
            for h in range(r):
                ln = d * r + h
                lhs.append(jnp.where(diag, in_decay[ln:ln + 1, :], 0.0).astype(BF16))
            y_s[1 + d, pl.ds(off, q), :] = jnp.dot(jnp.concatenate(lhs, axis=1),
                                                   jnp.concatenate(rhs_x + rhs_s, axis=0),
                                                   preferred_element_type=F32)
            upd = jnp.dot(jnp.concatenate(b_scaled, axis=1), jnp.concatenate(rhs_x, axis=0),
                          preferred_element_type=F32)
            st_s[d] = state * state_decay + upd
        return carry

    lax.fori_loop(0, nc, chunk_step, 0, unroll=2)

    def gate_step(c, carry):
        off = pl.multiple_of(c * q, q)
        zf = z_ref[0, pl.ds(off, q), :].astype(F32)
        y = y_s[0, pl.ds(off, q), :] + y_s[1, pl.ds(off, q), :] + y_s[2, pl.ds(off, q), :]
        y = y * (zf * jax.nn.sigmoid(zf))
        y = y * lax.rsqrt(jnp.mean(jnp.square(y), axis=-1, keepdims=True) + RMS_EPS)
        o_ref[0, pl.ds(off, q), :] = (y * nw_ref[...]).astype(o_ref.dtype)
        return carry

    lax.fori_loop(0, nc, gate_step, 0)


def _ssd_mixer(pm, p_t, bt_off, cum_t, dt_t, conv_w, conv_b, dskip_cols, norm_w, d_ssm, out_dtype):
    bsz, seq, _ = pm.shape
    gw = d_ssm // SSD_GROUPS
    r = gw // SSD_HEADDIM
    n = SSD_STATE
    xo, co = d_ssm // gw, 2 * d_ssm // n
    cw = lambda width, off: pl.BlockSpec((SSD_CONV, width), lambda b, g: (0, off + g))
    cb = lambda width, off: pl.BlockSpec((1, width), lambda b, g: (0, off + g))
    heads_rows = pl.BlockSpec((1, 2 * r, seq), lambda b, g: (b, g, 0))
    return pl.pallas_call(
        _ssd_kernel,
        grid=(bsz, SSD_GROUPS),
        in_specs=[pl.BlockSpec((1, seq, gw), lambda b, g: (b, 0, g)),
                  pl.BlockSpec((1, seq, gw), lambda b, g: (b, 0, xo + g)),
                  pl.BlockSpec((1, n, seq), lambda b, g: (b, bt_off + g, 0)),
                  pl.BlockSpec((1, seq, n), lambda b, g: (b, 0, co + g)),
                  heads_rows, heads_rows,
                  cw(gw, 0),
                  pl.BlockSpec((n, SSD_CONV), lambda b, g: (d_ssm // n + g, 0)),
                  cw(n, d_ssm // n + SSD_GROUPS),
                  cb(gw, 0),
                  pl.BlockSpec((n, 1), lambda b, g: (d_ssm // n + g, 0)),
                  cb(n, d_ssm // n + SSD_GROUPS),
                  cb(gw, 0), cb(gw, 0)],
        out_specs=pl.BlockSpec((1, seq, gw), lambda b, g: (b, 0, g)),
        out_shape=jax.ShapeDtypeStruct((bsz, seq, d_ssm), out_dtype),
        scratch_shapes=[pltpu.VMEM((r, seq, gw), BF16),
                        pltpu.VMEM((n, seq), BF16), pltpu.VMEM((seq, n), BF16),
                        pltpu.VMEM((3, seq, gw), F32), pltpu.VMEM((2, n, gw), F32)],
        compiler_params=_params("parallel", "parallel"),
        name="ssd_mixer",
    )(pm, pm, p_t, pm, cum_t, dt_t, conv_w, conv_w.T, conv_w, conv_b, conv_b.reshape(-1, 1), conv_b,
      dskip_cols, norm_w)


def _layer_norm(v, g, b):
    mu = jnp.mean(v, axis=-1, keepdims=True)
    c = v - mu
    var = jnp.mean(jnp.square(c), axis=-1, keepdims=True)
    return c * lax.rsqrt(var + LN_EPS) * g + b


def _merge_kernel(yh_ref, ys_ref, gh_ref, gs_ref, x_ref, whb_ref, wsb_ref, wo_ref, bo_ref,
                  lg_ref, lb_ref, rwt_ref, h_ref, hp_ref, lt_ref):
    a = jnp.dot(yh_ref[...], whb_ref[...], preferred_element_type=F32)
    b = jnp.dot(ys_ref[...], wsb_ref[...], preferred_element_type=F32)
    m = jax.nn.sigmoid(gh_ref[...].astype(F32)) * a + jax.nn.sigmoid(gs_ref[...].astype(F32)) * b
    mix = jnp.dot(m.astype(BF16), wo_ref[...], preferred_element_type=F32) + bo_ref[...]
    h = _layer_norm(DN_ALPHA * x_ref[...] + mix, lg_ref[...], lb_ref[...])
    h_ref[...] = h
    hp_ref[...] = _pack_halves(h)
    lt_ref[...] = lax.dot_general(rwt_ref[...], h, (((1,), (1,)), ((), ())), precision=HI,
                                  preferred_element_type=F32)


def _merge(yhy, yssd, pm, gate_off, x2d, whb, wsb, wo, bo, lg, lb, rwt, tm):
    t, d = x2d.shape
    c = yhy.shape[1]
    d_ssm = yssd.shape[1]
    e = rwt.shape[0]
    row = lambda w: pl.BlockSpec((tm, w), lambda i: (i, 0))
    whole = lambda a: pl.BlockSpec(a.shape, lambda i: (0,) * a.ndim)
    return pl.pallas_call(
        _merge_kernel,
        grid=(t // tm,),
        in_specs=[row(c), row(d_ssm),
                  pl.BlockSpec((tm, d), lambda i: (i, gate_off)),
                  pl.BlockSpec((tm, d), lambda i: (i, gate_off + 1)),
                  row(d), whole(whb), whole(wsb), whole(wo), whole(bo), whole(lg), whole(lb), whole(rwt)],
        out_specs=[row(d), row(d // 2), pl.BlockSpec((e, tm), lambda i: (0, i))],
        out_shape=[jax.ShapeDtypeStruct((t, d), F32), jax.ShapeDtypeStruct((t, d // 2), jnp.int32),
                   jax.ShapeDtypeStruct((e, t), F32)],
        compiler_params=_params("parallel"),
        name="merge_ln",
    )(yhy, yssd, pm, pm, x2d, whb, wsb, wo, bo, lg, lb, rwt)


def _first_max(v, axis, size):
    idx = lax.broadcasted_iota(jnp.int32, v.shape, axis)
    top = jnp.max(v, axis=axis, keepdims=True)
    first = jnp.min(jnp.where(v == top, idx, size), axis=axis, keepdims=True)
    return idx == first, top


def _route_kernel(lt_ref, bias_ref, w_ref, sel_ref):
    n_e, tn = lt_ref.shape
    per = n_e // N_EXPERT_GROUPS
    scores = jax.nn.sigmoid(lt_ref[...])
    biased = scores + bias_ref[...]
    b3 = biased.reshape(N_EXPERT_GROUPS, per, tn)
    hit, top1 = _first_max(b3, 1, per)
    top2 = jnp.max(jnp.where(hit, -jnp.inf, b3), axis=1, keepdims=True)
    cur = (top1 + top2).reshape(N_EXPERT_GROUPS, tn)
    grp = jnp.zeros(cur.shape, jnp.bool_)
    for _ in range(TOPK_GROUPS):
        hit, _ = _first_max(cur, 0, N_EXPERT_GROUPS)
        grp = jnp.logical_or(grp, hit)
        cur = jnp.where(hit, -jnp.inf, cur)
    cur = jnp.where(grp.reshape(N_EXPERT_GROUPS, 1, tn), b3, -jnp.inf).reshape(n_e, tn)
    chosen = jnp.zeros(cur.shape, jnp.bool_)
    for _ in range(TOP_K):
        hit, _ = _first_max(cur, 0, n_e)
        chosen = jnp.logical_or(chosen, hit)
        cur = jnp.where(hit, -jnp.inf, cur)
    w = jnp.where(chosen, scores, 0.0)
    w_ref[...] = w / jnp.sum(w, axis=0, keepdims=True) * ROUTED_SCALE
    sel_ref[...] = jnp.where(chosen, 1.0, 0.0).astype(sel_ref.dtype)


def _route(logits_t, bias_col, tn):
    e, t = logits_t.shape
    blk = pl.BlockSpec((e, tn), lambda i: (0, i))
    return pl.pallas_call(
        _route_kernel,
        grid=(t // tn,),
        in_specs=[blk, pl.BlockSpec((e, 1), lambda i: (0, 0))],
        out_specs=[blk, blk],
        out_shape=[jax.ShapeDtypeStruct((e, t), F32), jax.ShapeDtypeStruct((e, t), BF16)],
        compiler_params=_params("parallel"),
        name="route_topk",
    )(logits_t, bias_col)


MOE_BLOCK = 256


def _rank_kernel(sel_ref, pos_ref, cnt_ref, carry_ref):
    n_e, tn = sel_ref.shape

    @pl.when(pl.program_id(0) == 0)
    def _():
        carry_ref[...] = jnp.zeros_like(carry_ref)

    sel = sel_ref[...]
    s = lax.broadcasted_iota(jnp.int32, (tn, tn), 0)
    t = lax.broadcasted_iota(jnp.int32, (tn, tn), 1)
    earlier = jnp.where(s < t, 1.0, 0.0).astype(BF16)
    before = jnp.dot(sel, earlier, preferred_element_type=F32)
    carry = carry_ref[...]
    pos_ref[...] = carry[:, 0:1] + before
    carry = carry + jnp.sum(sel.astype(F32), axis=1, keepdims=True)
    carry_ref[...] = carry
    cnt_ref[...] = carry


def _rank(sel_t, tn):
    e, t = sel_t.shape
    return pl.pallas_call(
        _rank_kernel,
        grid=(t // tn,),
        in_specs=[pl.BlockSpec((e, tn), lambda i: (0, i))],
        out_specs=[pl.BlockSpec((e, tn), lambda i: (0, i)), pl.BlockSpec((e, 128), lambda i: (0, 0))],
        out_shape=[jax.ShapeDtypeStruct((e, t), F32), jax.ShapeDtypeStruct((e, 128), F32)],
        scratch_shapes=[pltpu.VMEM((e, 128), F32)],
        compiler_params=_params("arbitrary"),
        name="moe_rank",
    )(sel_t)


def _segment_starts(cnt):
    n_e = cnt.shape[0]
    padded = jnp.floor((cnt + (MOE_BLOCK - 1)) * (1.0 / MOE_BLOCK)) * MOE_BLOCK
    a = lax.broadcasted_iota(jnp.int32, (n_e, n_e), 0)
    b = lax.broadcasted_iota(jnp.int32, (n_e, n_e), 1)
    below = jnp.where(b < a, 1.0, 0.0).astype(F32)
    return padded, jnp.dot(below, padded, precision=HI, preferred_element_type=F32)


def _slots_kernel(sel_ref, w_ref, pos_ref, cnt_ref, slot_ref, wk_ref):
    n_e, tn = sel_ref.shape
    _, start = _segment_starts(cnt_ref[...])
    slot = start[:, 0:1] + pos_ref[...]
    w = w_ref[...]
    cur = sel_ref[...].astype(F32)
    idx = lax.broadcasted_iota(jnp.int32, (n_e, tn), 0)
    slots, weights = [], []
    for _ in range(TOP_K):
        first = jnp.min(jnp.where(cur > 0.0, idx, n_e), axis=0, keepdims=True)
        hit = idx == first
        slots.append(jnp.sum(jnp.where(hit, slot, 0.0), axis=0, keepdims=True))
        weights.append(jnp.sum(jnp.where(hit, w, 0.0), axis=0, keepdims=True))
        cur = jnp.where(hit, 0.0, cur)
    slot_ref[...] = jnp.concatenate(slots, axis=0).astype(jnp.int32)
    wk_ref[...] = jnp.concatenate(weights, axis=0).T


def _slots(sel_t, w_t, pos_t, cnt, tn):
    e, t = sel_t.shape
    blk = pl.BlockSpec((e, tn), lambda i: (0, i))
    return pl.pallas_call(
        _slots_kernel,
        grid=(t // tn,),
        in_specs=[blk, blk, blk, pl.BlockSpec((e, 128), lambda i: (0, 0))],
        out_specs=[pl.BlockSpec((TOP_K, tn), lambda i: (0, i)), pl.BlockSpec((tn, TOP_K), lambda i: (i, 0))],
        out_shape=[jax.ShapeDtypeStruct((TOP_K, t), jnp.int32), jax.ShapeDtypeStruct((t, TOP_K), F32)],
        compiler_params=_params("parallel"),
        name="moe_slots",
    )(sel_t, w_t, pos_t, cnt)


def _plan_kernel(cnt_ref, expert_ref, valid_ref):
    n_e = cnt_ref.shape[0]
    nbp = expert_ref.shape[1]
    cnt = cnt_ref[...]
    padded, start = _segment_starts(cnt)
    end = (start + padded)[:, 0:1]
    last = (start + cnt)[:, 0:1]
    row0 = (lax.broadcasted_iota(jnp.int32, (1, nbp), 1) * MOE_BLOCK).astype(F32)
    expert = jnp.minimum(jnp.sum(jnp.where(end <= row0, 1.0, 0.0), axis=0, keepdims=True), n_e - 1.0)
    eid = lax.broadcasted_iota(jnp.int32, (n_e, nbp), 0).astype(F32)
    last_b = jnp.sum(jnp.where(eid == expert, last, 0.0), axis=0, keepdims=True)
    expert_ref[...] = expert.astype(jnp.int32)
    valid_ref[...] = jnp.clip(last_b - row0, 0.0, float(MOE_BLOCK)).astype(jnp.int32)


def _plan(cnt, n_blocks):
    nbp = -(-n_blocks // 128) * 128
    return pl.pallas_call(
        _plan_kernel,
        out_shape=[jax.ShapeDtypeStruct((1, nbp), jnp.int32)] * 2,
        name="moe_plan",
    )(cnt)


V7X_SC_CORES = 2
V7X_SC_SUBCORES = 16
SC_WINDOW = 128


def _sc_mesh():
    return plsc.VectorSubcoreMesh(core_axis_name="c", subcore_axis_name="s",
                                  num_cores=V7X_SC_CORES, num_subcores=V7X_SC_SUBCORES)


def _sc_scatter_rows(rows, idx, n_out):
    t, width = rows.shape
    k_n = idx.shape[0]
    workers = V7X_SC_CORES * V7X_SC_SUBCORES
    per_worker = t // workers
    assert per_worker * workers == t and per_worker % SC_WINDOW == 0

    def body(rows_hbm, idx_hbm, out_hbm, idx_v, rows_v, sem):
        wid = lax.axis_index("s") * V7X_SC_CORES + lax.axis_index("c")

        @pl.loop(0, per_worker // SC_WINDOW)
        def _(j):
            base = wid * per_worker + j * SC_WINDOW
            pltpu.sync_copy(rows_hbm.at[pl.ds(base, SC_WINDOW)], rows_v)
            for k in range(k_n):
                pltpu.sync_copy(idx_hbm.at[pl.ds(k * t + base, SC_WINDOW)], idx_v)
                pltpu.async_copy(rows_v, out_hbm.at[idx_v], sem).wait()

    return pl.kernel(
        body, out_type=jax.ShapeDtypeStruct((n_out, width), rows.dtype), mesh=_sc_mesh(),
        scratch_types=[pltpu.VMEM((SC_WINDOW,), jnp.int32), pltpu.VMEM((SC_WINDOW, width), rows.dtype),
                       pltpu.SemaphoreType.DMA],
        name="moe_dispatch",
    )(rows, idx.reshape(-1))


def _sc_gather_rows(table, idx):
    width = table.shape[1]
    k_n, t = idx.shape
    workers = V7X_SC_CORES * V7X_SC_SUBCORES
    per_worker = t // workers
    assert per_worker * workers == t and per_worker % SC_WINDOW == 0

    def body(table_hbm, idx_hbm, out_hbm, idx_v, rows_v, sem):
        wid = lax.axis_index("s") * V7X_SC_CORES + lax.axis_index("c")

        @pl.loop(0, per_worker // SC_WINDOW)
        def _(j):
            base = wid * per_worker + j * SC_WINDOW
            for k in range(k_n):
                pltpu.sync_copy(idx_hbm.at[pl.ds(k * t + base, SC_WINDOW)], idx_v)
                pltpu.async_copy(table_hbm.at[idx_v], rows_v, sem).wait()
                pltpu.sync_copy(rows_v, out_hbm.at[pl.ds(k * t + base, SC_WINDOW)])

    out = pl.kernel(
        body, out_type=jax.ShapeDtypeStruct((k_n * t, width), table.dtype), mesh=_sc_mesh(),
        scratch_types=[pltpu.VMEM((SC_WINDOW,), jnp.int32), pltpu.VMEM((SC_WINDOW, width), table.dtype),
                       pltpu.SemaphoreType.DMA],
        name="moe_collect",
    )(table, idx.reshape(-1))
    return out.reshape(k_n, t, width)


def _swiglu(xb, wg, wu):
    gate = jnp.dot(xb, wg, preferred_element_type=F32)
    up = jnp.dot(xb, wu, preferred_element_type=F32)
    return gate * jax.nn.sigmoid(gate) * up


def _pack_halves(v):
    w = v.shape[1] // 2
    hi = pltpu.bitcast(v[:, :w].astype(BF16).astype(F32), jnp.int32)
    lo = pltpu.bitcast(v[:, w:].astype(BF16).astype(F32), jnp.int32)
    return jnp.bitwise_or(hi, lax.shift_right_logical(lo, 16))


def _unpack_halves(p):
    hi = pltpu.bitcast(jnp.bitwise_and(p, jnp.int32(-65536)), F32)
    lo = pltpu.bitcast(lax.shift_left(p, 16), F32)
    return jnp.concatenate([hi, lo], axis=1)


def _experts_kernel(expert_ref, valid_ref, x_ref, wg_ref, wu_ref, wd_ref, y_ref, wg_s, wu_s, wd_s):
    b = pl.program_id(0)
    valid = valid_ref[0, b]
    fresh = jnp.logical_or(b == 0, expert_ref[0, b] != expert_ref[0, jnp.maximum(b - 1, 0)])

    @pl.when(jnp.logical_and(valid > 0, fresh))
    def _():
        wg_s[...] = wg_ref[0].astype(BF16)
        wu_s[...] = wu_ref[0].astype(BF16)
        wd_s[...] = wd_ref[0].astype(BF16)

    @pl.when(valid > 0)
    def _():
        rows = lax.broadcasted_iota(jnp.int32, (x_ref.shape[0], 1), 0)
        x = jnp.where(rows < valid, _unpack_halves(x_ref[...]), 0.0).astype(BF16)
        mid = _swiglu(x, wg_s[...], wu_s[...]).astype(BF16)
        y_ref[...] = _pack_halves(jnp.dot(mid, wd_s[...], preferred_element_type=F32))

    @pl.when(valid == 0)
    def _():
        y_ref[...] = jnp.zeros_like(y_ref)


def _experts(x_sorted, expert, valid, wg, wu, wd):
    n_rows, half = x_sorted.shape
    n_e, d, f = wg.shape
    by_expert = lambda shape: pl.BlockSpec((1,) + shape, lambda b, e_ref, v_ref: (e_ref[0, b], 0, 0))
    rows = pl.BlockSpec((MOE_BLOCK, half), lambda b, e_ref, v_ref: (b, 0))
    return pl.pallas_call(
        _experts_kernel,
        grid_spec=pltpu.PrefetchScalarGridSpec(
            num_scalar_prefetch=2, grid=(n_rows // MOE_BLOCK,),
            in_specs=[rows, by_expert((d, f)), by_expert((d, f)), by_expert((f, d))],
            out_specs=rows,
            scratch_shapes=[pltpu.VMEM((d, f), BF16), pltpu.VMEM((d, f), BF16), pltpu.VMEM((f, d), BF16)]),
        out_shape=jax.ShapeDtypeStruct((n_rows, half), jnp.int32),
        compiler_params=_params("arbitrary"),
        name="moe_experts",
    )(expert, valid, x_sorted, wg, wu, wd)


def _combine_kernel(y_ref, wk_ref, hp_ref, h_ref, sg_ref, su_ref, sd_ref, lg_ref, lb_ref, o_ref):
    wk = wk_ref[...]
    mid = _swiglu(_unpack_halves(hp_ref[...]).astype(BF16), sg_ref[...], su_ref[...]).astype(BF16)
    acc = jnp.dot(mid, sd_ref[...], preferred_element_type=F32)
    for k in range(y_ref.shape[0]):
        acc = acc + wk[:, k:k + 1] * _unpack_halves(y_ref[k])
    o_ref[...] = _layer_norm(DN_ALPHA * h_ref[...] + acc, lg_ref[...], lb_ref[...])


def _combine(y_tok, wk, hp, h, sg, su, sd, lg, lb, tm):
    t, d = h.shape
    k_n, _, half = y_tok.shape
    row = lambda w: pl.BlockSpec((tm, w), lambda i: (i, 0))
    whole = lambda a: pl.BlockSpec(a.shape, lambda i: (0,) * a.ndim)
    return pl.pallas_call(
        _combine_kernel,
        grid=(t // tm,),
        in_specs=[pl.BlockSpec((k_n, tm, half), lambda i: (0, i, 0)), row(k_n), row(half), row(d),
                  whole(sg), whole(su), whole(sd), whole(lg), whole(lb)],
        out_specs=row(d),
        out_shape=jax.ShapeDtypeStruct((t, d), F32),
        compiler_params=_params("parallel"),
        name="moe_combine_ln",
    )(y_tok, wk, hp, h, sg, su, sd, lg, lb)


def _pick(n, *cands):
    for c in cands:
        if n % c == 0:
            return c
    return n


def _layer(h, w_in, b_in, hy_conv_w, hy_conv_b, hy_f_w1, hy_f_b1, hy_f_freq1, hy_f_w2, hy_f_b2,
           hy_f_freq2, hy_f_w3, hy_f_b3, hy_f_freq3, hy_f_w4, hy_bias, ssd_conv_w, ssd_conv_b,
           ssd_dt_bias, ssd_a_log, ssd_d, ssd_norm_w, w_hy_branch, w_ssd_branch, w_out, b_out,
           ln1_g, ln1_b, router_w, router_bias, exp_w_gate, exp_w_up, exp_w_down,
           sh_w_gate, sh_w_up, sh_w_down, ln2_g, ln2_b):
    bsz, seq, d = h.shape
    t = bsz * seq
    d_hy = hy_bias.shape[-1]
    d_ssm = ssd_norm_w.shape[-1]
    conv_dim = ssd_conv_w.shape[-1]
    heads = ssd_d.shape[-1]
    hy_cols = (HY_ORDER + 1) * d_hy
    o_z, o_dt, o_gate = hy_cols, hy_cols + d_ssm + conv_dim, hy_cols + d_ssm + conv_dim + 2 * heads
    o_b = o_z + 2 * d_ssm
    o_c = o_b + SSD_GROUPS * SSD_STATE
    row = lambda v: v.astype(F32).reshape(1, -1)
    col = lambda v: v.astype(F32).reshape(-1, 1)

    xb = h.astype(BF16)
    x2d = xb.reshape(t, d)
    tm = _pick(t, 1024, 512, 256, 128)

    ch = lambda v: jnp.concatenate([v[..., :o_z], v[..., o_b:o_c]], axis=-1)
    p_t = _matmul_channels_major(ch(w_in).T.astype(BF16), xb, col(ch(b_in)), BF16,
                                 _pick(hy_cols + o_c - o_b, 1024, 512, 384, 128), _pick(seq, 512, 256, 128))
    tmaj = lambda v: jnp.concatenate([v[..., o_z:o_b], v[..., o_c:o_dt], v[..., o_gate:]], axis=-1)
    w_main = tmaj(w_in).astype(BF16)
    pm = _matmul_bias(x2d, w_main, row(tmaj(b_in)), BF16, tm, _pick(w_main.shape[1], 1024, 512, 256, 128))
    pad = 128 - 2 * heads
    hpg = heads // SSD_GROUPS
    perm = [dr * heads + g * hpg + hh for g in range(SSD_GROUPS) for dr in range(2) for hh in range(hpg)]
    by_group = lambda v: jnp.pad(v.astype(F32).reshape(-1)[jnp.array(perm)], (0, pad)).reshape(1, -1)
    w_dt = jnp.pad(w_in[:, o_dt:o_gate][:, jnp.array(perm)], ((0, 0), (0, pad))).astype(BF16)
    p_dt = _matmul_bias(x2d, w_dt, by_group(b_in[o_dt:o_gate]), F32, tm, 128)

    p = _pick(seq, 512, 256, 128, 64)
    r = _pick(d_hy, 128, 64)
    spectra = _hyena_spectra(seq, p, hy_f_w1, hy_f_b1, hy_f_freq1, hy_f_w2, hy_f_b2, hy_f_freq2,
                             hy_f_w3, hy_f_b3, hy_f_freq3, hy_f_w4, d_hy, r, BF16)
    y_hy = _hyena_mixer(p_t, hy_conv_w.T.astype(F32), col(hy_conv_b), spectra, col(hy_bias),
                        d_hy, p, r, _pick(bsz, 2, 1), BF16)

    cum_t, dt_t = _ssd_prep(p_dt.reshape(bsz, seq, 128), by_group(ssd_dt_bias),
                            by_group(-jnp.exp(ssd_a_log.astype(F32))), hpg)
    y_ssd = _ssd_mixer(pm.reshape(bsz, seq, -1), p_t, hy_cols // SSD_STATE, cum_t, dt_t,
                       ssd_conv_w.astype(F32), row(ssd_conv_b),
                       row(jnp.repeat(ssd_d, SSD_HEADDIM)), row(ssd_norm_w), d_ssm, BF16)

    h1, h1p, logits_t = _merge(y_hy.reshape(t, d_hy), y_ssd.reshape(t, d_ssm), pm,
                               (o_dt - o_z - (o_c - o_b)) // d, h.reshape(t, d).astype(F32),
                               w_hy_branch.astype(BF16), w_ssd_branch.astype(BF16), w_out.astype(BF16),
                               row(b_out), row(ln1_g), row(ln1_b), router_w.T.astype(F32),
                               _pick(t, 512, 256, 128))

    tn = _pick(t, 512, 256, 128)
    w_t, sel_t = _route(logits_t, col(router_bias), tn)
    pos_t, cnt = _rank(sel_t, tn)
    slot_kt, wk = _slots(sel_t, w_t, pos_t, cnt, tn)
    n_blocks = -(-t * TOP_K // MOE_BLOCK) + router_w.shape[-1]
    expert, valid = _plan(cnt, n_blocks)
    x_sorted = _sc_scatter_rows(h1p, slot_kt, n_blocks * MOE_BLOCK)
    y_sorted = _experts(x_sorted, expert, valid, exp_w_gate.astype(F32), exp_w_up.astype(F32),
                        exp_w_down.astype(F32))
    y_tok = _sc_gather_rows(y_sorted, slot_kt)
    out = _combine(y_tok, wk, h1p, h1, sh_w_gate.astype(BF16), sh_w_up.astype(BF16),
                   sh_w_down.astype(BF16), row(ln2_g), row(ln2_b), tn)
    return out.reshape(bsz, seq, d)


def kernel(x, w_in, b_in, hy_conv_w, hy_conv_b, hy_f_w1, hy_f_b1, hy_f_freq1, hy_f_w2, hy_f_b2,
           hy_f_freq2, hy_f_w3, hy_f_b3, hy_f_freq3, hy_f_w4, hy_bias, ssd_conv_w, ssd_conv_b,
           ssd_dt_bias, ssd_a_log, ssd_d, ssd_norm_w, w_hy_branch, w_ssd_branch, w_out, b_out,
           ln1_g, ln1_b, router_w, router_bias, exp_w_gate, exp_w_up, exp_w_down,
           sh_w_gate, sh_w_up, sh_w_down, ln2_g, ln2_b):
    params = (w_in, b_in, hy_conv_w, hy_conv_b, hy_f_w1, hy_f_b1, hy_f_freq1, hy_f_w2, hy_f_b2,
              hy_f_freq2, hy_f_w3, hy_f_b3, hy_f_freq3, hy_f_w4, hy_bias, ssd_conv_w, ssd_conv_b,
              ssd_dt_bias, ssd_a_log, ssd_d, ssd_norm_w, w_hy_branch, w_ssd_branch, w_out, b_out,
              ln1_g, ln1_b, router_w, router_bias, exp_w_gate, exp_w_up, exp_w_down,
              sh_w_gate, sh_w_up, sh_w_down, ln2_g, ln2_b)
    h = x
    for i in range(DEPTH):
        h = _layer(h, *(p[i] for p in params))
    return h
```

```python
import functools
import math

import jax
import jax.numpy as jnp
from jax import lax
from jax.experimental import pallas as pl
from jax.experimental.pallas import tpu as pltpu
from jax.experimental.pallas import tpu_sc as plsc

F32 = jnp.float32
BF16 = jnp.bfloat16
HI = lax.Precision.HIGHEST

HY_ORDER = 2
HY_EMB_DIM = 33
HY_BANDS = (HY_EMB_DIM - 1) // 2
HY_FAST_DECAY_PCT = 0.3
HY_SLOW_DECAY_PCT = 1.5
HY_DECAY_TARGET = 1e-2
SSD_HEADDIM = 64
SSD_GROUPS = 8
SSD_STATE = 128
SSD_CONV = 5
SSD_CHUNK = 128
N_EXPERT_GROUPS = 8
TOPK_GROUPS = 4
TOP_K = 8
ROUTED_SCALE = 2.5
DEPTH = 1
DN_ALPHA = (2.0 * DEPTH) ** 0.25
LN_EPS = 1e-5
RMS_EPS = 1e-5

V7X_VMEM_LIMIT = 56 * 1024 * 1024


def _params(*sem):
    return pltpu.CompilerParams(dimension_semantics=sem, vmem_limit_bytes=V7X_VMEM_LIMIT)


def _mm_bias_kernel(a_ref, b_ref, bias_ref, o_ref):
    acc = jnp.dot(a_ref[...], b_ref[...], preferred_element_type=F32)
    o_ref[...] = (acc + bias_ref[...]).astype(o_ref.dtype)


def _matmul_bias(a, b, bias, out_dtype, tm, tn):
    m, k = a.shape
    n = b.shape[1]
    return pl.pallas_call(
        _mm_bias_kernel,
        grid=(m // tm, n // tn),
        in_specs=[pl.BlockSpec((tm, k), lambda i, j: (i, 0)),
                  pl.BlockSpec((k, tn), lambda i, j: (0, j)),
                  pl.BlockSpec((1, tn), lambda i, j: (0, j))],
        out_specs=pl.BlockSpec((tm, tn), lambda i, j: (i, j)),
        out_shape=jax.ShapeDtypeStruct((m, n), out_dtype),
        compiler_params=_params("parallel", "parallel"),
        name="proj_rows",
    )(a, b, bias)


def _mm_nt_kernel(w_ref, x_ref, bias_ref, o_ref):
    acc = lax.dot_general(w_ref[...], x_ref[0], (((1,), (1,)), ((), ())),
                          preferred_element_type=F32)
    o_ref[0] = (acc + bias_ref[...]).astype(o_ref.dtype)


def _matmul_channels_major(w_t, x, bias_col, out_dtype, tc, tl):
    c, k = w_t.shape
    bsz, seq, _ = x.shape
    return pl.pallas_call(
        _mm_nt_kernel,
        grid=(bsz, c // tc, seq // tl),
        in_specs=[pl.BlockSpec((tc, k), lambda b, i, j: (i, 0)),
                  pl.BlockSpec((1, tl, k), lambda b, i, j: (b, j, 0)),
                  pl.BlockSpec((tc, 1), lambda b, i, j: (i, 0))],
        out_specs=pl.BlockSpec((1, tc, tl), lambda b, i, j: (b, i, j)),
        out_shape=jax.ShapeDtypeStruct((bsz, c, seq), out_dtype),
        compiler_params=_params("parallel", "parallel", "parallel"),
        name="proj_channels",
    )(w_t, x, bias_col)


def _hy_mlp_kernel(w1t_ref, b1_ref, f1_ref, w2t_ref, b2_ref, f2_ref, w3t_ref, b3_ref, f3_ref,
                   o_ref, *, seq, emb_rows):
    shape = (emb_rows, 2 * seq)
    lag = lax.broadcasted_iota(jnp.int32, shape, 1) - seq
    pos = jnp.abs(lag).astype(F32)
    row = lax.broadcasted_iota(jnp.int32, shape, 0)
    t = pos / (seq - 1)
    w = (2.0 * math.pi) * pos / seq
    band = jnp.where(row <= HY_BANDS, row - 1, row - 1 - HY_BANDS).astype(F32)
    freq = 1e-4 + band * ((HY_BANDS - 1 - 1e-4) / (HY_BANDS - 1))
    ang = freq * w
    z = jnp.where(row == 0, t,
                  jnp.where(row <= HY_BANDS, jnp.cos(ang),
                            jnp.where(row <= 2 * HY_BANDS, -jnp.sin(ang), 0.0)))
    h = jnp.sin(f1_ref[...] * (jnp.dot(w1t_ref[...], z, precision=HI, preferred_element_type=F32)
                               + b1_ref[...]))
    h = jnp.sin(f2_ref[...] * (jnp.dot(w2t_ref[...], h, precision=HI, preferred_element_type=F32)
                               + b2_ref[...]))
    h = jnp.sin(f3_ref[...] * (jnp.dot(w3t_ref[...], h, precision=HI, preferred_element_type=F32)
                               + b3_ref[...]))
    o_ref[...] = h


def _dft_matrices(p):
    e = lax.broadcasted_iota(jnp.int32, (p, 2 * p), 0)
    j = lax.broadcasted_iota(jnp.int32, (p, 2 * p), 1)
    k = jnp.where(j < p, j, j - p)
    theta = ((e * k) % (2 * p)).astype(F32) * (math.pi / p)
    sign_e = jnp.where(e % 2 == 0, 1.0, -1.0)
    fwd = jnp.where(j < p, jnp.cos(theta), jnp.where(j == p, sign_e, -jnp.sin(theta)))
    jj = lax.broadcasted_iota(jnp.int32, (2 * p, p), 0)
    tt = lax.broadcasted_iota(jnp.int32, (2 * p, p), 1)
    kk = jnp.where(jj < p, jj, jj - p)
    th = ((tt * kk) % (2 * p)).astype(F32) * (math.pi / p)
    sign_t = jnp.where(tt % 2 == 0, 1.0, -1.0)
    n = 2.0 * p
    inv = jnp.where(jj == 0, 1.0 / n,
                    jnp.where(jj < p, (2.0 / n) * jnp.cos(th),
                              jnp.where(jj == p, sign_t / n, (-2.0 / n) * jnp.sin(th))))
    return fwd.astype(F32), inv.astype(F32)


def _hy_spec_kernel(h_ref, w4f_ref, w4b_ref, fwd_ref, o_ref, *, seq, p, d_hy, rows):
    nb = seq // p
    h = h_ref[...]
    kpos = jnp.dot(w4f_ref[...], h[:, seq:], precision=HI, preferred_element_type=F32)
    kneg = jnp.dot(w4b_ref[...], h[:, :seq], precision=HI, preferred_element_type=F32)
    kb0 = jnp.dot(w4b_ref[...], h[:, seq:seq + 128], precision=HI, preferred_element_type=F32)
    lane128 = lax.broadcasted_iota(jnp.int32, kb0.shape, 1)
    kb0 = jnp.sum(jnp.where(lane128 == 0, kb0, 0.0), axis=1, keepdims=True)

    r0 = pl.program_id(0) * rows
    ch = (r0 + lax.broadcasted_iota(jnp.int32, (rows, 1), 0)) % d_hy
    max_decay = math.log(HY_DECAY_TARGET) / HY_FAST_DECAY_PCT
    min_decay = math.log(HY_DECAY_TARGET) / HY_SLOW_DECAY_PCT
    delta = jnp.abs(min_decay + ch.astype(F32) * ((max_decay - min_decay) / (d_hy - 1)))
    i = lax.broadcasted_iota(jnp.int32, (1, seq), 1)
    t_pos = i.astype(F32) / (seq - 1)
    t_neg = (seq - i).astype(F32) / (seq - 1)
    kpos = (kpos + jnp.where(i == 0, kb0, 0.0)) * jnp.exp(-t_pos * delta)
    kneg = jnp.where(i == 0, 0.0, kneg * jnp.exp(-t_neg * delta))
    klag = jnp.concatenate([kneg, kpos], axis=1)

    fwd = fwd_ref[...]
    lane = lax.broadcasted_iota(jnp.int32, (1, 2 * p), 1)
    sign = jnp.where(lane % 2 == 0, 1.0, -1.0)
    spec = []
    first = []
    for m in range(2 * nb):
        blk = klag[:, m * p:(m + 1) * p]
        spec.append(jnp.dot(blk, fwd, precision=HI, preferred_element_type=F32))
        first.append(blk[:, 0:1])
    half = lax.broadcasted_iota(jnp.int32, (1, p), 1)
    for m in range(1, 2 * nb):
        tail = spec[m - 1] - jnp.where(lane <= p, first[m - 1], 0.0)
        full = spec[m] + sign * tail
        re, im = full[:, :p], full[:, p:]
        o_ref[m - 1, 0] = re.astype(o_ref.dtype)
        o_ref[m - 1, 1] = jnp.where(half == 0, 0.0, im).astype(o_ref.dtype)
        o_ref[m - 1, 2] = jnp.where(half == 0, im, re).astype(o_ref.dtype)


def _hyena_spectra(seq, p, w1, b1, fr1, w2, b2, fr2, w3, b3, fr3, w4, d_hy, rows, spec_dtype):
    hidden = w1.shape[1]
    emb_rows = -(-HY_EMB_DIM // 8) * 8
    w1t = jnp.zeros((hidden, emb_rows), F32).at[:, :HY_EMB_DIM].set(w1.T.astype(F32))
    col = lambda v: v.astype(F32).reshape(hidden, 1)
    h = pl.pallas_call(
        functools.partial(_hy_mlp_kernel, seq=seq, emb_rows=emb_rows),
        out_shape=jax.ShapeDtypeStruct((hidden, 2 * seq), F32),
        name="hyena_filter_mlp",
    )(w1t, col(b1), col(fr1), w2.T.astype(F32), col(b2), col(fr2), w3.T.astype(F32), col(b3), col(fr3))
    w4r = w4.astype(F32).reshape(hidden, HY_ORDER, 2, d_hy)
    w4f = w4r[:, :, 0].reshape(hidden, HY_ORDER * d_hy).T
    w4b = w4r[:, :, 1].reshape(hidden, HY_ORDER * d_hy).T
    fwd, _ = _dft_matrices(p)
    nb = seq // p
    n_rows = HY_ORDER * d_hy
    return pl.pallas_call(
        functools.partial(_hy_spec_kernel, seq=seq, p=p, d_hy=d_hy, rows=rows),
        grid=(n_rows // rows,),
        in_specs=[pl.BlockSpec((hidden, 2 * seq), lambda r: (0, 0)),
                  pl.BlockSpec((rows, hidden), lambda r: (r, 0)),
                  pl.BlockSpec((rows, hidden), lambda r: (r, 0)),
                  pl.BlockSpec((p, 2 * p), lambda r: (0, 0))],
        out_specs=pl.BlockSpec((2 * nb - 1, 3, rows, p), lambda r: (0, 0, r, 0)),
        out_shape=jax.ShapeDtypeStruct((2 * nb - 1, 3, n_rows, p), spec_dtype),
        compiler_params=_params("parallel"),
        name="hyena_filter_spectra",
    )(h, w4f, w4b, fwd)


def _hy_conv_kernel(g1_ref, g2_ref, v_ref, w1_ref, w2_ref, wv_ref, b1_ref, b2_ref, bv_ref,
                    k0_ref, k1_ref, s0_ref, s1_ref, fwd_ref, inv_ref, o_ref, *, p):
    bt, r, seq = v_ref.shape
    nb = seq // p
    rows = bt * r
    lane = lax.broadcasted_iota(jnp.int32, (1, 1, seq), 2)

    def short_conv(x_ref, w_ref, b_ref):
        x = x_ref[...].astype(F32)
        x2 = x.reshape(rows, seq)
        prev = jnp.where(lane == 0, 0.0, pltpu.roll(x2, 1, axis=1).reshape(bt, r, seq))
        nxt = jnp.where(lane == seq - 1, 0.0, pltpu.roll(x2, seq - 1, axis=1).reshape(bt, r, seq))
        w = w_ref[...]
        return b_ref[...] + w[:, 0:1] * prev + w[:, 1:2] * x + w[:, 2:3] * nxt

    fwd = fwd_ref[...]
    inv = inv_ref[...]

    def long_conv(u, k_ref):
        ub = u.astype(BF16).reshape(rows, seq)
        spec = [jnp.dot(ub[:, j * p:(j + 1) * p], fwd, preferred_element_type=F32).reshape(bt, r, 2 * p)
                for j in range(nb)]
        outs = []
        for i in range(nb):
            acc_re = jnp.zeros((bt, r, p), F32)
            acc_im = jnp.zeros((bt, r, p), F32)
            for j in range(nb):
                m = i - j + nb - 1
                ka = k_ref[m, 0].astype(F32)
                kb = k_ref[m, 1].astype(F32)
                kc = k_ref[m, 2].astype(F32)
                u_re = spec[j][:, :, :p]
                u_im = spec[j][:, :, p:]
                acc_re = acc_re + ka * u_re - kb * u_im
                acc_im = acc_im + kc * u_im + kb * u_re
            y_spec = jnp.concatenate([acc_re, acc_im], axis=2).astype(BF16).reshape(rows, 2 * p)
            outs.append(jnp.dot(y_spec, inv, preferred_element_type=F32).reshape(bt, r, p))
        return jnp.concatenate(outs, axis=2)

    v = short_conv(v_ref, wv_ref, bv_ref)
    z = short_conv(g1_ref, w1_ref, b1_ref) * (long_conv(v, k0_ref) + v * s0_ref[...])
    y = short_conv(g2_ref, w2_ref, b2_ref) * (long_conv(z, k1_ref) + z * s1_ref[...])
    for b in range(bt):
        o_ref[b] = y[b].T.astype(o_ref.dtype)


def _hyena_mixer(p_t, conv_w_t, conv_b_col, spectra, skip_col, d_hy, p, r, bt, out_dtype):
    bsz, _, seq = p_t.shape
    nb = seq // p
    nblk = d_hy // r
    fwd, inv = _dft_matrices(p)
    x_spec = lambda off: pl.BlockSpec((bt, r, seq), lambda j, b: (b, off * nblk + j, 0))
    w_spec = lambda off: pl.BlockSpec((r, conv_w_t.shape[1]), lambda j, b: (off * nblk + j, 0))
    c_spec = lambda off: pl.BlockSpec((r, 1), lambda j, b: (off * nblk + j, 0))
    k_spec = lambda off: pl.BlockSpec((2 * nb - 1, 3, r, p), lambda j, b: (0, 0, off * nblk + j, 0))
    return pl.pallas_call(
        functools.partial(_hy_conv_kernel, p=p),
        grid=(nblk, bsz // bt),
        in_specs=[x_spec(0), x_spec(1), x_spec(2), w_spec(0), w_spec(1), w_spec(2),
                  c_spec(0), c_spec(1), c_spec(2), k_spec(0), k_spec(1), c_spec(0), c_spec(1),
                  pl.BlockSpec((p, 2 * p), lambda j, b: (0, 0)),
                  pl.BlockSpec((2 * p, p), lambda j, b: (0, 0))],
        out_specs=pl.BlockSpec((bt, seq, r), lambda j, b: (b, 0, j)),
        out_shape=jax.ShapeDtypeStruct((bsz, seq, d_hy), out_dtype),
        compiler_params=_params("parallel", "arbitrary"),
        name="hyena_mixer",
    )(p_t, p_t, p_t, conv_w_t, conv_w_t, conv_w_t, conv_b_col, conv_b_col, conv_b_col,
      spectra, spectra, skip_col, skip_col, fwd.astype(BF16), inv.astype(BF16))


def _ssd_prep_kernel(dt_ref, dtb_ref, a_ref, cum_ref, dtt_ref, *, r):
    seq = dt_ref.shape[1]
    q = SSD_CHUNK
    raw = dt_ref[0] + dtb_ref[...]
    dt = jnp.maximum(raw, 0.0) + jnp.log1p(jnp.exp(-jnp.abs(raw)))
    a = dt * a_ref[...]
    li = lax.broadcasted_iota(jnp.int32, (q, q), 0)
    si = lax.broadcasted_iota(jnp.int32, (q, q), 1)
    lower = jnp.where(li >= si, 1.0, 0.0).astype(F32)
    upper = jnp.where(li <= si, 1.0, 0.0).astype(F32)
    fwd_col = lax.broadcasted_iota(jnp.int32, (1, 128), 1) % (2 * r) < r
    for c in range(seq // q):
        ac = a[c * q:(c + 1) * q]
        cum = jnp.where(fwd_col,
                        jnp.dot(lower, ac, precision=HI, preferred_element_type=F32),
                        jnp.dot(upper, ac, precision=HI, preferred_element_type=F32))
        cum_ref[0, :, c * q:(c + 1) * q] = cum.T
        dtt_ref[0, :, c * q:(c + 1) * q] = dt[c * q:(c + 1) * q].T


def _ssd_prep(p_dt, dt_bias, a_neg, r):
    bsz, seq, _ = p_dt.shape
    vec = pl.BlockSpec((1, 128), lambda b: (0, 0))
    out = pl.BlockSpec((1, 128, seq), lambda b: (b, 0, 0))
    return pl.pallas_call(
        functools.partial(_ssd_prep_kernel, r=r),
        grid=(bsz,),
        in_specs=[pl.BlockSpec((1, seq, 128), lambda b: (b, 0, 0)), vec, vec],
        out_specs=[out, out],
        out_shape=[jax.ShapeDtypeStruct((bsz, 128, seq), F32)] * 2,
        compiler_params=_params("parallel"),
        name="ssd_prep",
    )(p_dt, dt_bias, a_neg)


def _ssd_kernel(z_ref, x_ref, bt_ref, c_ref, cum_ref, dtt_ref, wx_ref, wbt_ref, wc_ref,
                bx_ref, bbt_ref, bc_ref, dskip_ref, nw_ref, o_ref, xm_s, bt_s, cs_s, y_s, st_s):
    nbat, seq, gw = x_ref.shape
    n = bt_ref.shape[1]
    r = gw // SSD_HEADDIM
    q = SSD_CHUNK
    nc = seq // q
    half = SSD_CONV // 2
    shifts = [k - half for k in range(SSD_CONV) if k != half]
    halo_r, halo_l = 16, 128

    k_r = 2 * q
    sr = lax.broadcasted_iota(jnp.int32, (len(shifts) * q, k_r), 0)
    se = lax.broadcasted_iota(jnp.int32, (len(shifts) * q, k_r), 1)
    band_r = jnp.zeros(sr.shape, jnp.bool_)
    for j, sh in enumerate(shifts):
        band_r = jnp.logical_or(band_r, jnp.logical_and(sr // q == j, se == halo_r + sr % q + sh))
    band_r = jnp.where(band_r, 1.0, 0.0).astype(BF16)
    k_l = q + 2 * halo_l
    le = lax.broadcasted_iota(jnp.int32, (k_l, len(shifts) * q), 0)
    lc = lax.broadcasted_iota(jnp.int32, (k_l, len(shifts) * q), 1)
    band_l = jnp.zeros(le.shape, jnp.bool_)
    for j, sh in enumerate(shifts):
        band_l = jnp.logical_or(band_l, jnp.logical_and(lc // q == j, le == halo_l + lc % q + sh))
    band_l = jnp.where(band_l, 1.0, 0.0).astype(BF16)

    col_head = lax.broadcasted_iota(jnp.int32, (1, gw), 1) // SSD_HEADDIM
    wx, wc, wbt = wx_ref[...], wc_ref[...], wbt_ref[...]

    def conv_rows(ref, bb, c, off, w, bias):
        width = ref.shape[2]
        cur = ref[bb, pl.ds(off, q), :]
        prev = ref[bb, pl.ds(pl.multiple_of(jnp.maximum(off - halo_r, 0), halo_r), halo_r), :]
        nxt = ref[bb, pl.ds(pl.multiple_of(jnp.minimum(off + q, seq - halo_r), halo_r), halo_r), :]
        prev = jnp.where(c > 0, prev, jnp.zeros_like(prev))
        nxt = jnp.where(c < nc - 1, nxt, jnp.zeros_like(nxt))
        ext = jnp.concatenate([prev, cur, nxt, jnp.zeros((k_r - q - 2 * halo_r, width), BF16)], axis=0)
        moved = jnp.dot(band_r, ext, preferred_element_type=F32)
        acc = bias + w[half:half + 1] * cur.astype(F32)
        for j, sh in enumerate(shifts):
            acc = acc + w[half + sh:half + sh + 1] * moved[j * q:(j + 1) * q]
        return acc * jax.nn.sigmoid(acc)

    def conv_one(bb, c, off):
        xs = conv_rows(x_ref, bb, c, off, wx, bx_ref[...])
        y_s[bb, 0, pl.ds(off, q), :] = xs * dskip_ref[...]
        for h in range(r):
            xm_s[bb, h, pl.ds(off, q), :] = jnp.where(col_head == h, xs, 0.0).astype(BF16)
        cs_s[bb, pl.ds(off, q), :] = conv_rows(c_ref, bb, c, off, wc, bc_ref[...]).astype(BF16)
        cur = bt_ref[bb, :, pl.ds(off, q)]
        prev = bt_ref[bb, :, pl.ds(pl.multiple_of(jnp.maximum(off - halo_l, 0), halo_l), halo_l)]
        nxt = bt_ref[bb, :, pl.ds(pl.multiple_of(jnp.minimum(off + q, seq - halo_l), halo_l), halo_l)]
        prev = jnp.where(c > 0, prev, jnp.zeros_like(prev))
        nxt = jnp.where(c < nc - 1, nxt, jnp.zeros_like(nxt))
        moved = jnp.dot(jnp.concatenate([prev, cur, nxt], axis=1), band_l, preferred_element_type=F32)
        acc = bbt_ref[...] + wbt[:, half:half + 1] * cur.astype(F32)
        for j, sh in enumerate(shifts):
            acc = acc + wbt[:, half + sh:half + sh + 1] * moved[:, j * q:(j + 1) * q]
        bt_s[bb, :, pl.ds(off, q)] = (acc * jax.nn.sigmoid(acc)).astype(BF16)

    def conv_step(c, carry):
        off = pl.multiple_of(c * q, q)
        for bb in range(nbat):
            conv_one(bb, c, off)
        return carry

    lax.fori_loop(0, nc, conv_step, 0)
    st_s[...] = jnp.zeros_like(st_s)

    li = lax.broadcasted_iota(jnp.int32, (q, q), 0)
    si = lax.broadcasted_iota(jnp.int32, (q, q), 1)
    diag = li == si
    sub = lax.broadcasted_iota(jnp.int32, (2 * r, 1), 0)
    neg_inf = jnp.float32(-jnp.inf)
    nh = 2 * r
    assert 3 * nh + 16 <= q
    ones_rows = jnp.ones((8, q), F32)
    pad_rows = jnp.zeros((q - 3 * nh - 16, q), F32)
    pk = lax.broadcasted_iota(jnp.int32, (3 * nh + 8, nh * q), 0)
    pj = lax.broadcasted_iota(jnp.int32, (3 * nh + 8, nh * q), 1) // q
    seg_top = jnp.where(jnp.logical_and(pk < 3 * nh, pk % nh == pj), 1.0, 0.0).astype(BF16)
    seg_bottom = jnp.zeros((q - 3 * nh - 8 - 16, nh * q), BF16)

    def split3(v):
        hi = v.astype(BF16).astype(F32)
        mid = (v - hi).astype(BF16).astype(F32)
        lo = (v - hi - mid).astype(BF16).astype(F32)
        return hi, mid, lo

    def flat(v):
        return jnp.concatenate([v[j:j + 1, :] for j in range(nh)], axis=1)

    def chunk_step(i, carry):
        offs = (pl.multiple_of(i * q, q), pl.multiple_of((nc - 1 - i) * q, q))
        chains = [(bb, d) for bb in range(nbat) for d in range(2)]
        pre = []
        for bb in range(nbat):
            cum = jnp.where(sub < r, cum_ref[bb, :, pl.ds(offs[0], q)], cum_ref[bb, :, pl.ds(offs[1], q)])
            dt = jnp.where(sub < r, dtt_ref[bb, :, pl.ds(offs[0], q)], dtt_ref[bb, :, pl.ds(offs[1], q)])
            total = jnp.where(sub < r, cum[:, q - 1:q], cum[:, 0:1])
            hi, mid, lo = split3(cum)
            left = jnp.concatenate([hi, mid, lo, jnp.zeros((8, q), F32), ones_rows, pad_rows], axis=0)
            left = left.T.astype(BF16)
            minus = jnp.concatenate([-flat(hi), -flat(mid), -flat(lo), jnp.zeros((13, nh * q), F32)],
                                    axis=0)
            right = jnp.concatenate([seg_top, minus.astype(BF16), seg_bottom], axis=0)
            pre.append(dict(dt=dt, carry_decay=jnp.exp(total), in_decay=jnp.exp(cum),
                            out_w=jnp.exp(total - cum) * dt, left=left, right=right))
        seg_all = [jnp.dot(p_["left"], p_["right"], preferred_element_type=F32) for p_ in pre]
        btc = {c: bt_s[c[0], :, pl.ds(offs[c[1]], q)] for c in chains}
        cc = {c: cs_s[c[0], pl.ds(offs[c[1]], q), :] for c in chains}
        state = {c: st_s[c[0], c[1]] for c in chains}
        cb = {c: jnp.dot(cc[c], btc[c], preferred_element_type=F32) for c in chains}
        c_state = {c: jnp.dot(cc[c], state[c].astype(BF16), preferred_element_type=F32) for c in chains}
        lhs_y, rhs_y, lhs_s, rhs_s, decays = {}, {}, {}, {}, {}
        for c in chains:
            bb, d = c
            p_ = pre[bb]
            tri = (li >= si) if d == 0 else (li <= si)
            btf = btc[c].astype(F32)
            mats, diags, xs_h, cs_h, b_scaled = [], [], [], [], []
            state_decay = jnp.zeros((1, gw), F32)
            for h in range(r):
                ln = d * r + h
                seg = seg_all[bb][:, ln * q:(ln + 1) * q]
                mats.append((cb[c] * jnp.exp(jnp.where(tri, seg, neg_inf)) * p_["dt"][ln:ln + 1, :])
                            .astype(BF16))
                diags.append(jnp.where(diag, p_["in_decay"][ln:ln + 1, :], 0.0).astype(BF16))
                xs_h.append(xm_s[bb, h, pl.ds(offs[d], q), :])
                cs_h.append(jnp.where(col_head == h, c_state[c], 0.0).astype(BF16))
                b_scaled.append((btf * p_["out_w"][ln:ln + 1, :]).astype(BF16))
                state_decay = jnp.where(col_head == h, p_["carry_decay"][ln:ln + 1, :], state_decay)
            lhs_y[c] = jnp.concatenate(mats + diags, axis=1)
            rhs_y[c] = jnp.concatenate(xs_h + cs_h, axis=0)
            lhs_s[c] = jnp.concatenate(b_scaled, axis=1)
            rhs_s[c] = jnp.concatenate(xs_h, axis=0)
            decays[c] = state_decay
        y_new = {c: jnp.dot(lhs_y[c], rhs_y[c], preferred_element_type=F32) for c in chains}
        upd = {c: jnp.dot(lhs_s[c], rhs_s[c], preferred_element_type=F32) for c in chains}
        for c in chains:
            bb, d = c
            y_s[bb, 1 + d, pl.ds(offs[d], q), :] = y_new[c]
            st_s[bb, d] = state[c] * decays[c] + upd[c]
        return carry

    lax.fori_loop(0, nc, chunk_step, 0, unroll=2)

    def gate_step(c, carry):
        off = pl.multiple_of(c * q, q)
        for bb in range(nbat):
            zf = z_ref[bb, pl.ds(off, q), :].astype(F32)
            y = (y_s[bb, 0, pl.ds(off, q), :] + y_s[bb, 1, pl.ds(off, q), :]
                 + y_s[bb, 2, pl.ds(off, q), :])
            y = y * (zf * jax.nn.sigmoid(zf))
            y = y * lax.rsqrt(jnp.mean(jnp.square(y), axis=-1, keepdims=True) + RMS_EPS)
            o_ref[bb, pl.ds(off, q), :] = (y * nw_ref[...]).astype(o_ref.dtype)
        return carry

    lax.fori_loop(0, nc, gate_step, 0)


def _ssd_mixer(pm, p_t, bt_off, cum_t, dt_t, conv_w, conv_b, dskip_cols, norm_w, d_ssm, out_dtype):
    bsz, seq, _ = pm.shape
    gw = d_ssm // SSD_GROUPS
    r = gw // SSD_HEADDIM
    n = SSD_STATE
    xo, co = d_ssm // gw, 2 * d_ssm // n
    cw = lambda width, off: pl.BlockSpec((SSD_CONV, width), lambda b, g: (0, off + g))
    cb = lambda width, off: pl.BlockSpec((1, width), lambda b, g: (0, off + g))
    nbat = _pick(bsz, 2, 1)
    heads_rows = pl.BlockSpec((nbat, 2 * r, seq), lambda b, g: (b, g, 0))
    return pl.pallas_call(
        _ssd_kernel,
        grid=(bsz // nbat, SSD_GROUPS),
        in_specs=[pl.BlockSpec((nbat, seq, gw), lambda b, g: (b, 0, g)),
                  pl.BlockSpec((nbat, seq, gw), lambda b, g: (b, 0, xo + g)),
                  pl.BlockSpec((nbat, n, seq), lambda b, g: (b, bt_off + g, 0)),
                  pl.BlockSpec((nbat, seq, n), lambda b, g: (b, 0, co + g)),
                  heads_rows, heads_rows,
                  cw(gw, 0),
                  pl.BlockSpec((n, SSD_CONV), lambda b, g: (d_ssm // n + g, 0)),
                  cw(n, d_ssm // n + SSD_GROUPS),
                  cb(gw, 0),
                  pl.BlockSpec((n, 1), lambda b, g: (d_ssm // n + g, 0)),
                  cb(n, d_ssm // n + SSD_GROUPS),
                  cb(gw, 0), cb(gw, 0)],
        out_specs=pl.BlockSpec((nbat, seq, gw), lambda b, g: (b, 0, g)),
        out_shape=jax.ShapeDtypeStruct((bsz, seq, d_ssm), out_dtype),
        scratch_shapes=[pltpu.VMEM((nbat, r, seq, gw), BF16),
                        pltpu.VMEM((nbat, n, seq), BF16), pltpu.VMEM((nbat, seq, n), BF16),
                        pltpu.VMEM((nbat, 3, seq, gw), F32), pltpu.VMEM((nbat, 2, n, gw), F32)],
        compiler_params=_params("parallel", "parallel"),
        name="ssd_mixer",
    )(pm, pm, p_t, pm, cum_t, dt_t, conv_w, conv_w.T, conv_w, conv_b, conv_b.reshape(-1, 1), conv_b,
      dskip_cols, norm_w)


def _layer_norm(v, g, b):
    mu = jnp.mean(v, axis=-1, keepdims=True)
    c = v - mu
    var = jnp.mean(jnp.square(c), axis=-1, keepdims=True)
    return c * lax.rsqrt(var + LN_EPS) * g + b


def _merge_kernel(yh_ref, ys_ref, gh_ref, gs_ref, x_ref, whb_ref, wsb_ref, wo_ref, bo_ref,
                  lg_ref, lb_ref, rwt_ref, h_ref, hp_ref, lt_ref):
    a = jnp.dot(yh_ref[...], whb_ref[...], preferred_element_type=F32)
    b = jnp.dot(ys_ref[...], wsb_ref[...], preferred_element_type=F32)
    m = jax.nn.sigmoid(gh_ref[...].astype(F32)) * a + jax.nn.sigmoid(gs_ref[...].astype(F32)) * b
    mix = jnp.dot(m.astype(BF16), wo_ref[...], preferred_element_type=F32) + bo_ref[...]
    h = _layer_norm(DN_ALPHA * x_ref[...] + mix, lg_ref[...], lb_ref[...])
    h_ref[...] = h
    hp_ref[...] = _pack_halves(h)
    lt_ref[...] = lax.dot_general(rwt_ref[...], h, (((1,), (1,)), ((), ())), precision=HI,
                                  preferred_element_type=F32)


def _merge(yhy, yssd, pm, gate_off, x2d, whb, wsb, wo, bo, lg, lb, rwt, tm):
    t, d = x2d.shape
    c = yhy.shape[1]
    d_ssm = yssd.shape[1]
    e = rwt.shape[0]
    row = lambda w: pl.BlockSpec((tm, w), lambda i: (i, 0))
    whole = lambda a: pl.BlockSpec(a.shape, lambda i: (0,) * a.ndim)
    return pl.pallas_call(
        _merge_kernel,
        grid=(t // tm,),
        in_specs=[row(c), row(d_ssm),
                  pl.BlockSpec((tm, d), lambda i: (i, gate_off)),
                  pl.BlockSpec((tm, d), lambda i: (i, gate_off + 1)),
                  row(d), whole(whb), whole(wsb), whole(wo), whole(bo), whole(lg), whole(lb), whole(rwt)],
        out_specs=[row(d), row(d // 2), pl.BlockSpec((e, tm), lambda i: (0, i))],
        out_shape=[jax.ShapeDtypeStruct((t, d), F32), jax.ShapeDtypeStruct((t, d // 2), jnp.int32),
                   jax.ShapeDtypeStruct((e, t), F32)],
        compiler_params=_params("parallel"),
        name="merge_ln",
    )(yhy, yssd, pm, pm, x2d, whb, wsb, wo, bo, lg, lb, rwt)


def _first_max(v, axis, size):
    idx = lax.broadcasted_iota(jnp.int32, v.shape, axis)
    top = jnp.max(v, axis=axis, keepdims=True)
    first = jnp.min(jnp.where(v == top, idx, size), axis=axis, keepdims=True)
    return idx == first, top


def _route_kernel(lt_ref, bias_ref, w_ref, sel_ref):
    n_e, tn = lt_ref.shape
    per = n_e // N_EXPERT_GROUPS
    scores = jax.nn.sigmoid(lt_ref[...])
    biased = scores + bias_ref[...]
    b3 = biased.reshape(N_EXPERT_GROUPS, per, tn)
    hit, top1 = _first_max(b3, 1, per)
    top2 = jnp.max(jnp.where(hit, -jnp.inf, b3), axis=1, keepdims=True)
    cur = (top1 + top2).reshape(N_EXPERT_GROUPS, tn)
    grp = jnp.zeros(cur.shape, jnp.bool_)
    for _ in range(TOPK_GROUPS):
        hit, _ = _first_max(cur, 0, N_EXPERT_GROUPS)
        grp = jnp.logical_or(grp, hit)
        cur = jnp.where(hit, -jnp.inf, cur)
    cur = jnp.where(grp.reshape(N_EXPERT_GROUPS, 1, tn), b3, -jnp.inf).reshape(n_e, tn)
    chosen = jnp.zeros(cur.shape, jnp.bool_)
    for _ in range(TOP_K):
        hit, _ = _first_max(cur, 0, n_e)
        chosen = jnp.logical_or(chosen, hit)
        cur = jnp.where(hit, -jnp.inf, cur)
    w = jnp.where(chosen, scores, 0.0)
    w_ref[...] = w / jnp.sum(w, axis=0, keepdims=True) * ROUTED_SCALE
    sel_ref[...] = jnp.where(chosen, 1.0, 0.0).astype(sel_ref.dtype)


def _route(logits_t, bias_col, tn):
    e, t = logits_t.shape
    blk = pl.BlockSpec((e, tn), lambda i: (0, i))
    return pl.pallas_call(
        _route_kernel,
        grid=(t // tn,),
        in_specs=[blk, pl.BlockSpec((e, 1), lambda i: (0, 0))],
        out_specs=[blk, blk],
        out_shape=[jax.ShapeDtypeStruct((e, t), F32), jax.ShapeDtypeStruct((e, t), BF16)],
        compiler_params=_params("parallel"),
        name="route_topk",
    )(logits_t, bias_col)


MOE_BLOCK = 512


def _rank_kernel(sel_ref, pos_ref, cnt_ref, carry_ref):
    n_e, tn = sel_ref.shape

    @pl.when(pl.program_id(0) == 0)
    def _():
        carry_ref[...] = jnp.zeros_like(carry_ref)

    sel = sel_ref[...]
    s = lax.broadcasted_iota(jnp.int32, (tn, tn), 0)
    t = lax.broadcasted_iota(jnp.int32, (tn, tn), 1)
    earlier = jnp.where(s < t, 1.0, 0.0).astype(BF16)
    before = jnp.dot(sel, earlier, preferred_element_type=F32)
    carry = carry_ref[...]
    pos_ref[...] = carry[:, 0:1] + before
    carry = carry + jnp.sum(sel.astype(F32), axis=1, keepdims=True)
    carry_ref[...] = carry
    cnt_ref[...] = carry


def _rank(sel_t, tn):
    e, t = sel_t.shape
    return pl.pallas_call(
        _rank_kernel,
        grid=(t // tn,),
        in_specs=[pl.BlockSpec((e, tn), lambda i: (0, i))],
        out_specs=[pl.BlockSpec((e, tn), lambda i: (0, i)), pl.BlockSpec((e, 128), lambda i: (0, 0))],
        out_shape=[jax.ShapeDtypeStruct((e, t), F32), jax.ShapeDtypeStruct((e, 128), F32)],
        scratch_shapes=[pltpu.VMEM((e, 128), F32)],
        compiler_params=_params("arbitrary"),
        name="moe_rank",
    )(sel_t)


def _segment_starts(cnt):
    n_e = cnt.shape[0]
    padded = jnp.floor((cnt + (MOE_BLOCK - 1)) * (1.0 / MOE_BLOCK)) * MOE_BLOCK
    a = lax.broadcasted_iota(jnp.int32, (n_e, n_e), 0)
    b = lax.broadcasted_iota(jnp.int32, (n_e, n_e), 1)
    below = jnp.where(b < a, 1.0, 0.0).astype(F32)
    return padded, jnp.dot(below, padded, precision=HI, preferred_element_type=F32)


def _slots_kernel(sel_ref, w_ref, pos_ref, cnt_ref, slot_ref, wk_ref):
    n_e, tn = sel_ref.shape
    _, start = _segment_starts(cnt_ref[...])
    slot = start[:, 0:1] + pos_ref[...]
    w = w_ref[...]
    cur = sel_ref[...].astype(F32)
    idx = lax.broadcasted_iota(jnp.int32, (n_e, tn), 0)
    slots, weights = [], []
    for _ in range(TOP_K):
        first = jnp.min(jnp.where(cur > 0.0, idx, n_e), axis=0, keepdims=True)
        hit = idx == first
        slots.append(jnp.sum(jnp.where(hit, slot, 0.0), axis=0, keepdims=True))
        weights.append(jnp.sum(jnp.where(hit, w, 0.0), axis=0, keepdims=True))
        cur = jnp.where(hit, 0.0, cur)
    slot_ref[...] = jnp.concatenate(slots, axis=0).astype(jnp.int32)
    wk_ref[...] = jnp.concatenate(weights, axis=0).T


def _slots(sel_t, w_t, pos_t, cnt, tn):
    e, t = sel_t.shape
    blk = pl.BlockSpec((e, tn), lambda i: (0, i))
    return pl.pallas_call(
        _slots_kernel,
        grid=(t // tn,),
        in_specs=[blk, blk, blk, pl.BlockSpec((e, 128), lambda i: (0, 0))],
        out_specs=[pl.BlockSpec((TOP_K, tn), lambda i: (0, i)), pl.BlockSpec((tn, TOP_K), lambda i: (i, 0))],
        out_shape=[jax.ShapeDtypeStruct((TOP_K, t), jnp.int32), jax.ShapeDtypeStruct((t, TOP_K), F32)],
        compiler_params=_params("parallel"),
        name="moe_slots",
    )(sel_t, w_t, pos_t, cnt)


def _plan_kernel(cnt_ref, expert_ref, valid_ref):
    n_e = cnt_ref.shape[0]
    nbp = expert_ref.shape[1]
    cnt = cnt_ref[...]
    padded, start = _segment_starts(cnt)
    end = (start + padded)[:, 0:1]
    last = (start + cnt)[:, 0:1]
    row0 = (lax.broadcasted_iota(jnp.int32, (1, nbp), 1) * MOE_BLOCK).astype(F32)
    expert = jnp.minimum(jnp.sum(jnp.where(end <= row0, 1.0, 0.0), axis=0, keepdims=True), n_e - 1.0)
    eid = lax.broadcasted_iota(jnp.int32, (n_e, nbp), 0).astype(F32)
    last_b = jnp.sum(jnp.where(eid == expert, last, 0.0), axis=0, keepdims=True)
    expert_ref[...] = expert.astype(jnp.int32)
    valid_ref[...] = jnp.clip(last_b - row0, 0.0, float(MOE_BLOCK)).astype(jnp.int32)


def _plan(cnt, n_blocks):
    nbp = -(-n_blocks // 128) * 128
    return pl.pallas_call(
        _plan_kernel,
        out_shape=[jax.ShapeDtypeStruct((1, nbp), jnp.int32)] * 2,
        name="moe_plan",
    )(cnt)


V7X_SC_CORES = 2
V7X_SC_SUBCORES = 16
SC_WINDOW = 128


def _sc_mesh():
    return plsc.VectorSubcoreMesh(core_axis_name="c", subcore_axis_name="s",
                                  num_cores=V7X_SC_CORES, num_subcores=V7X_SC_SUBCORES)


def _sc_scatter_rows(rows, idx, n_out):
    t, width = rows.shape
    k_n = idx.shape[0]
    workers = V7X_SC_CORES * V7X_SC_SUBCORES
    per_worker = t // workers
    assert per_worker * workers == t and per_worker % SC_WINDOW == 0

    def body(rows_hbm, idx_hbm, out_hbm, idx_v, rows_v, sem):
        wid = lax.axis_index("s") * V7X_SC_CORES + lax.axis_index("c")

        @pl.loop(0, per_worker // SC_WINDOW)
        def _(j):
            base = wid * per_worker + j * SC_WINDOW
            pltpu.sync_copy(rows_hbm.at[pl.ds(base, SC_WINDOW)], rows_v)
            for k in range(k_n):
                pltpu.sync_copy(idx_hbm.at[pl.ds(k * t + base, SC_WINDOW)], idx_v)
                pltpu.async_copy(rows_v, out_hbm.at[idx_v], sem).wait()

    return pl.kernel(
        body, out_type=jax.ShapeDtypeStruct((n_out, width), rows.dtype), mesh=_sc_mesh(),
        scratch_types=[pltpu.VMEM((SC_WINDOW,), jnp.int32), pltpu.VMEM((SC_WINDOW, width), rows.dtype),
                       pltpu.SemaphoreType.DMA],
        name="moe_dispatch",
    )(rows, idx.reshape(-1))


def _sc_gather_rows(table, idx):
    width = table.shape[1]
    k_n, t = idx.shape
    workers = V7X_SC_CORES * V7X_SC_SUBCORES
    per_worker = t // workers
    assert per_worker * workers == t and per_worker % SC_WINDOW == 0

    def body(table_hbm, idx_hbm, out_hbm, idx_v, rows_v, sem):
        wid = lax.axis_index("s") * V7X_SC_CORES + lax.axis_index("c")

        @pl.loop(0, per_worker // SC_WINDOW)
        def _(j):
            base = wid * per_worker + j * SC_WINDOW
            for k in range(k_n):
                pltpu.sync_copy(idx_hbm.at[pl.ds(k * t + base, SC_WINDOW)], idx_v)
                pltpu.async_copy(table_hbm.at[idx_v], rows_v, sem).wait()
                pltpu.sync_copy(rows_v, out_hbm.at[pl.ds(k * t + base, SC_WINDOW)])

    out = pl.kernel(
        body, out_type=jax.ShapeDtypeStruct((k_n * t, width), table.dtype), mesh=_sc_mesh(),
        scratch_types=[pltpu.VMEM((SC_WINDOW,), jnp.int32), pltpu.VMEM((SC_WINDOW, width), table.dtype),
                       pltpu.SemaphoreType.DMA],
        name="moe_collect",
    )(table, idx.reshape(-1))
    return out.reshape(k_n, t, width)


def _swiglu(xb, wg, wu):
    gate = jnp.dot(xb, wg, preferred_element_type=F32)
    up = jnp.dot(xb, wu, preferred_element_type=F32)
    return gate * jax.nn.sigmoid(gate) * up


def _pack_halves(v):
    w = v.shape[1] // 2
    hi = pltpu.bitcast(v[:, :w].astype(BF16).astype(F32), jnp.int32)
    lo = pltpu.bitcast(v[:, w:].astype(BF16).astype(F32), jnp.int32)
    return jnp.bitwise_or(hi, lax.shift_right_logical(lo, 16))


def _unpack_halves(p):
    hi = pltpu.bitcast(jnp.bitwise_and(p, jnp.int32(-65536)), F32)
    lo = pltpu.bitcast(lax.shift_left(p, 16), F32)
    return jnp.concatenate([hi, lo], axis=1)


def _experts_kernel(expert_ref, valid_ref, x_ref, wg_ref, wu_ref, wd_ref, y_ref, wg_s, wu_s, wd_s):
    b = pl.program_id(0)
    valid = valid_ref[0, b]
    fresh = jnp.logical_or(b == 0, expert_ref[0, b] != expert_ref[0, jnp.maximum(b - 1, 0)])

    @pl.when(jnp.logical_and(valid > 0, fresh))
    def _():
        wg_s[...] = wg_ref[0].astype(BF16)
        wu_s[...] = wu_ref[0].astype(BF16)
        wd_s[...] = wd_ref[0].astype(BF16)

    @pl.when(valid > 0)
    def _():
        rows = lax.broadcasted_iota(jnp.int32, (x_ref.shape[0], 1), 0)
        x = jnp.where(rows < valid, _unpack_halves(x_ref[...]), 0.0).astype(BF16)
        mid = _swiglu(x, wg_s[...], wu_s[...]).astype(BF16)
        y_ref[...] = _pack_halves(jnp.dot(mid, wd_s[...], preferred_element_type=F32))

    @pl.when(valid == 0)
    def _():
        y_ref[...] = jnp.zeros_like(y_ref)


def _experts(x_sorted, expert, valid, wg, wu, wd):
    n_rows, half = x_sorted.shape
    n_e, d, f = wg.shape
    by_expert = lambda shape: pl.BlockSpec((1,) + shape, lambda b, e_ref, v_ref: (e_ref[0, b], 0, 0))
    rows = pl.BlockSpec((MOE_BLOCK, half), lambda b, e_ref, v_ref: (b, 0))
    return pl.pallas_call(
        _experts_kernel,
        grid_spec=pltpu.PrefetchScalarGridSpec(
            num_scalar_prefetch=2, grid=(n_rows // MOE_BLOCK,),
            in_specs=[rows, by_expert((d, f)), by_expert((d, f)), by_expert((f, d))],
            out_specs=rows,
            scratch_shapes=[pltpu.VMEM((d, f), BF16), pltpu.VMEM((d, f), BF16), pltpu.VMEM((f, d), BF16)]),
        out_shape=jax.ShapeDtypeStruct((n_rows, half), jnp.int32),
        compiler_params=_params("arbitrary"),
        name="moe_experts",
    )(expert, valid, x_sorted, wg, wu, wd)


def _combine_kernel(y_ref, wk_ref, hp_ref, h_ref, sg_ref, su_ref, sd_ref, lg_ref, lb_ref, o_ref):
    wk = wk_ref[...]
    mid = _swiglu(_unpack_halves(hp_ref[...]).astype(BF16), sg_ref[...], su_ref[...]).astype(BF16)
    acc = jnp.dot(mid, sd_ref[...], preferred_element_type=F32)
    for k in range(y_ref.shape[0]):
        acc = acc + wk[:, k:k + 1] * _unpack_halves(y_ref[k])
    o_ref[...] = _layer_norm(DN_ALPHA * h_ref[...] + acc, lg_ref[...], lb_ref[...])


def _combine(y_tok, wk, hp, h, sg, su, sd, lg, lb, tm):
    t, d = h.shape
    k_n, _, half = y_tok.shape
    row = lambda w: pl.BlockSpec((tm, w), lambda i: (i, 0))
    whole = lambda a: pl.BlockSpec(a.shape, lambda i: (0,) * a.ndim)
    return pl.pallas_call(
        _combine_kernel,
        grid=(t // tm,),
        in_specs=[pl.BlockSpec((k_n, tm, half), lambda i: (0, i, 0)), row(k_n), row(half), row(d),
                  whole(sg), whole(su), whole(sd), whole(lg), whole(lb)],
        out_specs=row(d),
        out_shape=jax.ShapeDtypeStruct((t, d), F32),
        compiler_params=_params("parallel"),
        name="moe_combine_ln",
    )(y_tok, wk, hp, h, sg, su, sd, lg, lb)


def _pick(n, *cands):
    for c in cands:
        if n % c == 0:
            return c
    return n


def _layer(h, w_in, b_in, hy_conv_w, hy_conv_b, hy_f_w1, hy_f_b1, hy_f_freq1, hy_f_w2, hy_f_b2,
           hy_f_freq2, hy_f_w3, hy_f_b3, hy_f_freq3, hy_f_w4, hy_bias, ssd_conv_w, ssd_conv_b,
           ssd_dt_bias, ssd_a_log, ssd_d, ssd_norm_w, w_hy_branch, w_ssd_branch, w_out, b_out,
           ln1_g, ln1_b, router_w, router_bias, exp_w_gate, exp_w_up, exp_w_down,
           sh_w_gate, sh_w_up, sh_w_down, ln2_g, ln2_b):
    bsz, seq, d = h.shape
    t = bsz * seq
    d_hy = hy_bias.shape[-1]
    d_ssm = ssd_norm_w.shape[-1]
    conv_dim = ssd_conv_w.shape[-1]
    heads = ssd_d.shape[-1]
    hy_cols = (HY_ORDER + 1) * d_hy
    o_z, o_dt, o_gate = hy_cols, hy_cols + d_ssm + conv_dim, hy_cols + d_ssm + conv_dim + 2 * heads
    o_b = o_z + 2 * d_ssm
    o_c = o_b + SSD_GROUPS * SSD_STATE
    row = lambda v: v.astype(F32).reshape(1, -1)
    col = lambda v: v.astype(F32).reshape(-1, 1)

    xb = h.astype(BF16)
    x2d = xb.reshape(t, d)
    tm = _pick(t, 1024, 512, 256, 128)

    ch = lambda v: jnp.concatenate([v[..., :o_z], v[..., o_b:o_c]], axis=-1)
    p_t = _matmul_channels_major(ch(w_in).T.astype(BF16), xb, col(ch(b_in)), BF16,
                                 _pick(hy_cols + o_c - o_b, 1024, 512, 384, 128), _pick(seq, 512, 256, 128))
    tmaj = lambda v: jnp.concatenate([v[..., o_z:o_b], v[..., o_c:o_dt], v[..., o_gate:]], axis=-1)
    w_main = tmaj(w_in).astype(BF16)
    pm = _matmul_bias(x2d, w_main, row(tmaj(b_in)), BF16, tm, _pick(w_main.shape[1], 1024, 512, 256, 128))
    pad = 128 - 2 * heads
    hpg = heads // SSD_GROUPS
    perm = [dr * heads + g * hpg + hh for g in range(SSD_GROUPS) for dr in range(2) for hh in range(hpg)]
    by_group = lambda v: jnp.pad(v.astype(F32).reshape(-1)[jnp.array(perm)], (0, pad)).reshape(1, -1)
    w_dt = jnp.pad(w_in[:, o_dt:o_gate][:, jnp.array(perm)], ((0, 0), (0, pad))).astype(BF16)
    p_dt = _matmul_bias(x2d, w_dt, by_group(b_in[o_dt:o_gate]), F32, tm, 128)

    p = _pick(seq, 512, 256, 128, 64)
    r = _pick(d_hy, 128, 64)
    spectra = _hyena_spectra(seq, p, hy_f_w1, hy_f_b1, hy_f_freq1, hy_f_w2, hy_f_b2, hy_f_freq2,
                             hy_f_w3, hy_f_b3, hy_f_freq3, hy_f_w4, d_hy, r, BF16)
    y_hy = _hyena_mixer(p_t, hy_conv_w.T.astype(F32), col(hy_conv_b), spectra, col(hy_bias),
                        d_hy, p, r, _pick(bsz, 2, 1), BF16)

    cum_t, dt_t = _ssd_prep(p_dt.reshape(bsz, seq, 128), by_group(ssd_dt_bias),
                            by_group(-jnp.exp(ssd_a_log.astype(F32))), hpg)
    y_ssd = _ssd_mixer(pm.reshape(bsz, seq, -1), p_t, hy_cols // SSD_STATE, cum_t, dt_t,
                       ssd_conv_w.astype(F32), row(ssd_conv_b),
                       row(jnp.repeat(ssd_d, SSD_HEADDIM)), row(ssd_norm_w), d_ssm, BF16)

    h1, h1p, logits_t = _merge(y_hy.reshape(t, d_hy), y_ssd.reshape(t, d_ssm), pm,
                               (o_dt - o_z - (o_c - o_b)) // d, h.reshape(t, d).astype(F32),
                               w_hy_branch.astype(BF16), w_ssd_branch.astype(BF16), w_out.astype(BF16),
                               row(b_out), row(ln1_g), row(ln1_b), router_w.T.astype(F32),
                               _pick(t, 512, 256, 128))

    tn = _pick(t, 512, 256, 128)
    w_t, sel_t = _route(logits_t, col(router_bias), tn)
    pos_t, cnt = _rank(sel_t, tn)
    slot_kt, wk = _slots(sel_t, w_t, pos_t, cnt, tn)
    n_blocks = -(-t * TOP_K // MOE_BLOCK) + router_w.shape[-1]
    expert, valid = _plan(cnt, n_blocks)
    x_sorted = _sc_scatter_rows(h1p, slot_kt, n_blocks * MOE_BLOCK)
    y_sorted = _experts(x_sorted, expert, valid, exp_w_gate.astype(F32), exp_w_up.astype(F32),
                        exp_w_down.astype(F32))
    y_tok = _sc_gather_rows(y_sorted, slot_kt)
    out = _combine(y_tok, wk, h1p, h1, sh_w_gate.astype(BF16), sh_w_up.astype(BF16),
                   sh_w_down.astype(BF16), row(ln2_g), row(ln2_b), tn)
    return out.reshape(bsz, seq, d)


def kernel(x, w_in, b_in, hy_conv_w, hy_conv_b, hy_f_w1, hy_f_b1, hy_f_freq1, hy_f_w2, hy_f_b2,
           hy_f_freq2, hy_f_w3, hy_f_b3, hy_f_freq3, hy_f_w4, hy_bias, ssd_conv_w, ssd_conv_b,
           ssd_dt_bias, ssd_a_log, ssd_d, ssd_norm_w, w_hy_branch, w_ssd_branch, w_out, b_out,
           ln1_g, ln1_b, router_w, router_bias, exp_w_gate, exp_w_up, exp_w_down,
           sh_w_gate, sh_w_up, sh_w_down, ln2_g, ln2_b):
    params = (w_in, b_in, hy_conv_w, hy_conv_b, hy_f_w1, hy_f_b1, hy_f_freq1, hy_f_w2, hy_f_b2,
              hy_f_freq2, hy_f_w3, hy_f_b3, hy_f_freq3, hy_f_w4, hy_bias, ssd_conv_w, ssd_conv_b,
              ssd_dt_bias, ssd_a_log, ssd_d, ssd_norm_w, w_hy_branch, w_ssd_branch, w_out, b_out,
              ln1_g, ln1_b, router_w, router_bias, exp_w_gate, exp_w_up, exp_w_down,
              sh_w_gate, sh_w_up, sh_w_down, ln2_g, ln2_b)
    h = x
    for i in range(DEPTH):
        h = _layer(h, *(p[i] for p in params))
    return h
```

```python
import functools
import math

import jax
import jax.numpy as jnp
from jax import lax
from jax.experimental import pallas as pl
from jax.experimental.pallas import tpu as pltpu
from jax.experimental.pallas import tpu_sc as plsc

F32 = jnp.float32
BF16 = jnp.bfloat16
HI = lax.Precision.HIGHEST

HY_ORDER = 2
HY_EMB_DIM = 33
HY_BANDS = (HY_EMB_DIM - 1) // 2
HY_FAST_DECAY_PCT = 0.3
HY_SLOW_DECAY_PCT = 1.5
HY_DECAY_TARGET = 1e-2
SSD_HEADDIM = 64
SSD_GROUPS = 8
SSD_STATE = 128
SSD_CONV = 5
SSD_CHUNK = 128
N_EXPERT_GROUPS = 8
TOPK_GROUPS = 4
TOP_K = 8
ROUTED_SCALE = 2.5
DEPTH = 1
DN_ALPHA = (2.0 * DEPTH) ** 0.25
LN_EPS = 1e-5
RMS_EPS = 1e-5

V7X_VMEM_LIMIT = 56 * 1024 * 1024


def _params(*sem):
    return pltpu.CompilerParams(dimension_semantics=sem, vmem_limit_bytes=V7X_VMEM_LIMIT)


def _sigmoid(v):
    return 0.5 + 0.5 * jnp.tanh(0.5 * v)


def _silu(v):
    hv = 0.5 * v
    return hv + hv * jnp.tanh(hv)


def _mm_bias_kernel(a_ref, b_ref, bias_ref, o_ref):
    acc = jnp.dot(a_ref[...], b_ref[...], preferred_element_type=F32)
    o_ref[...] = (acc + bias_ref[...]).astype(o_ref.dtype)


def _matmul_bias(a, b, bias, out_dtype, tm, tn):
    m, k = a.shape
    n = b.shape[1]
    return pl.pallas_call(
        _mm_bias_kernel,
        grid=(m // tm, n // tn),
        in_specs=[pl.BlockSpec((tm, k), lambda i, j: (i, 0)),
                  pl.BlockSpec((k, tn), lambda i, j: (0, j)),
                  pl.BlockSpec((1, tn), lambda i, j: (0, j))],
        out_specs=pl.BlockSpec((tm, tn), lambda i, j: (i, j)),
        out_shape=jax.ShapeDtypeStruct((m, n), out_dtype),
        compiler_params=_params("parallel", "parallel"),
        name="proj_rows",
    )(a, b, bias)


def _mm_nt_kernel(w_ref, x_ref, bias_ref, o_ref):
    acc = lax.dot_general(w_ref[...], x_ref[0], (((1,), (1,)), ((), ())),
                          preferred_element_type=F32)
    o_ref[0] = (acc + bias_ref[...]).astype(o_ref.dtype)


def _matmul_channels_major(w_t, x, bias_col, out_dtype, tc, tl):
    c, k = w_t.shape
    bsz, seq, _ = x.shape
    return pl.pallas_call(
        _mm_nt_kernel,
        grid=(bsz, c // tc, seq // tl),
        in_specs=[pl.BlockSpec((tc, k), lambda b, i, j: (i, 0)),
                  pl.BlockSpec((1, tl, k), lambda b, i, j: (b, j, 0)),
                  pl.BlockSpec((tc, 1), lambda b, i, j: (i, 0))],
        out_specs=pl.BlockSpec((1, tc, tl), lambda b, i, j: (b, i, j)),
        out_shape=jax.ShapeDtypeStruct((bsz, c, seq), out_dtype),
        compiler_params=_params("parallel", "parallel", "parallel"),
        name="proj_channels",
    )(w_t, x, bias_col)


def _hy_mlp_kernel(w1t_ref, b1_ref, f1_ref, w2t_ref, b2_ref, f2_ref, w3t_ref, b3_ref, f3_ref,
                   o_ref, *, seq, emb_rows):
    shape = (emb_rows, 2 * seq)
    lag = lax.broadcasted_iota(jnp.int32, shape, 1) - seq
    pos = jnp.abs(lag).astype(F32)
    row = lax.broadcasted_iota(jnp.int32, shape, 0)
    t = pos / (seq - 1)
    w = (2.0 * math.pi) * pos / seq
    band = jnp.where(row <= HY_BANDS, row - 1, row - 1 - HY_BANDS).astype(F32)
    freq = 1e-4 + band * ((HY_BANDS - 1 - 1e-4) / (HY_BANDS - 1))
    ang = freq * w
    z = jnp.where(row == 0, t,
                  jnp.where(row <= HY_BANDS, jnp.cos(ang),
                            jnp.where(row <= 2 * HY_BANDS, -jnp.sin(ang), 0.0)))
    h = jnp.sin(f1_ref[...] * (jnp.dot(w1t_ref[...], z, precision=HI, preferred_element_type=F32)
                               + b1_ref[...]))
    h = jnp.sin(f2_ref[...] * (jnp.dot(w2t_ref[...], h, precision=HI, preferred_element_type=F32)
                               + b2_ref[...]))
    h = jnp.sin(f3_ref[...] * (jnp.dot(w3t_ref[...], h, precision=HI, preferred_element_type=F32)
                               + b3_ref[...]))
    o_ref[...] = h


def _dft_matrices(p):
    e = lax.broadcasted_iota(jnp.int32, (p, 2 * p), 0)
    j = lax.broadcasted_iota(jnp.int32, (p, 2 * p), 1)
    k = jnp.where(j < p, j, j - p)
    theta = ((e * k) % (2 * p)).astype(F32) * (math.pi / p)
    sign_e = jnp.where(e % 2 == 0, 1.0, -1.0)
    fwd = jnp.where(j < p, jnp.cos(theta), jnp.where(j == p, sign_e, -jnp.sin(theta)))
    jj = lax.broadcasted_iota(jnp.int32, (2 * p, p), 0)
    tt = lax.broadcasted_iota(jnp.int32, (2 * p, p), 1)
    kk = jnp.where(jj < p, jj, jj - p)
    th = ((tt * kk) % (2 * p)).astype(F32) * (math.pi / p)
    sign_t = jnp.where(tt % 2 == 0, 1.0, -1.0)
    n = 2.0 * p
    inv = jnp.where(jj == 0, 1.0 / n,
                    jnp.where(jj < p, (2.0 / n) * jnp.cos(th),
                              jnp.where(jj == p, sign_t / n, (-2.0 / n) * jnp.sin(th))))
    return fwd.astype(F32), inv.astype(F32)


def _hy_spec_kernel(h_ref, w4f_ref, w4b_ref, fwd_ref, o_ref, *, seq, p, d_hy, rows):
    nb = seq // p
    h = h_ref[...]
    kpos = jnp.dot(w4f_ref[...], h[:, seq:], precision=HI, preferred_element_type=F32)
    kneg = jnp.dot(w4b_ref[...], h[:, :seq], precision=HI, preferred_element_type=F32)
    kb0 = jnp.dot(w4b_ref[...], h[:, seq:seq + 128], precision=HI, preferred_element_type=F32)
    lane128 = lax.broadcasted_iota(jnp.int32, kb0.shape, 1)
    kb0 = jnp.sum(jnp.where(lane128 == 0, kb0, 0.0), axis=1, keepdims=True)

    r0 = pl.program_id(0) * rows
    ch = (r0 + lax.broadcasted_iota(jnp.int32, (rows, 1), 0)) % d_hy
    max_decay = math.log(HY_DECAY_TARGET) / HY_FAST_DECAY_PCT
    min_decay = math.log(HY_DECAY_TARGET) / HY_SLOW_DECAY_PCT
    delta = jnp.abs(min_decay + ch.astype(F32) * ((max_decay - min_decay) / (d_hy - 1)))
    i = lax.broadcasted_iota(jnp.int32, (1, seq), 1)
    t_pos = i.astype(F32) / (seq - 1)
    t_neg = (seq - i).astype(F32) / (seq - 1)
    kpos = (kpos + jnp.where(i == 0, kb0, 0.0)) * jnp.exp(-t_pos * delta)
    kneg = jnp.where(i == 0, 0.0, kneg * jnp.exp(-t_neg * delta))
    klag = jnp.concatenate([kneg, kpos], axis=1)

    fwd = fwd_ref[...]
    lane = lax.broadcasted_iota(jnp.int32, (1, 2 * p), 1)
    sign = jnp.where(lane % 2 == 0, 1.0, -1.0)
    spec = []
    first = []
    for m in range(2 * nb):
        blk = klag[:, m * p:(m + 1) * p]
        spec.append(jnp.dot(blk, fwd, precision=HI, preferred_element_type=F32))
        first.append(blk[:, 0:1])
    half = lax.broadcasted_iota(jnp.int32, (1, p), 1)
    for m in range(1, 2 * nb):
        tail = spec[m - 1] - jnp.where(lane <= p, first[m - 1], 0.0)
        full = spec[m] + sign * tail
        re, im = full[:, :p], full[:, p:]
        o_ref[m - 1, 0] = re.astype(o_ref.dtype)
        o_ref[m - 1, 1] = jnp.where(half == 0, 0.0, im).astype(o_ref.dtype)
        o_ref[m - 1, 2] = jnp.where(half == 0, im, re).astype(o_ref.dtype)


def _hyena_spectra(seq, p, w1, b1, fr1, w2, b2, fr2, w3, b3, fr3, w4, d_hy, rows, spec_dtype):
    hidden = w1.shape[1]
    emb_rows = -(-HY_EMB_DIM // 8) * 8
    w1t = jnp.zeros((hidden, emb_rows), F32).at[:, :HY_EMB_DIM].set(w1.T.astype(F32))
    col = lambda v: v.astype(F32).reshape(hidden, 1)
    h = pl.pallas_call(
        functools.partial(_hy_mlp_kernel, seq=seq, emb_rows=emb_rows),
        out_shape=jax.ShapeDtypeStruct((hidden, 2 * seq), F32),
        name="hyena_filter_mlp",
    )(w1t, col(b1), col(fr1), w2.T.astype(F32), col(b2), col(fr2), w3.T.astype(F32), col(b3), col(fr3))
    w4r = w4.astype(F32).reshape(hidden, HY_ORDER, 2, d_hy)
    w4f = w4r[:, :, 0].reshape(hidden, HY_ORDER * d_hy).T
    w4b = w4r[:, :, 1].reshape(hidden, HY_ORDER * d_hy).T
    fwd, _ = _dft_matrices(p)
    nb = seq // p
    n_rows = HY_ORDER * d_hy
    return pl.pallas_call(
        functools.partial(_hy_spec_kernel, seq=seq, p=p, d_hy=d_hy, rows=rows),
        grid=(n_rows // rows,),
        in_specs=[pl.BlockSpec((hidden, 2 * seq), lambda r: (0, 0)),
                  pl.BlockSpec((rows, hidden), lambda r: (r, 0)),
                  pl.BlockSpec((rows, hidden), lambda r: (r, 0)),
                  pl.BlockSpec((p, 2 * p), lambda r: (0, 0))],
        out_specs=pl.BlockSpec((2 * nb - 1, 3, rows, p), lambda r: (0, 0, r, 0)),
        out_shape=jax.ShapeDtypeStruct((2 * nb - 1, 3, n_rows, p), spec_dtype),
        compiler_params=_params("parallel"),
        name="hyena_filter_spectra",
    )(h, w4f, w4b, fwd)


def _hy_conv_kernel(g1_ref, g2_ref, v_ref, w1_ref, w2_ref, wv_ref, b1_ref, b2_ref, bv_ref,
                    k0_ref, k1_ref, s0_ref, s1_ref, fwd_ref, inv_ref, o_ref, *, p):
    bt, r, seq = v_ref.shape
    nb = seq // p
    rows = bt * r
    lane = lax.broadcasted_iota(jnp.int32, (1, 1, seq), 2)

    def short_conv(x_ref, w_ref, b_ref):
        x = x_ref[...].astype(F32)
        x2 = x.reshape(rows, seq)
        prev = jnp.where(lane == 0, 0.0, pltpu.roll(x2, 1, axis=1).reshape(bt, r, seq))
        nxt = jnp.where(lane == seq - 1, 0.0, pltpu.roll(x2, seq - 1, axis=1).reshape(bt, r, seq))
        w = w_ref[...]
        return b_ref[...] + w[:, 0:1] * prev + w[:, 1:2] * x + w[:, 2:3] * nxt

    fwd = fwd_ref[...]
    inv = inv_ref[...]

    def long_conv(u, k_ref):
        ub = u.astype(BF16).reshape(rows, seq)
        spec = [jnp.dot(ub[:, j * p:(j + 1) * p], fwd, preferred_element_type=F32).reshape(bt, r, 2 * p)
                for j in range(nb)]
        outs = []
        for i in range(nb):
            acc_re = jnp.zeros((bt, r, p), F32)
            acc_im = jnp.zeros((bt, r, p), F32)
            for j in range(nb):
                m = i - j + nb - 1
                ka = k_ref[m, 0].astype(F32)
                kb = k_ref[m, 1].astype(F32)
                kc = k_ref[m, 2].astype(F32)
                u_re = spec[j][:, :, :p]
                u_im = spec[j][:, :, p:]
                acc_re = acc_re + ka * u_re - kb * u_im
                acc_im = acc_im + kc * u_im + kb * u_re
            y_spec = jnp.concatenate([acc_re, acc_im], axis=2).astype(BF16).reshape(rows, 2 * p)
            outs.append(jnp.dot(y_spec, inv, preferred_element_type=F32).reshape(bt, r, p))
        return jnp.concatenate(outs, axis=2)

    v = short_conv(v_ref, wv_ref, bv_ref)
    z = short_conv(g1_ref, w1_ref, b1_ref) * (long_conv(v, k0_ref) + v * s0_ref[...])
    y = short_conv(g2_ref, w2_ref, b2_ref) * (long_conv(z, k1_ref) + z * s1_ref[...])
    for b in range(bt):
        o_ref[b] = y[b].T.astype(o_ref.dtype)


def _hyena_mixer(p_t, conv_w_t, conv_b_col, spectra, skip_col, d_hy, p, r, bt, out_dtype):
    bsz, _, seq = p_t.shape
    nb = seq // p
    nblk = d_hy // r
    fwd, inv = _dft_matrices(p)
    x_spec = lambda off: pl.BlockSpec((bt, r, seq), lambda j, b: (b, off * nblk + j, 0))
    w_spec = lambda off: pl.BlockSpec((r, conv_w_t.shape[1]), lambda j, b: (off * nblk + j, 0))
    c_spec = lambda off: pl.BlockSpec((r, 1), lambda j, b: (off * nblk + j, 0))
    k_spec = lambda off: pl.BlockSpec((2 * nb - 1, 3, r, p), lambda j, b: (0, 0, off * nblk + j, 0))
    return pl.pallas_call(
        functools.partial(_hy_conv_kernel, p=p),
        grid=(nblk, bsz // bt),
        in_specs=[x_spec(0), x_spec(1), x_spec(2), w_spec(0), w_spec(1), w_spec(2),
                  c_spec(0), c_spec(1), c_spec(2), k_spec(0), k_spec(1), c_spec(0), c_spec(1),
                  pl.BlockSpec((p, 2 * p), lambda j, b: (0, 0)),
                  pl.BlockSpec((2 * p, p), lambda j, b: (0, 0))],
        out_specs=pl.BlockSpec((bt, seq, r), lambda j, b: (b, 0, j)),
        out_shape=jax.ShapeDtypeStruct((bsz, seq, d_hy), out_dtype),
        compiler_params=_params("parallel", "arbitrary"),
        name="hyena_mixer",
    )(p_t, p_t, p_t, conv_w_t, conv_w_t, conv_w_t, conv_b_col, conv_b_col, conv_b_col,
      spectra, spectra, skip_col, skip_col, fwd.astype(BF16), inv.astype(BF16))


def _ssd_prep_kernel(dt_ref, dtb_ref, a_ref, cum_ref, dtt_ref, *, r):
    seq = dt_ref.shape[1]
    q = SSD_CHUNK
    raw = dt_ref[0] + dtb_ref[...]
    dt = jnp.maximum(raw, 0.0) + jnp.log1p(jnp.exp(-jnp.abs(raw)))
    a = dt * a_ref[...]
    li = lax.broadcasted_iota(jnp.int32, (q, q), 0)
    si = lax.broadcasted_iota(jnp.int32, (q, q), 1)
    lower = jnp.where(li >= si, 1.0, 0.0).astype(F32)
    upper = jnp.where(li <= si, 1.0, 0.0).astype(F32)
    fwd_col = lax.broadcasted_iota(jnp.int32, (1, 128), 1) % (2 * r) < r
    for c in range(seq // q):
        ac = a[c * q:(c + 1) * q]
        cum = jnp.where(fwd_col,
                        jnp.dot(lower, ac, precision=HI, preferred_element_type=F32),
                        jnp.dot(upper, ac, precision=HI, preferred_element_type=F32))
        cum_ref[0, :, c * q:(c + 1) * q] = cum.T
        dtt_ref[0, :, c * q:(c + 1) * q] = dt[c * q:(c + 1) * q].T


def _ssd_prep(p_dt, dt_bias, a_neg, r):
    bsz, seq, _ = p_dt.shape
    vec = pl.BlockSpec((1, 128), lambda b: (0, 0))
    out = pl.BlockSpec((1, 128, seq), lambda b: (b, 0, 0))
    return pl.pallas_call(
        functools.partial(_ssd_prep_kernel, r=r),
        grid=(bsz,),
        in_specs=[pl.BlockSpec((1, seq, 128), lambda b: (b, 0, 0)), vec, vec],
        out_specs=[out, out],
        out_shape=[jax.ShapeDtypeStruct((bsz, 128, seq), F32)] * 2,
        compiler_params=_params("parallel"),
        name="ssd_prep",
    )(p_dt, dt_bias, a_neg)


def _ssd_kernel(z_ref, x_ref, bt_ref, c_ref, cum_ref, dtt_ref, wx_ref, wbt_ref, wc_ref,
                bx_ref, bbt_ref, bc_ref, dskip_ref, nw_ref, o_ref, xm_s, bt_s, cs_s, y_s, st_s):
    nbat, seq, gw = x_ref.shape
    n = bt_ref.shape[1]
    r = gw // SSD_HEADDIM
    q = SSD_CHUNK
    nc = seq // q
    half = SSD_CONV // 2
    shifts = [k - half for k in range(SSD_CONV) if k != half]
    halo_r, halo_l = 16, 128

    k_r = 2 * q
    sr = lax.broadcasted_iota(jnp.int32, (len(shifts) * q, k_r), 0)
    se = lax.broadcasted_iota(jnp.int32, (len(shifts) * q, k_r), 1)
    band_r = jnp.zeros(sr.shape, jnp.bool_)
    for j, sh in enumerate(shifts):
        band_r = jnp.logical_or(band_r, jnp.logical_and(sr // q == j, se == halo_r + sr % q + sh))
    band_r = jnp.where(band_r, 1.0, 0.0).astype(BF16)
    k_l = q + 2 * halo_l
    le = lax.broadcasted_iota(jnp.int32, (k_l, len(shifts) * q), 0)
    lc = lax.broadcasted_iota(jnp.int32, (k_l, len(shifts) * q), 1)
    band_l = jnp.zeros(le.shape, jnp.bool_)
    for j, sh in enumerate(shifts):
        band_l = jnp.logical_or(band_l, jnp.logical_and(lc // q == j, le == halo_l + lc % q + sh))
    band_l = jnp.where(band_l, 1.0, 0.0).astype(BF16)

    col_head = lax.broadcasted_iota(jnp.int32, (1, gw), 1) // SSD_HEADDIM
    wx, wc, wbt = wx_ref[...], wc_ref[...], wbt_ref[...]

    def conv_rows(ref, bb, c, off, w, bias):
        width = ref.shape[2]
        cur = ref[bb, pl.ds(off, q), :]
        prev = ref[bb, pl.ds(pl.multiple_of(jnp.maximum(off - halo_r, 0), halo_r), halo_r), :]
        nxt = ref[bb, pl.ds(pl.multiple_of(jnp.minimum(off + q, seq - halo_r), halo_r), halo_r), :]
        prev = jnp.where(c > 0, prev, jnp.zeros_like(prev))
        nxt = jnp.where(c < nc - 1, nxt, jnp.zeros_like(nxt))
        ext = jnp.concatenate([prev, cur, nxt, jnp.zeros((k_r - q - 2 * halo_r, width), BF16)], axis=0)
        moved = jnp.dot(band_r, ext, preferred_element_type=F32)
        acc = bias + w[half:half + 1] * cur.astype(F32)
        for j, sh in enumerate(shifts):
            acc = acc + w[half + sh:half + sh + 1] * moved[j * q:(j + 1) * q]
        return _silu(acc)

    def conv_one(bb, c, off):
        xs = conv_rows(x_ref, bb, c, off, wx, bx_ref[...])
        y_s[bb, 0, pl.ds(off, q), :] = xs * dskip_ref[...]
        for h in range(r):
            xm_s[bb, h, pl.ds(off, q), :] = jnp.where(col_head == h, xs, 0.0).astype(BF16)
        cs_s[bb, pl.ds(off, q), :] = conv_rows(c_ref, bb, c, off, wc, bc_ref[...]).astype(BF16)
        cur = bt_ref[bb, :, pl.ds(off, q)]
        prev = bt_ref[bb, :, pl.ds(pl.multiple_of(jnp.maximum(off - halo_l, 0), halo_l), halo_l)]
        nxt = bt_ref[bb, :, pl.ds(pl.multiple_of(jnp.minimum(off + q, seq - halo_l), halo_l), halo_l)]
        prev = jnp.where(c > 0, prev, jnp.zeros_like(prev))
        nxt = jnp.where(c < nc - 1, nxt, jnp.zeros_like(nxt))
        moved = jnp.dot(jnp.concatenate([prev, cur, nxt], axis=1), band_l, preferred_element_type=F32)
        acc = bbt_ref[...] + wbt[:, half:half + 1] * cur.astype(F32)
        for j, sh in enumerate(shifts):
            acc = acc + wbt[:, half + sh:half + sh + 1] * moved[:, j * q:(j + 1) * q]
        bt_s[bb, :, pl.ds(off, q)] = _silu(acc).astype(BF16)

    def conv_step(c, carry):
        off = pl.multiple_of(c * q, q)
        for bb in range(nbat):
            conv_one(bb, c, off)
        return carry

    lax.fori_loop(0, nc, conv_step, 0)
    st_s[...] = jnp.zeros_like(st_s)

    li = lax.broadcasted_iota(jnp.int32, (q, q), 0)
    si = lax.broadcasted_iota(jnp.int32, (q, q), 1)
    diag = li == si
    sub = lax.broadcasted_iota(jnp.int32, (2 * r, 1), 0)
    neg_inf = jnp.float32(-jnp.inf)
    nh = 2 * r
    assert 3 * nh + 16 <= q
    ones_rows = jnp.ones((8, q), F32)
    pad_rows = jnp.zeros((q - 3 * nh - 16, q), F32)
    pk = lax.broadcasted_iota(jnp.int32, (3 * nh + 8, nh * q), 0)
    pj = lax.broadcasted_iota(jnp.int32, (3 * nh + 8, nh * q), 1) // q
    seg_top = jnp.where(jnp.logical_and(pk < 3 * nh, pk % nh == pj), 1.0, 0.0).astype(BF16)
    seg_bottom = jnp.zeros((q - 3 * nh - 8 - 16, nh * q), BF16)

    def split3(v):
        hi = v.astype(BF16).astype(F32)
        mid = (v - hi).astype(BF16).astype(F32)
        lo = (v - hi - mid).astype(BF16).astype(F32)
        return hi, mid, lo

    def flat(v):
        return jnp.concatenate([v[j:j + 1, :] for j in range(nh)], axis=1)

    def chunk_step(i, carry):
        offs = (pl.multiple_of(i * q, q), pl.multiple_of((nc - 1 - i) * q, q))
        chains = [(bb, d) for bb in range(nbat) for d in range(2)]
        pre = []
        for bb in range(nbat):
            cum = jnp.where(sub < r, cum_ref[bb, :, pl.ds(offs[0], q)], cum_ref[bb, :, pl.ds(offs[1], q)])
            dt = jnp.where(sub < r, dtt_ref[bb, :, pl.ds(offs[0], q)], dtt_ref[bb, :, pl.ds(offs[1], q)])
            total = jnp.where(sub < r, cum[:, q - 1:q], cum[:, 0:1])
            hi, mid, lo = split3(cum)
            left = jnp.concatenate([hi, mid, lo, jnp.zeros((8, q), F32), ones_rows, pad_rows], axis=0)
            left = left.T.astype(BF16)
            minus = jnp.concatenate([-flat(hi), -flat(mid), -flat(lo), jnp.zeros((13, nh * q), F32)],
                                    axis=0)
            right = jnp.concatenate([seg_top, minus.astype(BF16), seg_bottom], axis=0)
            pre.append(dict(dt=dt, carry_decay=jnp.exp(total), in_decay=jnp.exp(cum),
                            out_w=jnp.exp(total - cum) * dt, left=left, right=right))
        seg_all = [jnp.dot(p_["left"], p_["right"], preferred_element_type=F32) for p_ in pre]
        btc = {c: bt_s[c[0], :, pl.ds(offs[c[1]], q)] for c in chains}
        cc = {c: cs_s[c[0], pl.ds(offs[c[1]], q), :] for c in chains}
        state = {c: st_s[c[0], c[1]] for c in chains}
        cb = {c: jnp.dot(cc[c], btc[c], preferred_element_type=F32) for c in chains}
        c_state = {c: jnp.dot(cc[c], state[c].astype(BF16), preferred_element_type=F32) for c in chains}
        lhs_y, rhs_y, lhs_s, rhs_s, decays = {}, {}, {}, {}, {}
        for c in chains:
            bb, d = c
            p_ = pre[bb]
            tri = (li >= si) if d == 0 else (li <= si)
            btf = btc[c].astype(F32)
            mats, diags, xs_h, cs_h, b_scaled = [], [], [], [], []
            state_decay = jnp.zeros((1, gw), F32)
            for h in range(r):
                ln = d * r + h
                seg = seg_all[bb][:, ln * q:(ln + 1) * q]
                mats.append((cb[c] * jnp.exp(jnp.where(tri, seg, neg_inf)) * p_["dt"][ln:ln + 1, :])
                            .astype(BF16))
                diags.append(jnp.where(diag, p_["in_decay"][ln:ln + 1, :], 0.0).astype(BF16))
                xs_h.append(xm_s[bb, h, pl.ds(offs[d], q), :])
                cs_h.append(jnp.where(col_head == h, c_state[c], 0.0).astype(BF16))
                b_scaled.append((btf * p_["out_w"][ln:ln + 1, :]).astype(BF16))
                state_decay = jnp.where(col_head == h, p_["carry_decay"][ln:ln + 1, :], state_decay)
            lhs_y[c] = jnp.concatenate(mats + diags, axis=1)
            rhs_y[c] = jnp.concatenate(xs_h + cs_h, axis=0)
            lhs_s[c] = jnp.concatenate(b_scaled, axis=1)
            rhs_s[c] = jnp.concatenate(xs_h, axis=0)
            decays[c] = state_decay
        y_new = {c: jnp.dot(lhs_y[c], rhs_y[c], preferred_element_type=F32) for c in chains}
        upd = {c: jnp.dot(lhs_s[c], rhs_s[c], preferred_element_type=F32) for c in chains}
        for c in chains:
            bb, d = c
            y_s[bb, 1 + d, pl.ds(offs[d], q), :] = y_new[c]
            st_s[bb, d] = state[c] * decays[c] + upd[c]
        return carry

    lax.fori_loop(0, nc, chunk_step, 0, unroll=2)

    def gate_step(c, carry):
        off = pl.multiple_of(c * q, q)
        for bb in range(nbat):
            zf = z_ref[bb, pl.ds(off, q), :].astype(F32)
            y = (y_s[bb, 0, pl.ds(off, q), :] + y_s[bb, 1, pl.ds(off, q), :]
                 + y_s[bb, 2, pl.ds(off, q), :])
            y = y * _silu(zf)
            y = y * lax.rsqrt(jnp.mean(jnp.square(y), axis=-1, keepdims=True) + RMS_EPS)
            o_ref[bb, pl.ds(off, q), :] = (y * nw_ref[...]).astype(o_ref.dtype)
        return carry

    lax.fori_loop(0, nc, gate_step, 0)


def _ssd_mixer(pm, p_t, bt_off, cum_t, dt_t, conv_w, conv_b, dskip_cols, norm_w, d_ssm, out_dtype):
    bsz, seq, _ = pm.shape
    gw = d_ssm // SSD_GROUPS
    r = gw // SSD_HEADDIM
    n = SSD_STATE
    xo, co = d_ssm // gw, 2 * d_ssm // n
    cw = lambda width, off: pl.BlockSpec((SSD_CONV, width), lambda b, g: (0, off + g))
    cb = lambda width, off: pl.BlockSpec((1, width), lambda b, g: (0, off + g))
    nbat = _pick(bsz, 2, 1)
    heads_rows = pl.BlockSpec((nbat, 2 * r, seq), lambda b, g: (b, g, 0))
    return pl.pallas_call(
        _ssd_kernel,
        grid=(bsz // nbat, SSD_GROUPS),
        in_specs=[pl.BlockSpec((nbat, seq, gw), lambda b, g: (b, 0, g)),
                  pl.BlockSpec((nbat, seq, gw), lambda b, g: (b, 0, xo + g)),
                  pl.BlockSpec((nbat, n, seq), lambda b, g: (b, bt_off + g, 0)),
                  pl.BlockSpec((nbat, seq, n), lambda b, g: (b, 0, co + g)),
                  heads_rows, heads_rows,
                  cw(gw, 0),
                  pl.BlockSpec((n, SSD_CONV), lambda b, g: (d_ssm // n + g, 0)),
                  cw(n, d_ssm // n + SSD_GROUPS),
                  cb(gw, 0),
                  pl.BlockSpec((n, 1), lambda b, g: (d_ssm // n + g, 0)),
                  cb(n, d_ssm // n + SSD_GROUPS),
                  cb(gw, 0), cb(gw, 0)],
        out_specs=pl.BlockSpec((nbat, seq, gw), lambda b, g: (b, 0, g)),
        out_shape=jax.ShapeDtypeStruct((bsz, seq, d_ssm), out_dtype),
        scratch_shapes=[pltpu.VMEM((nbat, r, seq, gw), BF16),
                        pltpu.VMEM((nbat, n, seq), BF16), pltpu.VMEM((nbat, seq, n), BF16),
                        pltpu.VMEM((nbat, 3, seq, gw), F32), pltpu.VMEM((nbat, 2, n, gw), F32)],
        compiler_params=_params("parallel", "parallel"),
        name="ssd_mixer",
    )(pm, pm, p_t, pm, cum_t, dt_t, conv_w, conv_w.T, conv_w, conv_b, conv_b.reshape(-1, 1), conv_b,
      dskip_cols, norm_w)


def _layer_norm(v, g, b):
    mu = jnp.mean(v, axis=-1, keepdims=True)
    c = v - mu
    var = jnp.mean(jnp.square(c), axis=-1, keepdims=True)
    return c * lax.rsqrt(var + LN_EPS) * g + b


def _merge_kernel(yh_ref, ys_ref, gh_ref, gs_ref, x_ref, whb_ref, wsb_ref, wo_ref, bo_ref,
                  lg_ref, lb_ref, rwt_ref, h_ref, hp_ref, lt_ref):
    a = jnp.dot(yh_ref[...], whb_ref[...], preferred_element_type=F32)
    b = jnp.dot(ys_ref[...], wsb_ref[...], preferred_element_type=F32)
    m = _sigmoid(gh_ref[...].astype(F32)) * a + _sigmoid(gs_ref[...].astype(F32)) * b
    mix = jnp.dot(m.astype(BF16), wo_ref[...], preferred_element_type=F32) + bo_ref[...]
    h = _layer_norm(DN_ALPHA * x_ref[...] + mix, lg_ref[...], lb_ref[...])
    h_ref[...] = h
    hp_ref[...] = _pack_halves(h)
    lt_ref[...] = lax.dot_general(rwt_ref[...], h, (((1,), (1,)), ((), ())), precision=HI,
                                  preferred_element_type=F32)


def _merge(yhy, yssd, pm, gate_off, x2d, whb, wsb, wo, bo, lg, lb, rwt, tm):
    t, d = x2d.shape
    c = yhy.shape[1]
    d_ssm = yssd.shape[1]
    e = rwt.shape[0]
    row = lambda w: pl.BlockSpec((tm, w), lambda i: (i, 0))
    whole = lambda a: pl.BlockSpec(a.shape, lambda i: (0,) * a.ndim)
    return pl.pallas_call(
        _merge_kernel,
        grid=(t // tm,),
        in_specs=[row(c), row(d_ssm),
                  pl.BlockSpec((tm, d), lambda i: (i, gate_off)),
                  pl.BlockSpec((tm, d), lambda i: (i, gate_off + 1)),
                  row(d), whole(whb), whole(wsb), whole(wo), whole(bo), whole(lg), whole(lb), whole(rwt)],
        out_specs=[row(d), row(d // 2), pl.BlockSpec((e, tm), lambda i: (0, i))],
        out_shape=[jax.ShapeDtypeStruct((t, d), F32), jax.ShapeDtypeStruct((t, d // 2), jnp.int32),
                   jax.ShapeDtypeStruct((e, t), F32)],
        compiler_params=_params("parallel"),
        name="merge_ln",
    )(yhy, yssd, pm, pm, x2d, whb, wsb, wo, bo, lg, lb, rwt)


def _first_max(v, axis, size):
    idx = lax.broadcasted_iota(jnp.int32, v.shape, axis)
    top = jnp.max(v, axis=axis, keepdims=True)
    first = jnp.min(jnp.where(v == top, idx, size), axis=axis, keepdims=True)
    return idx == first, top


def _route_kernel(lt_ref, bias_ref, w_ref, sel_ref):
    n_e, tn = lt_ref.shape
    per = n_e // N_EXPERT_GROUPS
    scores = jax.nn.sigmoid(lt_ref[...])
    biased = scores + bias_ref[...]
    b3 = biased.reshape(N_EXPERT_GROUPS, per, tn)
    hit, top1 = _first_max(b3, 1, per)
    top2 = jnp.max(jnp.where(hit, -jnp.inf, b3), axis=1, keepdims=True)
    cur = (top1 + top2).reshape(N_EXPERT_GROUPS, tn)
    grp = jnp.zeros(cur.shape, jnp.bool_)
    for _ in range(TOPK_GROUPS):
        hit, _ = _first_max(cur, 0, N_EXPERT_GROUPS)
        grp = jnp.logical_or(grp, hit)
        cur = jnp.where(hit, -jnp.inf, cur)
    cur = jnp.where(grp.reshape(N_EXPERT_GROUPS, 1, tn), b3, -jnp.inf).reshape(n_e, tn)
    chosen = jnp.zeros(cur.shape, jnp.bool_)
    for _ in range(TOP_K):
        hit, _ = _first_max(cur, 0, n_e)
        chosen = jnp.logical_or(chosen, hit)
        cur = jnp.where(hit, -jnp.inf, cur)
    w = jnp.where(chosen, scores, 0.0)
    w_ref[...] = w / jnp.sum(w, axis=0, keepdims=True) * ROUTED_SCALE
    sel_ref[...] = jnp.where(chosen, 1.0, 0.0).astype(sel_ref.dtype)


def _route(logits_t, bias_col, tn):
    e, t = logits_t.shape
    blk = pl.BlockSpec((e, tn), lambda i: (0, i))
    return pl.pallas_call(
        _route_kernel,
        grid=(t // tn,),
        in_specs=[blk, pl.BlockSpec((e, 1), lambda i: (0, 0))],
        out_specs=[blk, blk],
        out_shape=[jax.ShapeDtypeStruct((e, t), F32), jax.ShapeDtypeStruct((e, t), BF16)],
        compiler_params=_params("parallel"),
        name="route_topk",
    )(logits_t, bias_col)


MOE_BLOCK = 512
MOE_TOKEN_RANGES = 2


def _rank_kernel(sel_ref, pos_ref, cnt_ref, carry_ref):
    n_e, tn = sel_ref.shape

    @pl.when(pl.program_id(0) == 0)
    def _():
        carry_ref[...] = jnp.zeros_like(carry_ref)

    sel = sel_ref[...]
    s = lax.broadcasted_iota(jnp.int32, (tn, tn), 0)
    t = lax.broadcasted_iota(jnp.int32, (tn, tn), 1)
    earlier = jnp.where(s < t, 1.0, 0.0).astype(BF16)
    before = jnp.dot(sel, earlier, preferred_element_type=F32)
    carry = carry_ref[...]
    pos_ref[...] = carry[:, 0:1] + before
    carry = carry + jnp.sum(sel.astype(F32), axis=1, keepdims=True)
    carry_ref[...] = carry
    cnt_ref[...] = carry


def _rank(sel_t, tn):
    e, t = sel_t.shape
    return pl.pallas_call(
        _rank_kernel,
        grid=(t // tn,),
        in_specs=[pl.BlockSpec((e, tn), lambda i: (0, i))],
        out_specs=[pl.BlockSpec((e, tn), lambda i: (0, i)), pl.BlockSpec((e, 128), lambda i: (0, 0))],
        out_shape=[jax.ShapeDtypeStruct((e, t), F32), jax.ShapeDtypeStruct((e, 128), F32)],
        scratch_shapes=[pltpu.VMEM((e, 128), F32)],
        compiler_params=_params("arbitrary"),
        name="moe_rank",
    )(sel_t)


def _segment_starts(cnt):
    n_e = cnt.shape[0]
    padded = jnp.floor((cnt + (MOE_BLOCK - 1)) * (1.0 / MOE_BLOCK)) * MOE_BLOCK
    a = lax.broadcasted_iota(jnp.int32, (n_e, n_e), 0)
    b = lax.broadcasted_iota(jnp.int32, (n_e, n_e), 1)
    below = jnp.where(b < a, 1.0, 0.0).astype(F32)
    return padded, jnp.dot(below, padded, precision=HI, preferred_element_type=F32)


def _slots_kernel(sel_ref, w_ref, pos_ref, cnt_ref, slot_ref, wk_ref):
    n_e, tn = sel_ref.shape
    _, start = _segment_starts(cnt_ref[...])
    slot = start[:, 0:1] + pos_ref[...]
    w = w_ref[...]
    cur = sel_ref[...].astype(F32)
    idx = lax.broadcasted_iota(jnp.int32, (n_e, tn), 0)
    slots, weights = [], []
    for _ in range(TOP_K):
        first = jnp.min(jnp.where(cur > 0.0, idx, n_e), axis=0, keepdims=True)
        hit = idx == first
        slots.append(jnp.sum(jnp.where(hit, slot, 0.0), axis=0, keepdims=True))
        weights.append(jnp.sum(jnp.where(hit, w, 0.0), axis=0, keepdims=True))
        cur = jnp.where(hit, 0.0, cur)
    slot_ref[...] = jnp.concatenate(slots, axis=0).astype(jnp.int32)
    wk_ref[...] = jnp.concatenate(weights, axis=0).T


def _slots(sel_t, w_t, pos_t, cnt, tn):
    e, t = sel_t.shape
    blk = pl.BlockSpec((e, tn), lambda i: (0, i))
    return pl.pallas_call(
        _slots_kernel,
        grid=(t // tn,),
        in_specs=[blk, blk, blk, pl.BlockSpec((e, 128), lambda i: (0, 0))],
        out_specs=[pl.BlockSpec((TOP_K, tn), lambda i: (0, i)), pl.BlockSpec((tn, TOP_K), lambda i: (i, 0))],
        out_shape=[jax.ShapeDtypeStruct((TOP_K, t), jnp.int32), jax.ShapeDtypeStruct((t, TOP_K), F32)],
        compiler_params=_params("parallel"),
        name="moe_slots",
    )(sel_t, w_t, pos_t, cnt)


def _plan_kernel(cnt_ref, expert_ref, valid_ref):
    n_e = cnt_ref.shape[0]
    nbp = expert_ref.shape[1]
    cnt = cnt_ref[...]
    padded, start = _segment_starts(cnt)
    end = (start + padded)[:, 0:1]
    last = (start + cnt)[:, 0:1]
    row0 = (lax.broadcasted_iota(jnp.int32, (1, nbp), 1) * MOE_BLOCK).astype(F32)
    expert = jnp.minimum(jnp.sum(jnp.where(end <= row0, 1.0, 0.0), axis=0, keepdims=True), n_e - 1.0)
    eid = lax.broadcasted_iota(jnp.int32, (n_e, nbp), 0).astype(F32)
    last_b = jnp.sum(jnp.where(eid == expert, last, 0.0), axis=0, keepdims=True)
    expert_ref[...] = expert.astype(jnp.int32)
    valid_ref[...] = jnp.clip(last_b - row0, 0.0, float(MOE_BLOCK)).astype(jnp.int32)


def _plan(cnt, n_blocks):
    nbp = -(-n_blocks // 128) * 128
    return pl.pallas_call(
        _plan_kernel,
        out_shape=[jax.ShapeDtypeStruct((1, nbp), jnp.int32)] * 2,
        name="moe_plan",
    )(cnt)


V7X_SC_CORES = 2
V7X_SC_SUBCORES = 16
SC_WINDOW = 128


def _sc_mesh():
    return plsc.VectorSubcoreMesh(core_axis_name="c", subcore_axis_name="s",
                                  num_cores=V7X_SC_CORES, num_subcores=V7X_SC_SUBCORES)


def _sc_scatter_rows(rows, row0, idx, n_out):
    width = rows.shape[1]
    k_n, t = idx.shape
    workers = V7X_SC_CORES * V7X_SC_SUBCORES
    per_worker = t // workers
    assert per_worker * workers == t and per_worker % SC_WINDOW == 0

    def body(rows_hbm, idx_hbm, out_hbm, idx_v, rows_v, sem):
        wid = lax.axis_index("s") * V7X_SC_CORES + lax.axis_index("c")

        @pl.loop(0, per_worker // SC_WINDOW)
        def _(j):
            base = wid * per_worker + j * SC_WINDOW
            pltpu.sync_copy(rows_hbm.at[pl.ds(row0 + base, SC_WINDOW)], rows_v)
            for k in range(k_n):
                pltpu.sync_copy(idx_hbm.at[pl.ds(k * t + base, SC_WINDOW)], idx_v)
                pltpu.async_copy(rows_v, out_hbm.at[idx_v], sem).wait()

    return pl.kernel(
        body, out_type=jax.ShapeDtypeStruct((n_out, width), rows.dtype), mesh=_sc_mesh(),
        scratch_types=[pltpu.VMEM((SC_WINDOW,), jnp.int32), pltpu.VMEM((SC_WINDOW, width), rows.dtype),
                       pltpu.SemaphoreType.DMA],
        name="moe_dispatch",
    )(rows, idx.reshape(-1))


def _sc_gather_rows(table, idx):
    width = table.shape[1]
    k_n, t = idx.shape
    workers = V7X_SC_CORES * V7X_SC_SUBCORES
    per_worker = t // workers
    assert per_worker * workers == t and per_worker % SC_WINDOW == 0

    def body(table_hbm, idx_hbm, out_hbm, idx_v, rows_v, sem):
        wid = lax.axis_index("s") * V7X_SC_CORES + lax.axis_index("c")

        @pl.loop(0, per_worker // SC_WINDOW)
        def _(j):
            base = wid * per_worker + j * SC_WINDOW
            for k in range(k_n):
                pltpu.sync_copy(idx_hbm.at[pl.ds(k * t + base, SC_WINDOW)], idx_v)
                pltpu.async_copy(table_hbm.at[idx_v], rows_v, sem).wait()
                pltpu.sync_copy(rows_v, out_hbm.at[pl.ds(k * t + base, SC_WINDOW)])

    out = pl.kernel(
        body, out_type=jax.ShapeDtypeStruct((k_n * t, width), table.dtype), mesh=_sc_mesh(),
        scratch_types=[pltpu.VMEM((SC_WINDOW,), jnp.int32), pltpu.VMEM((SC_WINDOW, width), table.dtype),
                       pltpu.SemaphoreType.DMA],
        name="moe_collect",
    )(table, idx.reshape(-1))
    return out.reshape(k_n, t, width)


def _swiglu(xb, wg, wu):
    gate = jnp.dot(xb, wg, preferred_element_type=F32)
    up = jnp.dot(xb, wu, preferred_element_type=F32)
    return _silu(gate) * up


def _pack_halves(v):
    w = v.shape[1] // 2
    hi = pltpu.bitcast(v[:, :w].astype(BF16).astype(F32), jnp.int32)
    lo = pltpu.bitcast(v[:, w:].astype(BF16).astype(F32), jnp.int32)
    return jnp.bitwise_or(hi, lax.shift_right_logical(lo, 16))


def _unpack_halves(p):
    hi = pltpu.bitcast(jnp.bitwise_and(p, jnp.int32(-65536)), F32)
    lo = pltpu.bitcast(lax.shift_left(p, 16), F32)
    return jnp.concatenate([hi, lo], axis=1)


def _experts_kernel(expert_ref, valid_ref, x_ref, wg_ref, wu_ref, wd_ref, y_ref, wg_s, wu_s, wd_s):
    b = pl.program_id(0)
    valid = valid_ref[0, b]
    fresh = jnp.logical_or(b == 0, expert_ref[0, b] != expert_ref[0, jnp.maximum(b - 1, 0)])

    @pl.when(jnp.logical_and(valid > 0, fresh))
    def _():
        wg_s[...] = wg_ref[0].astype(BF16)
        wu_s[...] = wu_ref[0].astype(BF16)
        wd_s[...] = wd_ref[0].astype(BF16)

    @pl.when(valid > 0)
    def _():
        rows = lax.broadcasted_iota(jnp.int32, (x_ref.shape[0], 1), 0)
        x = jnp.where(rows < valid, _unpack_halves(x_ref[...]), 0.0).astype(BF16)
        mid = _swiglu(x, wg_s[...], wu_s[...]).astype(BF16)
        y_ref[...] = _pack_halves(jnp.dot(mid, wd_s[...], preferred_element_type=F32))

    @pl.when(valid == 0)
    def _():
        y_ref[...] = jnp.zeros_like(y_ref)


def _experts(x_sorted, expert, valid, wg, wu, wd):
    n_rows, half = x_sorted.shape
    n_e, d, f = wg.shape
    by_expert = lambda shape: pl.BlockSpec((1,) + shape, lambda b, e_ref, v_ref: (e_ref[0, b], 0, 0))
    rows = pl.BlockSpec((MOE_BLOCK, half), lambda b, e_ref, v_ref: (b, 0))
    return pl.pallas_call(
        _experts_kernel,
        grid_spec=pltpu.PrefetchScalarGridSpec(
            num_scalar_prefetch=2, grid=(n_rows // MOE_BLOCK,),
            in_specs=[rows, by_expert((d, f)), by_expert((d, f)), by_expert((f, d))],
            out_specs=rows,
            scratch_shapes=[pltpu.VMEM((d, f), BF16), pltpu.VMEM((d, f), BF16), pltpu.VMEM((f, d), BF16)]),
        out_shape=jax.ShapeDtypeStruct((n_rows, half), jnp.int32),
        compiler_params=_params("arbitrary"),
        name="moe_experts",
    )(expert, valid, x_sorted, wg, wu, wd)


def _combine_kernel(y_ref, wk_ref, hp_ref, h_ref, sg_ref, su_ref, sd_ref, lg_ref, lb_ref, o_ref):
    wk = wk_ref[...]
    mid = _swiglu(_unpack_halves(hp_ref[...]).astype(BF16), sg_ref[...], su_ref[...]).astype(BF16)
    acc = jnp.dot(mid, sd_ref[...], preferred_element_type=F32)
    for k in range(y_ref.shape[0]):
        acc = acc + wk[:, k:k + 1] * _unpack_halves(y_ref[k])
    o_ref[...] = _layer_norm(DN_ALPHA * h_ref[...] + acc, lg_ref[...], lb_ref[...])


def _combine_kernel_into(y_ref, wk_ref, hp_ref, h_ref, sg_ref, su_ref, sd_ref, lg_ref, lb_ref, prev_ref,
                         o_ref):
    del prev_ref
    _combine_kernel(y_ref, wk_ref, hp_ref, h_ref, sg_ref, su_ref, sd_ref, lg_ref, lb_ref, o_ref)


def _combine(y_tok, wk, hp, h, row0, prev, sg, su, sd, lg, lb, tm):
    t, d = h.shape
    k_n, t_part, half = y_tok.shape
    b0 = row0 // tm
    part = lambda w: pl.BlockSpec((tm, w), lambda i: (i, 0))
    full = lambda w: pl.BlockSpec((tm, w), lambda i: (b0 + i, 0))
    whole = lambda a: pl.BlockSpec(a.shape, lambda i: (0,) * a.ndim)
    in_specs = [pl.BlockSpec((k_n, tm, half), lambda i: (0, i, 0)), part(k_n), full(half), full(d),
                whole(sg), whole(su), whole(sd), whole(lg), whole(lb)]
    args = [y_tok, wk, hp, h, sg, su, sd, lg, lb]
    if prev is not None:
        in_specs.append(pl.BlockSpec(memory_space=pl.ANY))
        args.append(prev)
    return pl.pallas_call(
        _combine_kernel if prev is None else _combine_kernel_into,
        grid=(t_part // tm,),
        in_specs=in_specs,
        out_specs=full(d),
        out_shape=jax.ShapeDtypeStruct((t, d), F32),
        input_output_aliases={} if prev is None else {len(args) - 1: 0},
        compiler_params=_params("parallel"),
        name="moe_combine_ln",
    )(*args)


def _pick(n, *cands):
    for c in cands:
        if n % c == 0:
            return c
    return n


def _layer(h, w_in, b_in, hy_conv_w, hy_conv_b, hy_f_w1, hy_f_b1, hy_f_freq1, hy_f_w2, hy_f_b2,
           hy_f_freq2, hy_f_w3, hy_f_b3, hy_f_freq3, hy_f_w4, hy_bias, ssd_conv_w, ssd_conv_b,
           ssd_dt_bias, ssd_a_log, ssd_d, ssd_norm_w, w_hy_branch, w_ssd_branch, w_out, b_out,
           ln1_g, ln1_b, router_w, router_bias, exp_w_gate, exp_w_up, exp_w_down,
           sh_w_gate, sh_w_up, sh_w_down, ln2_g, ln2_b):
    bsz, seq, d = h.shape
    t = bsz * seq
    d_hy = hy_bias.shape[-1]
    d_ssm = ssd_norm_w.shape[-1]
    conv_dim = ssd_conv_w.shape[-1]
    heads = ssd_d.shape[-1]
    hy_cols = (HY_ORDER + 1) * d_hy
    o_z, o_dt, o_gate = hy_cols, hy_cols + d_ssm + conv_dim, hy_cols + d_ssm + conv_dim + 2 * heads
    o_b = o_z + 2 * d_ssm
    o_c = o_b + SSD_GROUPS * SSD_STATE
    row = lambda v: v.astype(F32).reshape(1, -1)
    col = lambda v: v.astype(F32).reshape(-1, 1)

    xb = h.astype(BF16)
    x2d = xb.reshape(t, d)
    tm = _pick(t, 1024, 512, 256, 128)

    ch = lambda v: jnp.concatenate([v[..., :o_z], v[..., o_b:o_c]], axis=-1)
    p_t = _matmul_channels_major(ch(w_in).T.astype(BF16), xb, col(ch(b_in)), BF16,
                                 _pick(hy_cols + o_c - o_b, 1024, 512, 384, 128),
                                 _pick(seq, 2048, 1024, 512, 256, 128))
    tmaj = lambda v: jnp.concatenate([v[..., o_z:o_b], v[..., o_c:o_dt], v[..., o_gate:]], axis=-1)
    w_main = tmaj(w_in).astype(BF16)
    pm = _matmul_bias(x2d, w_main, row(tmaj(b_in)), BF16, tm, _pick(w_main.shape[1], 1024, 512, 256, 128))
    pad = 128 - 2 * heads
    hpg = heads // SSD_GROUPS
    perm = [dr * heads + g * hpg + hh for g in range(SSD_GROUPS) for dr in range(2) for hh in range(hpg)]
    by_group = lambda v: jnp.pad(v.astype(F32).reshape(-1)[jnp.array(perm)], (0, pad)).reshape(1, -1)
    w_dt = jnp.pad(w_in[:, o_dt:o_gate][:, jnp.array(perm)], ((0, 0), (0, pad))).astype(BF16)
    p_dt = _matmul_bias(x2d, w_dt, by_group(b_in[o_dt:o_gate]), F32, tm, 128)

    p = _pick(seq, 512, 256, 128, 64)
    r = _pick(d_hy, 128, 64)
    spectra = _hyena_spectra(seq, p, hy_f_w1, hy_f_b1, hy_f_freq1, hy_f_w2, hy_f_b2, hy_f_freq2,
                             hy_f_w3, hy_f_b3, hy_f_freq3, hy_f_w4, d_hy, r, BF16)
    y_hy = _hyena_mixer(p_t, hy_conv_w.T.astype(F32), col(hy_conv_b), spectra, col(hy_bias),
                        d_hy, p, r, _pick(bsz, 2, 1), BF16)

    cum_t, dt_t = _ssd_prep(p_dt.reshape(bsz, seq, 128), by_group(ssd_dt_bias),
                            by_group(-jnp.exp(ssd_a_log.astype(F32))), hpg)
    y_ssd = _ssd_mixer(pm.reshape(bsz, seq, -1), p_t, hy_cols // SSD_STATE, cum_t, dt_t,
                       ssd_conv_w.astype(F32), row(ssd_conv_b),
                       row(jnp.repeat(ssd_d, SSD_HEADDIM)), row(ssd_norm_w), d_ssm, BF16)

    h1, h1p, logits_t = _merge(y_hy.reshape(t, d_hy), y_ssd.reshape(t, d_ssm), pm,
                               (o_dt - o_z - (o_c - o_b)) // d, h.reshape(t, d).astype(F32),
                               w_hy_branch.astype(BF16), w_ssd_branch.astype(BF16), w_out.astype(BF16),
                               row(b_out), row(ln1_g), row(ln1_b), router_w.T.astype(F32),
                               _pick(t, 512, 256, 128))

    n_parts = MOE_TOKEN_RANGES if t % (MOE_TOKEN_RANGES * V7X_SC_CORES * V7X_SC_SUBCORES * SC_WINDOW) == 0 else 1
    tp = t // n_parts
    tn = _pick(tp, 512, 256, 128)
    w_t, sel_t = _route(logits_t, col(router_bias), tn)
    n_blocks = -(-tp * TOP_K // MOE_BLOCK) + router_w.shape[-1]
    shared = (sh_w_gate.astype(BF16), sh_w_up.astype(BF16), sh_w_down.astype(BF16))
    experts_w = (exp_w_gate.astype(F32), exp_w_up.astype(F32), exp_w_down.astype(F32))
    out = None
    for part in range(n_parts):
        cols = slice(part * tp, (part + 1) * tp)
        sel_p, w_p = sel_t[:, cols], w_t[:, cols]
        pos_t, cnt = _rank(sel_p, tn)
        slot_kt, wk = _slots(sel_p, w_p, pos_t, cnt, tn)
        expert, valid = _plan(cnt, n_blocks)
        x_sorted = _sc_scatter_rows(h1p, part * tp, slot_kt, n_blocks * MOE_BLOCK)
        y_sorted = _experts(x_sorted, expert, valid, *experts_w)
        y_tok = _sc_gather_rows(y_sorted, slot_kt)
        out = _combine(y_tok, wk, h1p, h1, part * tp, out, *shared, row(ln2_g), row(ln2_b), tn)
    return out.reshape(bsz, seq, d)


def kernel(x, w_in, b_in, hy_conv_w, hy_conv_b, hy_f_w1, hy_f_b1, hy_f_freq1, hy_f_w2, hy_f_b2,
           hy_f_freq2, hy_f_w3, hy_f_b3, hy_f_freq3, hy_f_w4, hy_bias, ssd_conv_w, ssd_conv_b,
           ssd_dt_bias, ssd_a_log, ssd_d, ssd_norm_w, w_hy_branch, w_ssd_branch, w_out, b_out,
           ln1_g, ln1_b, router_w, router_bias, exp_w_gate, exp_w_up, exp_w_down,
           sh_w_gate, sh_w_up, sh_w_down, ln2_g, ln2_b):
    params = (w_in, b_in, hy_conv_w, hy_conv_b, hy_f_w1, hy_f_b1, hy_f_freq1, hy_f_w2, hy_f_b2,
              hy_f_freq2, hy_f_w3, hy_f_b3, hy_f_freq3, hy_f_w4, hy_bias, ssd_conv_w, ssd_conv_b,
              ssd_dt_bias, ssd_a_log, ssd_d, ssd_norm_w, w_hy_branch, w_ssd_branch, w_out, b_out,
              ln1_g, ln1_b, router_w, router_bias, exp_w_gate, exp_w_up, exp_w_down,
              sh_w_gate, sh_w_up, sh_w_down, ln2_g, ln2_b)
    h = x
    for i in range(DEPTH):
        h = _layer(h, *(p[i] for p in params))
    return h
```

```python
import functools
import math

import jax
import jax.numpy as jnp
from jax import lax
from jax.experimental import pallas as pl
from jax.experimental.pallas import tpu as pltpu
from jax.experimental.pallas import tpu_sc as plsc

F32 = jnp.float32
BF16 = jnp.bfloat16
HI = lax.Precision.HIGHEST

HY_ORDER = 2
HY_EMB_DIM = 33
HY_BANDS = (HY_EMB_DIM - 1) // 2
HY_FAST_DECAY_PCT = 0.3
HY_SLOW_DECAY_PCT = 1.5
HY_DECAY_TARGET = 1e-2
SSD_HEADDIM = 64
SSD_GROUPS = 8
SSD_STATE = 128
SSD_CONV = 5
SSD_CHUNK = 128
N_EXPERT_GROUPS = 8
TOPK_GROUPS = 4
TOP_K = 8
ROUTED_SCALE = 2.5
DEPTH = 1
DN_ALPHA = (2.0 * DEPTH) ** 0.25
LN_EPS = 1e-5
RMS_EPS = 1e-5

V7X_VMEM_LIMIT = 56 * 1024 * 1024


def _params(*sem):
    return pltpu.CompilerParams(dimension_semantics=sem, vmem_limit_bytes=V7X_VMEM_LIMIT)


def _sigmoid(v):
    return 0.5 + 0.5 * jnp.tanh(0.5 * v)


def _silu(v):
    hv = 0.5 * v
    return hv + hv * jnp.tanh(hv)


def _mm_bias_kernel(a_ref, b_ref, bias_ref, o_ref):
    acc = jnp.dot(a_ref[...], b_ref[...], preferred_element_type=F32)
    o_ref[...] = (acc + bias_ref[...]).astype(o_ref.dtype)


def _matmul_bias(a, b, bias, out_dtype, tm, tn):
    m, k = a.shape
    n = b.shape[1]
    return pl.pallas_call(
        _mm_bias_kernel,
        grid=(m // tm, n // tn),
        in_specs=[pl.BlockSpec((tm, k), lambda i, j: (i, 0)),
                  pl.BlockSpec((k, tn), lambda i, j: (0, j)),
                  pl.BlockSpec((1, tn), lambda i, j: (0, j))],
        out_specs=pl.BlockSpec((tm, tn), lambda i, j: (i, j)),
        out_shape=jax.ShapeDtypeStruct((m, n), out_dtype),
        compiler_params=_params("parallel", "parallel"),
        name="proj_rows",
    )(a, b, bias)


def _mm_nt_kernel(w_ref, x_ref, bias_ref, o_ref):
    acc = lax.dot_general(w_ref[...], x_ref[0], (((1,), (1,)), ((), ())),
                          preferred_element_type=F32)
    o_ref[0] = (acc + bias_ref[...]).astype(o_ref.dtype)


def _matmul_channels_major(w_t, x, bias_col, out_dtype, tc):
    c, k = w_t.shape
    bsz, seq, _ = x.shape
    return pl.pallas_call(
        _mm_nt_kernel,
        grid=(bsz, c // tc),
        in_specs=[pl.BlockSpec((tc, k), lambda b, i: (i, 0)),
                  pl.BlockSpec((1, seq, k), lambda b, i: (b, 0, 0)),
                  pl.BlockSpec((tc, 1), lambda b, i: (i, 0))],
        out_specs=pl.BlockSpec((1, tc, seq), lambda b, i: (b, i, 0)),
        out_shape=jax.ShapeDtypeStruct((bsz, c, seq), out_dtype),
        compiler_params=_params("parallel", "parallel"),
        name="proj_channels",
    )(w_t, x, bias_col)


CONV_BORDER = 8


def _mm_conv_kernel(x_ref, w_ref, bias_ref, cw_ref, cb_ref, o_ref, pre_a, pre_b):
    seq = o_ref.shape[1]
    q = SSD_CHUNK
    half = SSD_CONV // 2
    border = CONV_BORDER
    step = pl.program_id(0)

    @pl.when(step == 0)
    def _():
        pre_a[...] = jnp.zeros_like(pre_a)
        pre_b[...] = jnp.zeros_like(pre_b)

    def run(pre_new, pre_old):
        pre_new[border:seq + border, :] = (jnp.dot(x_ref[0], w_ref[...], preferred_element_type=F32)
                                           + bias_ref[...])
        cw = cw_ref[...]
        cb = cb_ref[...]
        for c in range(seq // q):
            ext = pre_old[c * q:(c + 1) * q + 2 * border, :]
            acc = cb
            for k in range(SSD_CONV):
                lo = border + k - half
                acc = acc + cw[k:k + 1] * ext[lo:lo + q]
            o_ref[0, c * q:(c + 1) * q, :] = _silu(acc).astype(o_ref.dtype)

    pl.when(step % 2 == 0)(lambda: run(pre_a, pre_b))
    pl.when(step % 2 == 1)(lambda: run(pre_b, pre_a))


def _matmul_conv_rows(x, w, bias, conv_w, conv_b, out_dtype, tn):
    bsz, seq, k = x.shape
    n = w.shape[1]
    nt = n // tn
    items = bsz * nt
    mm = lambda s: jnp.minimum(s, items - 1)
    cv = lambda s: jnp.maximum(s - 1, 0)
    return pl.pallas_call(
        _mm_conv_kernel,
        grid=(items + 1,),
        in_specs=[pl.BlockSpec((1, seq, k), lambda s: (mm(s) // nt, 0, 0)),
                  pl.BlockSpec((k, tn), lambda s: (0, mm(s) % nt)),
                  pl.BlockSpec((1, tn), lambda s: (0, mm(s) % nt)),
                  pl.BlockSpec((SSD_CONV, tn), lambda s: (0, cv(s) % nt)),
                  pl.BlockSpec((1, tn), lambda s: (0, cv(s) % nt))],
        out_specs=pl.BlockSpec((1, seq, tn), lambda s: (cv(s) // nt, 0, cv(s) % nt)),
        out_shape=jax.ShapeDtypeStruct((bsz, seq, n), out_dtype),
        scratch_shapes=[pltpu.VMEM((seq + 2 * CONV_BORDER, tn), F32)] * 2,
        compiler_params=_params("arbitrary"),
        name="proj_conv_rows",
    )(x, w, bias, conv_w, conv_b)


def _hy_mlp_kernel(w1t_ref, b1_ref, f1_ref, w2t_ref, b2_ref, f2_ref, w3t_ref, b3_ref, f3_ref,
                   o_ref, *, seq, emb_rows):
    shape = (emb_rows, 2 * seq)
    lag = lax.broadcasted_iota(jnp.int32, shape, 1) - seq
    pos = jnp.abs(lag).astype(F32)
    row = lax.broadcasted_iota(jnp.int32, shape, 0)
    t = pos / (seq - 1)
    w = (2.0 * math.pi) * pos / seq
    band = jnp.where(row <= HY_BANDS, row - 1, row - 1 - HY_BANDS).astype(F32)
    freq = 1e-4 + band * ((HY_BANDS - 1 - 1e-4) / (HY_BANDS - 1))
    ang = freq * w
    z = jnp.where(row == 0, t,
                  jnp.where(row <= HY_BANDS, jnp.cos(ang),
                            jnp.where(row <= 2 * HY_BANDS, -jnp.sin(ang), 0.0)))
    h = jnp.sin(f1_ref[...] * (jnp.dot(w1t_ref[...], z, precision=HI, preferred_element_type=F32)
                               + b1_ref[...]))
    h = jnp.sin(f2_ref[...] * (jnp.dot(w2t_ref[...], h, precision=HI, preferred_element_type=F32)
                               + b2_ref[...]))
    h = jnp.sin(f3_ref[...] * (jnp.dot(w3t_ref[...], h, precision=HI, preferred_element_type=F32)
                               + b3_ref[...]))
    o_ref[...] = h


def _dft_matrices(p):
    e = lax.broadcasted_iota(jnp.int32, (p, 2 * p), 0)
    j = lax.broadcasted_iota(jnp.int32, (p, 2 * p), 1)
    k = jnp.where(j < p, j, j - p)
    theta = ((e * k) % (2 * p)).astype(F32) * (math.pi / p)
    sign_e = jnp.where(e % 2 == 0, 1.0, -1.0)
    fwd = jnp.where(j < p, jnp.cos(theta), jnp.where(j == p, sign_e, -jnp.sin(theta)))
    jj = lax.broadcasted_iota(jnp.int32, (2 * p, p), 0)
    tt = lax.broadcasted_iota(jnp.int32, (2 * p, p), 1)
    kk = jnp.where(jj < p, jj, jj - p)
    th = ((tt * kk) % (2 * p)).astype(F32) * (math.pi / p)
    sign_t = jnp.where(tt % 2 == 0, 1.0, -1.0)
    n = 2.0 * p
    inv = jnp.where(jj == 0, 1.0 / n,
                    jnp.where(jj < p, (2.0 / n) * jnp.cos(th),
                              jnp.where(jj == p, sign_t / n, (-2.0 / n) * jnp.sin(th))))
    return fwd.astype(F32), inv.astype(F32)


def _hy_spec_kernel(h_ref, w4f_ref, w4b_ref, fwd_ref, o_ref, *, seq, p, d_hy, rows):
    nb = seq // p
    h = h_ref[...]
    kpos = jnp.dot(w4f_ref[...], h[:, seq:], precision=HI, preferred_element_type=F32)
    kneg = jnp.dot(w4b_ref[...], h[:, :seq], precision=HI, preferred_element_type=F32)
    kb0 = jnp.dot(w4b_ref[...], h[:, seq:seq + 128], precision=HI, preferred_element_type=F32)
    lane128 = lax.broadcasted_iota(jnp.int32, kb0.shape, 1)
    kb0 = jnp.sum(jnp.where(lane128 == 0, kb0, 0.0), axis=1, keepdims=True)

    r0 = pl.program_id(0) * rows
    ch = (r0 + lax.broadcasted_iota(jnp.int32, (rows, 1), 0)) % d_hy
    max_decay = math.log(HY_DECAY_TARGET) / HY_FAST_DECAY_PCT
    min_decay = math.log(HY_DECAY_TARGET) / HY_SLOW_DECAY_PCT
    delta = jnp.abs(min_decay + ch.astype(F32) * ((max_decay - min_decay) / (d_hy - 1)))
    i = lax.broadcasted_iota(jnp.int32, (1, seq), 1)
    t_pos = i.astype(F32) / (seq - 1)
    t_neg = (seq - i).astype(F32) / (seq - 1)
    kpos = (kpos + jnp.where(i == 0, kb0, 0.0)) * jnp.exp(-t_pos * delta)
    kneg = jnp.where(i == 0, 0.0, kneg * jnp.exp(-t_neg * delta))
    klag = jnp.concatenate([kneg, kpos], axis=1)

    fwd = fwd_ref[...]
    lane = lax.broadcasted_iota(jnp.int32, (1, 2 * p), 1)
    sign = jnp.where(lane % 2 == 0, 1.0, -1.0)
    spec = []
    first = []
    for m in range(2 * nb):
        blk = klag[:, m * p:(m + 1) * p]
        spec.append(jnp.dot(blk, fwd, precision=HI, preferred_element_type=F32))
        first.append(blk[:, 0:1])
    half = lax.broadcasted_iota(jnp.int32, (1, p), 1)
    for m in range(1, 2 * nb):
        tail = spec[m - 1] - jnp.where(lane <= p, first[m - 1], 0.0)
        full = spec[m] + sign * tail
        re, im = full[:, :p], full[:, p:]
        o_ref[m - 1, 0] = re.astype(o_ref.dtype)
        o_ref[m - 1, 1] = jnp.where(half == 0, 0.0, im).astype(o_ref.dtype)
        o_ref[m - 1, 2] = jnp.where(half == 0, im, re).astype(o_ref.dtype)


def _hyena_spectra(seq, p, w1, b1, fr1, w2, b2, fr2, w3, b3, fr3, w4, d_hy, rows, spec_dtype):
    hidden = w1.shape[1]
    emb_rows = -(-HY_EMB_DIM // 8) * 8
    w1t = jnp.zeros((hidden, emb_rows), F32).at[:, :HY_EMB_DIM].set(w1.T.astype(F32))
    col = lambda v: v.astype(F32).reshape(hidden, 1)
    h = pl.pallas_call(
        functools.partial(_hy_mlp_kernel, seq=seq, emb_rows=emb_rows),
        out_shape=jax.ShapeDtypeStruct((hidden, 2 * seq), F32),
        name="hyena_filter_mlp",
    )(w1t, col(b1), col(fr1), w2.T.astype(F32), col(b2), col(fr2), w3.T.astype(F32), col(b3), col(fr3))
    w4r = w4.astype(F32).reshape(hidden, HY_ORDER, 2, d_hy)
    w4f = w4r[:, :, 0].reshape(hidden, HY_ORDER * d_hy).T
    w4b = w4r[:, :, 1].reshape(hidden, HY_ORDER * d_hy).T
    fwd, _ = _dft_matrices(p)
    nb = seq // p
    n_rows = HY_ORDER * d_hy
    return pl.pallas_call(
        functools.partial(_hy_spec_kernel, seq=seq, p=p, d_hy=d_hy, rows=rows),
        grid=(n_rows // rows,),
        in_specs=[pl.BlockSpec((hidden, 2 * seq), lambda r: (0, 0)),
                  pl.BlockSpec((rows, hidden), lambda r: (r, 0)),
                  pl.BlockSpec((rows, hidden), lambda r: (r, 0)),
                  pl.BlockSpec((p, 2 * p), lambda r: (0, 0))],
        out_specs=pl.BlockSpec((2 * nb - 1, 3, rows, p), lambda r: (0, 0, r, 0)),
        out_shape=jax.ShapeDtypeStruct((2 * nb - 1, 3, n_rows, p), spec_dtype),
        compiler_params=_params("parallel"),
        name="hyena_filter_spectra",
    )(h, w4f, w4b, fwd)


def _hy_conv_kernel(g1_ref, g2_ref, v_ref, w1_ref, w2_ref, wv_ref, b1_ref, b2_ref, bv_ref,
                    k0_ref, k1_ref, s0_ref, s1_ref, fwd_ref, inv_ref, o_ref, *, p):
    bt, r, seq = v_ref.shape
    nb = seq // p
    rows = bt * r
    lane = lax.broadcasted_iota(jnp.int32, (1, 1, seq), 2)

    def short_conv(x_ref, w_ref, b_ref):
        x = x_ref[...].astype(F32)
        x2 = x.reshape(rows, seq)
        prev = jnp.where(lane == 0, 0.0, pltpu.roll(x2, 1, axis=1).reshape(bt, r, seq))
        nxt = jnp.where(lane == seq - 1, 0.0, pltpu.roll(x2, seq - 1, axis=1).reshape(bt, r, seq))
        w = w_ref[...]
        return b_ref[...] + w[:, 0:1] * prev + w[:, 1:2] * x + w[:, 2:3] * nxt

    fwd = fwd_ref[...]
    inv = inv_ref[...]

    def long_conv(u, k_ref):
        ub = u.astype(BF16).reshape(rows, seq)
        spec = [jnp.dot(ub[:, j * p:(j + 1) * p], fwd, preferred_element_type=F32).reshape(bt, r, 2 * p)
                for j in range(nb)]
        outs = []
        for i in range(nb):
            acc_re = jnp.zeros((bt, r, p), F32)
            acc_im = jnp.zeros((bt, r, p), F32)
            for j in range(nb):
                m = i - j + nb - 1
                ka = k_ref[m, 0].astype(F32)
                kb = k_ref[m, 1].astype(F32)
                kc = k_ref[m, 2].astype(F32)
                u_re = spec[j][:, :, :p]
                u_im = spec[j][:, :, p:]
                acc_re = acc_re + ka * u_re - kb * u_im
                acc_im = acc_im + kc * u_im + kb * u_re
            y_spec = jnp.concatenate([acc_re, acc_im], axis=2).astype(BF16).reshape(rows, 2 * p)
            outs.append(jnp.dot(y_spec, inv, preferred_element_type=F32).reshape(bt, r, p))
        return jnp.concatenate(outs, axis=2)

    v = short_conv(v_ref, wv_ref, bv_ref)
    z = short_conv(g1_ref, w1_ref, b1_ref) * (long_conv(v, k0_ref) + v * s0_ref[...])
    y = short_conv(g2_ref, w2_ref, b2_ref) * (long_conv(z, k1_ref) + z * s1_ref[...])
    for b in range(bt):
        o_ref[b] = y[b].T.astype(o_ref.dtype)


def _hyena_mixer(p_t, conv_w_t, conv_b_col, spectra, skip_col, d_hy, p, r, bt, out_dtype):
    bsz, _, seq = p_t.shape
    nb = seq // p
    nblk = d_hy // r
    fwd, inv = _dft_matrices(p)
    x_spec = lambda off: pl.BlockSpec((bt, r, seq), lambda j, b: (b, off * nblk + j, 0))
    w_spec = lambda off: pl.BlockSpec((r, conv_w_t.shape[1]), lambda j, b: (off * nblk + j, 0))
    c_spec = lambda off: pl.BlockSpec((r, 1), lambda j, b: (off * nblk + j, 0))
    k_spec = lambda off: pl.BlockSpec((2 * nb - 1, 3, r, p), lambda j, b: (0, 0, off * nblk + j, 0))
    return pl.pallas_call(
        functools.partial(_hy_conv_kernel, p=p),
        grid=(nblk, bsz // bt),
        in_specs=[x_spec(0), x_spec(1), x_spec(2), w_spec(0), w_spec(1), w_spec(2),
                  c_spec(0), c_spec(1), c_spec(2), k_spec(0), k_spec(1), c_spec(0), c_spec(1),
                  pl.BlockSpec((p, 2 * p), lambda j, b: (0, 0)),
                  pl.BlockSpec((2 * p, p), lambda j, b: (0, 0))],
        out_specs=pl.BlockSpec((bt, seq, r), lambda j, b: (b, 0, j)),
        out_shape=jax.ShapeDtypeStruct((bsz, seq, d_hy), out_dtype),
        compiler_params=_params("parallel", "arbitrary"),
        name="hyena_mixer",
    )(p_t, p_t, p_t, conv_w_t, conv_w_t, conv_w_t, conv_b_col, conv_b_col, conv_b_col,
      spectra, spectra, skip_col, skip_col, fwd.astype(BF16), inv.astype(BF16))


def _ssd_prep_kernel(dt_ref, dtb_ref, a_ref, cum_ref, dtt_ref, *, r):
    seq = dt_ref.shape[1]
    q = SSD_CHUNK
    raw = dt_ref[0] + dtb_ref[...]
    dt = jnp.maximum(raw, 0.0) + jnp.log1p(jnp.exp(-jnp.abs(raw)))
    a = dt * a_ref[...]
    li = lax.broadcasted_iota(jnp.int32, (q, q), 0)
    si = lax.broadcasted_iota(jnp.int32, (q, q), 1)
    lower = jnp.where(li >= si, 1.0, 0.0).astype(F32)
    upper = jnp.where(li <= si, 1.0, 0.0).astype(F32)
    fwd_col = lax.broadcasted_iota(jnp.int32, (1, 128), 1) % (2 * r) < r
    for c in range(seq // q):
        ac = a[c * q:(c + 1) * q]
        cum = jnp.where(fwd_col,
                        jnp.dot(lower, ac, precision=HI, preferred_element_type=F32),
                        jnp.dot(upper, ac, precision=HI, preferred_element_type=F32))
        cum_ref[0, :, c * q:(c + 1) * q] = cum.T
        dtt_ref[0, :, c * q:(c + 1) * q] = dt[c * q:(c + 1) * q].T


def _ssd_prep(p_dt, dt_bias, a_neg, r):
    bsz, seq, _ = p_dt.shape
    vec = pl.BlockSpec((1, 128), lambda b: (0, 0))
    out = pl.BlockSpec((1, 128, seq), lambda b: (b, 0, 0))
    return pl.pallas_call(
        functools.partial(_ssd_prep_kernel, r=r),
        grid=(bsz,),
        in_specs=[pl.BlockSpec((1, seq, 128), lambda b: (b, 0, 0)), vec, vec],
        out_specs=[out, out],
        out_shape=[jax.ShapeDtypeStruct((bsz, 128, seq), F32)] * 2,
        compiler_params=_params("parallel"),
        name="ssd_prep",
    )(p_dt, dt_bias, a_neg)


def _ssd_kernel(z_ref, x_ref, bt_ref, c_ref, cum_ref, dtt_ref, wx_ref, wbt_ref, wc_ref,
                bx_ref, bbt_ref, bc_ref, dskip_ref, nw_ref, o_ref, xm_s, bt_s, cs_s, y_s, st_s):
    nbat, seq, gw = x_ref.shape
    n = bt_ref.shape[1]
    r = gw // SSD_HEADDIM
    q = SSD_CHUNK
    nc = seq // q
    half = SSD_CONV // 2
    shifts = [k - half for k in range(SSD_CONV) if k != half]
    halo_r, halo_l = 16, 128

    k_r = 2 * q
    sr = lax.broadcasted_iota(jnp.int32, (len(shifts) * q, k_r), 0)
    se = lax.broadcasted_iota(jnp.int32, (len(shifts) * q, k_r), 1)
    band_r = jnp.zeros(sr.shape, jnp.bool_)
    for j, sh in enumerate(shifts):
        band_r = jnp.logical_or(band_r, jnp.logical_and(sr // q == j, se == halo_r + sr % q + sh))
    band_r = jnp.where(band_r, 1.0, 0.0).astype(BF16)
    k_l = q + 2 * halo_l
    le = lax.broadcasted_iota(jnp.int32, (k_l, len(shifts) * q), 0)
    lc = lax.broadcasted_iota(jnp.int32, (k_l, len(shifts) * q), 1)
    band_l = jnp.zeros(le.shape, jnp.bool_)
    for j, sh in enumerate(shifts):
        band_l = jnp.logical_or(band_l, jnp.logical_and(lc // q == j, le == halo_l + lc % q + sh))
    band_l = jnp.where(band_l, 1.0, 0.0).astype(BF16)

    col_head = lax.broadcasted_iota(jnp.int32, (1, gw), 1) // SSD_HEADDIM
    wx, wc, wbt = wx_ref[...], wc_ref[...], wbt_ref[...]

    def conv_rows(ref, bb, c, off, w, bias):
        width = ref.shape[2]
        cur = ref[bb, pl.ds(off, q), :]
        prev = ref[bb, pl.ds(pl.multiple_of(jnp.maximum(off - halo_r, 0), halo_r), halo_r), :]
        nxt = ref[bb, pl.ds(pl.multiple_of(jnp.minimum(off + q, seq - halo_r), halo_r), halo_r), :]
        prev = jnp.where(c > 0, prev, jnp.zeros_like(prev))
        nxt = jnp.where(c < nc - 1, nxt, jnp.zeros_like(nxt))
        ext = jnp.concatenate([prev, cur, nxt, jnp.zeros((k_r - q - 2 * halo_r, width), BF16)], axis=0)
        moved = jnp.dot(band_r, ext, preferred_element_type=F32)
        acc = bias + w[half:half + 1] * cur.astype(F32)
        for j, sh in enumerate(shifts):
            acc = acc + w[half + sh:half + sh + 1] * moved[j * q:(j + 1) * q]
        return _silu(acc)

    def conv_one(bb, c, off):
        xs = conv_rows(x_ref, bb, c, off, wx, bx_ref[...])
        y_s[bb, 0, pl.ds(off, q), :] = xs * dskip_ref[...]
        for h in range(r):
            xm_s[bb, h, pl.ds(off, q), :] = jnp.where(col_head == h, xs, 0.0).astype(BF16)
        cs_s[bb, pl.ds(off, q), :] = conv_rows(c_ref, bb, c, off, wc, bc_ref[...]).astype(BF16)
        cur = bt_ref[bb, :, pl.ds(off, q)]
        prev = bt_ref[bb, :, pl.ds(pl.multiple_of(jnp.maximum(off - halo_l, 0), halo_l), halo_l)]
        nxt = bt_ref[bb, :, pl.ds(pl.multiple_of(jnp.minimum(off + q, seq - halo_l), halo_l), halo_l)]
        prev = jnp.where(c > 0, prev, jnp.zeros_like(prev))
        nxt = jnp.where(c < nc - 1, nxt, jnp.zeros_like(nxt))
        moved = jnp.dot(jnp.concatenate([prev, cur, nxt], axis=1), band_l, preferred_element_type=F32)
        acc = bbt_ref[...] + wbt[:, half:half + 1] * cur.astype(F32)
        for j, sh in enumerate(shifts):
            acc = acc + wbt[:, half + sh:half + sh + 1] * moved[:, j * q:(j + 1) * q]
        bt_s[bb, :, pl.ds(off, q)] = _silu(acc).astype(BF16)

    def conv_step(c, carry):
        off = pl.multiple_of(c * q, q)
        for bb in range(nbat):
            conv_one(bb, c, off)
        return carry

    lax.fori_loop(0, nc, conv_step, 0)
    st_s[...] = jnp.zeros_like(st_s)

    li = lax.broadcasted_iota(jnp.int32, (q, q), 0)
    si = lax.broadcasted_iota(jnp.int32, (q, q), 1)
    diag = li == si
    sub = lax.broadcasted_iota(jnp.int32, (2 * r, 1), 0)
    neg_inf = jnp.float32(-jnp.inf)
    nh = 2 * r
    assert 3 * nh + 16 <= q
    ones_rows = jnp.ones((8, q), F32)
    pad_rows = jnp.zeros((q - 3 * nh - 16, q), F32)
    pk = lax.broadcasted_iota(jnp.int32, (3 * nh + 8, nh * q), 0)
    pj = lax.broadcasted_iota(jnp.int32, (3 * nh + 8, nh * q), 1) // q
    seg_top = jnp.where(jnp.logical_and(pk < 3 * nh, pk % nh == pj), 1.0, 0.0).astype(BF16)
    seg_bottom = jnp.zeros((q - 3 * nh - 8 - 16, nh * q), BF16)

    def split3(v):
        hi = v.astype(BF16).astype(F32)
        mid = (v - hi).astype(BF16).astype(F32)
        lo = (v - hi - mid).astype(BF16).astype(F32)
        return hi, mid, lo

    def flat(v):
        return jnp.concatenate([v[j:j + 1, :] for j in range(nh)], axis=1)

    def chunk_step(i, carry):
        offs = (pl.multiple_of(i * q, q), pl.multiple_of((nc - 1 - i) * q, q))
        chains = [(bb, d) for bb in range(nbat) for d in range(2)]
        pre = []
        for bb in range(nbat):
            cum = jnp.where(sub < r, cum_ref[bb, :, pl.ds(offs[0], q)], cum_ref[bb, :, pl.ds(offs[1], q)])
            dt = jnp.where(sub < r, dtt_ref[bb, :, pl.ds(offs[0], q)], dtt_ref[bb, :, pl.ds(offs[1], q)])
            total = jnp.where(sub < r, cum[:, q - 1:q], cum[:, 0:1])
            hi, mid, lo = split3(cum)
            left = jnp.concatenate([hi, mid, lo, jnp.zeros((8, q), F32), ones_rows, pad_rows], axis=0)
            left = left.T.astype(BF16)
            minus = jnp.concatenate([-flat(hi), -flat(mid), -flat(lo), jnp.zeros((13, nh * q), F32)],
                                    axis=0)
            right = jnp.concatenate([seg_top, minus.astype(BF16), seg_bottom], axis=0)
            pre.append(dict(dt=dt, carry_decay=jnp.exp(total), in_decay=jnp.exp(cum),
                            out_w=jnp.exp(total - cum) * dt, left=left, right=right))
        seg_all = [jnp.dot(p_["left"], p_["right"], preferred_element_type=F32) for p_ in pre]
        btc = {c: bt_s[c[0], :, pl.ds(offs[c[1]], q)] for c in chains}
        cc = {c: cs_s[c[0], pl.ds(offs[c[1]], q), :] for c in chains}
        state = {c: st_s[c[0], c[1]] for c in chains}
        cb = {c: jnp.dot(cc[c], btc[c], preferred_element_type=F32) for c in chains}
        c_state = {c: jnp.dot(cc[c], state[c].astype(BF16), preferred_element_type=F32) for c in chains}
        lhs_y, rhs_y, lhs_s, rhs_s, decays = {}, {}, {}, {}, {}
        for c in chains:
            bb, d = c
            p_ = pre[bb]
            tri = (li >= si) if d == 0 else (li <= si)
            btf = btc[c].astype(F32)
            mats, diags, xs_h, cs_h, b_scaled = [], [], [], [], []
            state_decay = jnp.zeros((1, gw), F32)
            for h in range(r):
                ln = d * r + h
                seg = seg_all[bb][:, ln * q:(ln + 1) * q]
                mats.append((cb[c] * jnp.exp(jnp.where(tri, seg, neg_inf)) * p_["dt"][ln:ln + 1, :])
                            .astype(BF16))
                diags.append(jnp.where(diag, p_["in_decay"][ln:ln + 1, :], 0.0).astype(BF16))
                xs_h.append(xm_s[bb, h, pl.ds(offs[d], q), :])
                cs_h.append(jnp.where(col_head == h, c_state[c], 0.0).astype(BF16))
                b_scaled.append((btf * p_["out_w"][ln:ln + 1, :]).astype(BF16))
                state_decay = jnp.where(col_head == h, p_["carry_decay"][ln:ln + 1, :], state_decay)
            lhs_y[c] = jnp.concatenate(mats + diags, axis=1)
            rhs_y[c] = jnp.concatenate(xs_h + cs_h, axis=0)
            lhs_s[c] = jnp.concatenate(b_scaled, axis=1)
            rhs_s[c] = jnp.concatenate(xs_h, axis=0)
            decays[c] = state_decay
        y_new = {c: jnp.dot(lhs_y[c], rhs_y[c], preferred_element_type=F32) for c in chains}
        upd = {c: jnp.dot(lhs_s[c], rhs_s[c], preferred_element_type=F32) for c in chains}
        for c in chains:
            bb, d = c
            y_s[bb, 1 + d, pl.ds(offs[d], q), :] = y_new[c]
            st_s[bb, d] = state[c] * decays[c] + upd[c]
        return carry

    lax.fori_loop(0, nc, chunk_step, 0, unroll=2)

    def gate_step(c, carry):
        off = pl.multiple_of(c * q, q)
        for bb in range(nbat):
            zf = z_ref[bb, pl.ds(off, q), :].astype(F32)
            y = (y_s[bb, 0, pl.ds(off, q), :] + y_s[bb, 1, pl.ds(off, q), :]
                 + y_s[bb, 2, pl.ds(off, q), :])
            y = y * _silu(zf)
            y = y * lax.rsqrt(jnp.mean(jnp.square(y), axis=-1, keepdims=True) + RMS_EPS)
            o_ref[bb, pl.ds(off, q), :] = (y * nw_ref[...]).astype(o_ref.dtype)
        return carry

    lax.fori_loop(0, nc, gate_step, 0)


def _ssd_mixer(pm, p_t, bt_off, cum_t, dt_t, conv_w, conv_b, dskip_cols, norm_w, d_ssm, out_dtype):
    bsz, seq, _ = pm.shape
    gw = d_ssm // SSD_GROUPS
    r = gw // SSD_HEADDIM
    n = SSD_STATE
    xo, co = d_ssm // gw, 2 * d_ssm // n
    cw = lambda width, off: pl.BlockSpec((SSD_CONV, width), lambda b, g: (0, off + g))
    cb = lambda width, off: pl.BlockSpec((1, width), lambda b, g: (0, off + g))
    nbat = _pick(bsz, 2, 1)
    heads_rows = pl.BlockSpec((nbat, 2 * r, seq), lambda b, g: (b, g, 0))
    return pl.pallas_call(
        _ssd_kernel,
        grid=(bsz // nbat, SSD_GROUPS),
        in_specs=[pl.BlockSpec((nbat, seq, gw), lambda b, g: (b, 0, g)),
                  pl.BlockSpec((nbat, seq, gw), lambda b, g: (b, 0, xo + g)),
                  pl.BlockSpec((nbat, n, seq), lambda b, g: (b, bt_off + g, 0)),
                  pl.BlockSpec((nbat, seq, n), lambda b, g: (b, 0, co + g)),
                  heads_rows, heads_rows,
                  cw(gw, 0),
                  pl.BlockSpec((n, SSD_CONV), lambda b, g: (d_ssm // n + g, 0)),
                  cw(n, d_ssm // n + SSD_GROUPS),
                  cb(gw, 0),
                  pl.BlockSpec((n, 1), lambda b, g: (d_ssm // n + g, 0)),
                  cb(n, d_ssm // n + SSD_GROUPS),
                  cb(gw, 0), cb(gw, 0)],
        out_specs=pl.BlockSpec((nbat, seq, gw), lambda b, g: (b, 0, g)),
        out_shape=jax.ShapeDtypeStruct((bsz, seq, d_ssm), out_dtype),
        scratch_shapes=[pltpu.VMEM((nbat, r, seq, gw), BF16),
                        pltpu.VMEM((nbat, n, seq), BF16), pltpu.VMEM((nbat, seq, n), BF16),
                        pltpu.VMEM((nbat, 3, seq, gw), F32), pltpu.VMEM((nbat, 2, n, gw), F32)],
        compiler_params=_params("parallel", "parallel"),
        name="ssd_mixer",
    )(pm, pm, p_t, pm, cum_t, dt_t, conv_w, conv_w.T, conv_w, conv_b, conv_b.reshape(-1, 1), conv_b,
      dskip_cols, norm_w)


def _ssd_pipe_kernel(z_ref, x_ref, bt_ref, c_ref, cum_ref, dtt_ref, wbt_ref, bbt_ref, dskip_ref, nw_ref,
                     o_ref, xm_a, xm_b, bt_a, bt_b, y0_a, y0_b, yd_a, yd_b, st_s):
    seq, gw = x_ref.shape[1], x_ref.shape[2]
    r = gw // SSD_HEADDIM
    q = SSD_CHUNK
    nc = seq // q
    step = pl.program_id(0)
    half = SSD_CONV // 2
    shifts = [k - half for k in range(SSD_CONV) if k != half]
    halo = 128

    k_l = q + 2 * halo
    le = lax.broadcasted_iota(jnp.int32, (k_l, len(shifts) * q), 0)
    lc = lax.broadcasted_iota(jnp.int32, (k_l, len(shifts) * q), 1)
    band = jnp.zeros(le.shape, jnp.bool_)
    for j, sh in enumerate(shifts):
        band = jnp.logical_or(band, jnp.logical_and(lc // q == j, le == halo + lc % q + sh))
    band = jnp.where(band, 1.0, 0.0).astype(BF16)
    wbt = wbt_ref[...]

    col_head = lax.broadcasted_iota(jnp.int32, (1, gw), 1) // SSD_HEADDIM
    li = lax.broadcasted_iota(jnp.int32, (q, q), 0)
    si = lax.broadcasted_iota(jnp.int32, (q, q), 1)
    diag = li == si
    sub = lax.broadcasted_iota(jnp.int32, (2 * r, 1), 0)
    neg_inf = jnp.float32(-jnp.inf)
    nh = 2 * r
    assert 3 * nh + 16 <= q
    ones_rows = jnp.ones((8, q), F32)
    pad_rows = jnp.zeros((q - 3 * nh - 16, q), F32)
    pk = lax.broadcasted_iota(jnp.int32, (3 * nh + 8, nh * q), 0)
    pj = lax.broadcasted_iota(jnp.int32, (3 * nh + 8, nh * q), 1) // q
    seg_top = jnp.where(jnp.logical_and(pk < 3 * nh, pk % nh == pj), 1.0, 0.0).astype(BF16)
    seg_bottom = jnp.zeros((q - 3 * nh - 8 - 16, nh * q), BF16)

    def split3(v):
        hi = v.astype(BF16).astype(F32)
        mid = (v - hi).astype(BF16).astype(F32)
        lo = (v - hi - mid).astype(BF16).astype(F32)
        return hi, mid, lo

    def flat(v):
        return jnp.concatenate([v[j:j + 1, :] for j in range(nh)], axis=1)

    def prep_chunk(c, off, xm_s, bt_s, y0_s):
        xs = x_ref[0, pl.ds(off, q), :]
        y0_s[pl.ds(off, q), :] = xs.astype(F32) * dskip_ref[...]
        for h in range(r):
            xm_s[h, pl.ds(off, q), :] = jnp.where(col_head == h, xs, jnp.zeros_like(xs))
        cur = bt_ref[0, :, pl.ds(off, q)]
        prev = bt_ref[0, :, pl.ds(pl.multiple_of(jnp.maximum(off - halo, 0), halo), halo)]
        nxt = bt_ref[0, :, pl.ds(pl.multiple_of(jnp.minimum(off + q, seq - halo), halo), halo)]
        prev = jnp.where(c > 0, prev, jnp.zeros_like(prev))
        nxt = jnp.where(c < nc - 1, nxt, jnp.zeros_like(nxt))
        moved = jnp.dot(jnp.concatenate([prev, cur, nxt], axis=1), band, preferred_element_type=F32)
        acc = bbt_ref[...] + wbt[:, half:half + 1] * cur.astype(F32)
        for j, sh in enumerate(shifts):
            acc = acc + wbt[:, half + sh:half + sh + 1] * moved[:, j * q:(j + 1) * q]
        bt_s[:, pl.ds(off, q)] = _silu(acc).astype(BF16)

    def gate_chunk(off, y0_s, yd_s):
        zf = z_ref[0, pl.ds(off, q), :].astype(F32)
        y = y0_s[pl.ds(off, q), :] + yd_s[0, pl.ds(off, q), :] + yd_s[1, pl.ds(off, q), :]
        y = y * _silu(zf)
        y = y * lax.rsqrt(jnp.mean(jnp.square(y), axis=-1, keepdims=True) + RMS_EPS)
        o_ref[0, pl.ds(off, q), :] = (y * nw_ref[...]).astype(o_ref.dtype)

    def scan_chunk(offs, xm_s, bt_s, yd_s, extra):
        cum = jnp.where(sub < r, cum_ref[0, :, pl.ds(offs[0], q)], cum_ref[0, :, pl.ds(offs[1], q)])
        dt = jnp.where(sub < r, dtt_ref[0, :, pl.ds(offs[0], q)], dtt_ref[0, :, pl.ds(offs[1], q)])
        total = jnp.where(sub < r, cum[:, q - 1:q], cum[:, 0:1])
        carry_decay = jnp.exp(total)
        in_decay = jnp.exp(cum)
        out_w = jnp.exp(total - cum) * dt
        hi, mid, lo = split3(cum)
        left = jnp.concatenate([hi, mid, lo, jnp.zeros((8, q), F32), ones_rows, pad_rows], axis=0)
        left = left.T.astype(BF16)
        minus = jnp.concatenate([-flat(hi), -flat(mid), -flat(lo), jnp.zeros((13, nh * q), F32)], axis=0)
        right = jnp.concatenate([seg_top, minus.astype(BF16), seg_bottom], axis=0)
        seg_all = jnp.dot(left, right, preferred_element_type=F32)
        btc = [bt_s[:, pl.ds(offs[d], q)] for d in range(2)]
        cc = [c_ref[0, pl.ds(offs[d], q), :] for d in range(2)]
        state = [st_s[d] for d in range(2)]
        cb = [jnp.dot(cc[d], btc[d], preferred_element_type=F32) for d in range(2)]
        c_state = [jnp.dot(cc[d], state[d].astype(BF16), preferred_element_type=F32) for d in range(2)]
        extra()
        lhs_y, rhs_y, lhs_s, rhs_s, decays = [], [], [], [], []
        for d in range(2):
            tri = (li >= si) if d == 0 else (li <= si)
            btf = btc[d].astype(F32)
            mats, diags, xs_h, cs_h, b_scaled = [], [], [], [], []
            state_decay = jnp.zeros((1, gw), F32)
            for h in range(r):
                ln = d * r + h
                seg = seg_all[:, ln * q:(ln + 1) * q]
                mats.append((cb[d] * jnp.exp(jnp.where(tri, seg, neg_inf)) * dt[ln:ln + 1, :]).astype(BF16))
                diags.append(jnp.where(diag, in_decay[ln:ln + 1, :], 0.0).astype(BF16))
                xs_h.append(xm_s[h, pl.ds(offs[d], q), :])
                cs_h.append(jnp.where(col_head == h, c_state[d], 0.0).astype(BF16))
                b_scaled.append((btf * out_w[ln:ln + 1, :]).astype(BF16))
                state_decay = jnp.where(col_head == h, carry_decay[ln:ln + 1, :], state_decay)
            lhs_y.append(jnp.concatenate(mats + diags, axis=1))
            rhs_y.append(jnp.concatenate(xs_h + cs_h, axis=0))
            lhs_s.append(jnp.concatenate(b_scaled, axis=1))
            rhs_s.append(jnp.concatenate(xs_h, axis=0))
            decays.append(state_decay)
        y_new = [jnp.dot(lhs_y[d], rhs_y[d], preferred_element_type=F32) for d in range(2)]
        upd = [jnp.dot(lhs_s[d], rhs_s[d], preferred_element_type=F32) for d in range(2)]
        for d in range(2):
            yd_s[d, pl.ds(offs[d], q), :] = y_new[d]
            st_s[d] = state[d] * decays[d] + upd[d]

    @pl.when(step == 0)
    def _():
        for ref in (xm_a, xm_b, bt_a, bt_b, y0_a, y0_b, yd_a, yd_b):
            ref[...] = jnp.zeros_like(ref)

    st_s[...] = jnp.zeros_like(st_s)

    def stages(new_set, scan_set):
        xm_c, bt_c, y0_c, yd_c = new_set
        xm_p, bt_p, _, yd_p = scan_set

        def body(i, carry):
            off = pl.multiple_of(i * q, q)
            offs = (off, pl.multiple_of((nc - 1 - i) * q, q))

            def extra():
                gate_chunk(off, y0_c, yd_c)
                prep_chunk(i, off, xm_c, bt_c, y0_c)

            scan_chunk(offs, xm_p, bt_p, yd_p, extra)
            return carry

        lax.fori_loop(0, nc, body, 0, unroll=2)

    set_a = (xm_a, bt_a, y0_a, yd_a)
    set_b = (xm_b, bt_b, y0_b, yd_b)
    pl.when(step % 2 == 0)(lambda: stages(set_a, set_b))
    pl.when(step % 2 == 1)(lambda: stages(set_b, set_a))


def _ssd_mixer_pipelined(pz, z_off, pxc, p_t, bt_off, cum_t, dt_t, wbt, bbt_col, dskip_cols, norm_w,
                         d_ssm, out_dtype):
    bsz, seq, _ = pxc.shape
    gw = d_ssm // SSD_GROUPS
    r = gw // SSD_HEADDIM
    n = SSD_STATE
    g_n = SSD_GROUPS
    n_prob = bsz * g_n
    co = d_ssm // n

    def prob(lag):
        def bg(s):
            p = jnp.clip(s - lag, 0, n_prob - 1)
            return p // g_n, p % g_n
        return bg

    new_p, scan_p, gate_p = prob(0), prob(1), prob(2)
    at = lambda bg, f: (lambda s: f(*bg(s)))
    heads_rows = pl.BlockSpec((1, 2 * r, seq), at(scan_p, lambda b, g: (b, g, 0)))
    return pl.pallas_call(
        _ssd_pipe_kernel,
        grid=(n_prob + 2,),
        in_specs=[pl.BlockSpec((1, seq, gw), at(gate_p, lambda b, g: (b, 0, z_off + g))),
                  pl.BlockSpec((1, seq, gw), at(new_p, lambda b, g: (b, 0, g))),
                  pl.BlockSpec((1, n, seq), at(new_p, lambda b, g: (b, bt_off + g, 0))),
                  pl.BlockSpec((1, seq, n), at(scan_p, lambda b, g: (b, 0, co + g))),
                  heads_rows, heads_rows,
                  pl.BlockSpec((n, SSD_CONV), at(new_p, lambda b, g: (g, 0))),
                  pl.BlockSpec((n, 1), at(new_p, lambda b, g: (g, 0))),
                  pl.BlockSpec((1, gw), at(new_p, lambda b, g: (0, g))),
                  pl.BlockSpec((1, gw), at(gate_p, lambda b, g: (0, g)))],
        out_specs=pl.BlockSpec((1, seq, gw), at(gate_p, lambda b, g: (b, 0, g))),
        out_shape=jax.ShapeDtypeStruct((bsz, seq, d_ssm), out_dtype),
        scratch_shapes=[pltpu.VMEM((r, seq, gw), BF16), pltpu.VMEM((r, seq, gw), BF16),
                        pltpu.VMEM((n, seq), BF16), pltpu.VMEM((n, seq), BF16),
                        pltpu.VMEM((seq, gw), F32), pltpu.VMEM((seq, gw), F32),
                        pltpu.VMEM((2, seq, gw), F32), pltpu.VMEM((2, seq, gw), F32),
                        pltpu.VMEM((2, n, gw), F32)],
        compiler_params=_params("arbitrary"),
        name="ssd_mixer",
    )(pz, pxc, p_t, pxc, cum_t, dt_t, wbt, bbt_col, dskip_cols, norm_w)


def _layer_norm(v, g, b):
    mu = jnp.mean(v, axis=-1, keepdims=True)
    c = v - mu
    var = jnp.mean(jnp.square(c), axis=-1, keepdims=True)
    return c * lax.rsqrt(var + LN_EPS) * g + b


def _merge_kernel(yh_ref, ys_ref, gh_ref, gs_ref, x_ref, whb_ref, wsb_ref, wo_ref, bo_ref,
                  lg_ref, lb_ref, rwt_ref, h_ref, hp_ref, lt_ref):
    a = jnp.dot(yh_ref[...], whb_ref[...], preferred_element_type=F32)
    b = jnp.dot(ys_ref[...], wsb_ref[...], preferred_element_type=F32)
    m = _sigmoid(gh_ref[...].astype(F32)) * a + _sigmoid(gs_ref[...].astype(F32)) * b
    mix = jnp.dot(m.astype(BF16), wo_ref[...], preferred_element_type=F32) + bo_ref[...]
    h = _layer_norm(DN_ALPHA * x_ref[...] + mix, lg_ref[...], lb_ref[...])
    h_ref[...] = h
    hp_ref[...] = _pack_halves(h)
    lt_ref[...] = lax.dot_general(rwt_ref[...], h, (((1,), (1,)), ((), ())), precision=HI,
                                  preferred_element_type=F32)


def _merge(yhy, yssd, pm, gate_off, x2d, x_row0, whb, wsb, wo, bo, lg, lb, rwt, tm):
    t, c = yhy.shape
    d = x2d.shape[1]
    d_ssm = yssd.shape[1]
    e = rwt.shape[0]
    xb0 = x_row0 // tm
    row = lambda w: pl.BlockSpec((tm, w), lambda i: (i, 0))
    whole = lambda a: pl.BlockSpec(a.shape, lambda i: (0,) * a.ndim)
    return pl.pallas_call(
        _merge_kernel,
        grid=(t // tm,),
        in_specs=[row(c), row(d_ssm),
                  pl.BlockSpec((tm, d), lambda i: (i, gate_off)),
                  pl.BlockSpec((tm, d), lambda i: (i, gate_off + 1)),
                  pl.BlockSpec((tm, d), lambda i: (xb0 + i, 0)),
                  whole(whb), whole(wsb), whole(wo), whole(bo), whole(lg), whole(lb), whole(rwt)],
        out_specs=[row(d), row(d // 2), pl.BlockSpec((e, tm), lambda i: (0, i))],
        out_shape=[jax.ShapeDtypeStruct((t, d), F32), jax.ShapeDtypeStruct((t, d // 2), jnp.int32),
                   jax.ShapeDtypeStruct((e, t), F32)],
        compiler_params=_params("parallel"),
        name="merge_ln",
    )(yhy, yssd, pm, pm, x2d, whb, wsb, wo, bo, lg, lb, rwt)


def _first_max(v, axis, size):
    idx = lax.broadcasted_iota(jnp.int32, v.shape, axis)
    top = jnp.max(v, axis=axis, keepdims=True)
    first = jnp.min(jnp.where(v == top, idx, size), axis=axis, keepdims=True)
    return idx == first, top


def _route_kernel(lt_ref, bias_ref, w_ref, sel_ref):
    n_e, tn = lt_ref.shape
    per = n_e // N_EXPERT_GROUPS
    scores = jax.nn.sigmoid(lt_ref[...])
    biased = scores + bias_ref[...]
    b3 = biased.reshape(N_EXPERT_GROUPS, per, tn)
    hit, top1 = _first_max(b3, 1, per)
    top2 = jnp.max(jnp.where(hit, -jnp.inf, b3), axis=1, keepdims=True)
    cur = (top1 + top2).reshape(N_EXPERT_GROUPS, tn)
    grp = jnp.zeros(cur.shape, jnp.bool_)
    for _ in range(TOPK_GROUPS):
        hit, _ = _first_max(cur, 0, N_EXPERT_GROUPS)
        grp = jnp.logical_or(grp, hit)
        cur = jnp.where(hit, -jnp.inf, cur)
    cur = jnp.where(grp.reshape(N_EXPERT_GROUPS, 1, tn), b3, -jnp.inf).reshape(n_e, tn)
    chosen = jnp.zeros(cur.shape, jnp.bool_)
    for _ in range(TOP_K):
        hit, _ = _first_max(cur, 0, n_e)
        chosen = jnp.logical_or(chosen, hit)
        cur = jnp.where(hit, -jnp.inf, cur)
    w = jnp.where(chosen, scores, 0.0)
    w_ref[...] = w / jnp.sum(w, axis=0, keepdims=True) * ROUTED_SCALE
    sel_ref[...] = jnp.where(chosen, 1.0, 0.0).astype(sel_ref.dtype)


def _route(logits_t, bias_col, tn):
    e, t = logits_t.shape
    blk = pl.BlockSpec((e, tn), lambda i: (0, i))
    return pl.pallas_call(
        _route_kernel,
        grid=(t // tn,),
        in_specs=[blk, pl.BlockSpec((e, 1), lambda i: (0, 0))],
        out_specs=[blk, blk],
        out_shape=[jax.ShapeDtypeStruct((e, t), F32), jax.ShapeDtypeStruct((e, t), BF16)],
        compiler_params=_params("parallel"),
        name="route_topk",
    )(logits_t, bias_col)


MOE_BLOCK = 512
BATCH_RANGES = 2


def _rank_kernel(sel_ref, pos_ref, cnt_ref, carry_ref):
    n_e, tn = sel_ref.shape

    @pl.when(pl.program_id(0) == 0)
    def _():
        carry_ref[...] = jnp.zeros_like(carry_ref)

    sel = sel_ref[...]
    s = lax.broadcasted_iota(jnp.int32, (tn, tn), 0)
    t = lax.broadcasted_iota(jnp.int32, (tn, tn), 1)
    earlier = jnp.where(s < t, 1.0, 0.0).astype(BF16)
    before = jnp.dot(sel, earlier, preferred_element_type=F32)
    carry = carry_ref[...]
    pos_ref[...] = carry[:, 0:1] + before
    carry = carry + jnp.sum(sel.astype(F32), axis=1, keepdims=True)
    carry_ref[...] = carry
    cnt_ref[...] = carry


def _rank(sel_t, tn):
    e, t = sel_t.shape
    return pl.pallas_call(
        _rank_kernel,
        grid=(t // tn,),
        in_specs=[pl.BlockSpec((e, tn), lambda i: (0, i))],
        out_specs=[pl.BlockSpec((e, tn), lambda i: (0, i)), pl.BlockSpec((e, 128), lambda i: (0, 0))],
        out_shape=[jax.ShapeDtypeStruct((e, t), F32), jax.ShapeDtypeStruct((e, 128), F32)],
        scratch_shapes=[pltpu.VMEM((e, 128), F32)],
        compiler_params=_params("arbitrary"),
        name="moe_rank",
    )(sel_t)


def _segment_starts(cnt):
    n_e = cnt.shape[0]
    padded = jnp.floor((cnt + (MOE_BLOCK - 1)) * (1.0 / MOE_BLOCK)) * MOE_BLOCK
    a = lax.broadcasted_iota(jnp.int32, (n_e, n_e), 0)
    b = lax.broadcasted_iota(jnp.int32, (n_e, n_e), 1)
    below = jnp.where(b < a, 1.0, 0.0).astype(F32)
    return padded, jnp.dot(below, padded, precision=HI, preferred_element_type=F32)


def _slots_kernel(sel_ref, w_ref, pos_ref, cnt_ref, slot_ref, wk_ref):
    n_e, tn = sel_ref.shape
    _, start = _segment_starts(cnt_ref[...])
    slot = start[:, 0:1] + pos_ref[...]
    w = w_ref[...]
    cur = sel_ref[...].astype(F32)
    idx = lax.broadcasted_iota(jnp.int32, (n_e, tn), 0)
    slots, weights = [], []
    for _ in range(TOP_K):
        first = jnp.min(jnp.where(cur > 0.0, idx, n_e), axis=0, keepdims=True)
        hit = idx == first
        slots.append(jnp.sum(jnp.where(hit, slot, 0.0), axis=0, keepdims=True))
        weights.append(jnp.sum(jnp.where(hit, w, 0.0), axis=0, keepdims=True))
        cur = jnp.where(hit, 0.0, cur)
    slot_ref[...] = jnp.concatenate(slots, axis=0).astype(jnp.int32)
    wk_ref[...] = jnp.concatenate(weights, axis=0).T


def _slots(sel_t, w_t, pos_t, cnt, tn):
    e, t = sel_t.shape
    blk = pl.BlockSpec((e, tn), lambda i: (0, i))
    return pl.pallas_call(
        _slots_kernel,
        grid=(t // tn,),
        in_specs=[blk, blk, blk, pl.BlockSpec((e, 128), lambda i: (0, 0))],
        out_specs=[pl.BlockSpec((TOP_K, tn), lambda i: (0, i)), pl.BlockSpec((tn, TOP_K), lambda i: (i, 0))],
        out_shape=[jax.ShapeDtypeStruct((TOP_K, t), jnp.int32), jax.ShapeDtypeStruct((t, TOP_K), F32)],
        compiler_params=_params("parallel"),
        name="moe_slots",
    )(sel_t, w_t, pos_t, cnt)


def _plan_kernel(cnt_ref, expert_ref, valid_ref):
    n_e = cnt_ref.shape[0]
    nbp = expert_ref.shape[1]
    cnt = cnt_ref[...]
    padded, start = _segment_starts(cnt)
    end = (start + padded)[:, 0:1]
    last = (start + cnt)[:, 0:1]
    row0 = (lax.broadcasted_iota(jnp.int32, (1, nbp), 1) * MOE_BLOCK).astype(F32)
    expert = jnp.minimum(jnp.sum(jnp.where(end <= row0, 1.0, 0.0), axis=0, keepdims=True), n_e - 1.0)
    eid = lax.broadcasted_iota(jnp.int32, (n_e, nbp), 0).astype(F32)
    last_b = jnp.sum(jnp.where(eid == expert, last, 0.0), axis=0, keepdims=True)
    expert_ref[...] = expert.astype(jnp.int32)
    valid_ref[...] = jnp.clip(last_b - row0, 0.0, float(MOE_BLOCK)).astype(jnp.int32)


def _plan(cnt, n_blocks):
    nbp = -(-n_blocks // 128) * 128
    return pl.pallas_call(
        _plan_kernel,
        out_shape=[jax.ShapeDtypeStruct((1, nbp), jnp.int32)] * 2,
        name="moe_plan",
    )(cnt)


V7X_SC_CORES = 2
V7X_SC_SUBCORES = 16
SC_WINDOW = 128


def _sc_mesh():
    return plsc.VectorSubcoreMesh(core_axis_name="c", subcore_axis_name="s",
                                  num_cores=V7X_SC_CORES, num_subcores=V7X_SC_SUBCORES)


def _sc_scatter_rows(rows, row0, idx, n_out):
    width = rows.shape[1]
    k_n, t = idx.shape
    workers = V7X_SC_CORES * V7X_SC_SUBCORES
    per_worker = t // workers
    assert per_worker * workers == t and per_worker % SC_WINDOW == 0

    def body(rows_hbm, idx_hbm, out_hbm, idx_v, rows_v, sem):
        wid = lax.axis_index("s") * V7X_SC_CORES + lax.axis_index("c")

        @pl.loop(0, per_worker // SC_WINDOW)
        def _(j):
            base = wid * per_worker + j * SC_WINDOW
            pltpu.sync_copy(rows_hbm.at[pl.ds(row0 + base, SC_WINDOW)], rows_v)
            for k in range(k_n):
                pltpu.sync_copy(idx_hbm.at[pl.ds(k * t + base, SC_WINDOW)], idx_v)
                pltpu.async_copy(rows_v, out_hbm.at[idx_v], sem).wait()

    return pl.kernel(
        body, out_type=jax.ShapeDtypeStruct((n_out, width), rows.dtype), mesh=_sc_mesh(),
        scratch_types=[pltpu.VMEM((SC_WINDOW,), jnp.int32), pltpu.VMEM((SC_WINDOW, width), rows.dtype),
                       pltpu.SemaphoreType.DMA],
        name="moe_dispatch",
    )(rows, idx.reshape(-1))


def _sc_gather_rows(table, idx):
    width = table.shape[1]
    k_n, t = idx.shape
    workers = V7X_SC_CORES * V7X_SC_SUBCORES
    per_worker = t // workers
    assert per_worker * workers == t and per_worker % SC_WINDOW == 0

    def body(table_hbm, idx_hbm, out_hbm, idx_v, rows_v, sem):
        wid = lax.axis_index("s") * V7X_SC_CORES + lax.axis_index("c")

        @pl.loop(0, per_worker // SC_WINDOW)
        def _(j):
            base = wid * per_worker + j * SC_WINDOW
            for k in range(k_n):
                pltpu.sync_copy(idx_hbm.at[pl.ds(k * t + base, SC_WINDOW)], idx_v)
                pltpu.async_copy(table_hbm.at[idx_v], rows_v, sem).wait()
                pltpu.sync_copy(rows_v, out_hbm.at[pl.ds(k * t + base, SC_WINDOW)])

    out = pl.kernel(
        body, out_type=jax.ShapeDtypeStruct((k_n * t, width), table.dtype), mesh=_sc_mesh(),
        scratch_types=[pltpu.VMEM((SC_WINDOW,), jnp.int32), pltpu.VMEM((SC_WINDOW, width), table.dtype),
                       pltpu.SemaphoreType.DMA],
        name="moe_collect",
    )(table, idx.reshape(-1))
    return out.reshape(k_n, t, width)


def _swiglu(xb, wg, wu):
    gate = jnp.dot(xb, wg, preferred_element_type=F32)
    up = jnp.dot(xb, wu, preferred_element_type=F32)
    return _silu(gate) * up


def _pack_halves(v):
    w = v.shape[1] // 2
    hi = pltpu.bitcast(v[:, :w].astype(BF16).astype(F32), jnp.int32)
    lo = pltpu.bitcast(v[:, w:].astype(BF16).astype(F32), jnp.int32)
    return jnp.bitwise_or(hi, lax.shift_right_logical(lo, 16))


def _unpack_halves(p):
    hi = pltpu.bitcast(jnp.bitwise_and(p, jnp.int32(-65536)), F32)
    lo = pltpu.bitcast(lax.shift_left(p, 16), F32)
    return jnp.concatenate([hi, lo], axis=1)


def _experts_kernel(expert_ref, valid_ref, x_ref, wg_ref, wu_ref, wd_ref, y_ref, wg_s, wu_s, wd_s):
    b = pl.program_id(0)
    valid = valid_ref[0, b]
    fresh = jnp.logical_or(b == 0, expert_ref[0, b] != expert_ref[0, jnp.maximum(b - 1, 0)])

    @pl.when(jnp.logical_and(valid > 0, fresh))
    def _():
        wg_s[...] = wg_ref[0].astype(BF16)
        wu_s[...] = wu_ref[0].astype(BF16)
        wd_s[...] = wd_ref[0].astype(BF16)

    @pl.when(valid > 0)
    def _():
        rows = lax.broadcasted_iota(jnp.int32, (x_ref.shape[0], 1), 0)
        x = jnp.where(rows < valid, _unpack_halves(x_ref[...]), 0.0).astype(BF16)
        mid = _swiglu(x, wg_s[...], wu_s[...]).astype(BF16)
        y_ref[...] = _pack_halves(jnp.dot(mid, wd_s[...], preferred_element_type=F32))

    @pl.when(valid == 0)
    def _():
        y_ref[...] = jnp.zeros_like(y_ref)


def _experts(x_sorted, expert, valid, wg, wu, wd):
    n_rows, half = x_sorted.shape
    n_e, d, f = wg.shape
    by_expert = lambda shape: pl.BlockSpec((1,) + shape, lambda b, e_ref, v_ref: (e_ref[0, b], 0, 0))
    rows = pl.BlockSpec((MOE_BLOCK, half), lambda b, e_ref, v_ref: (b, 0))
    return pl.pallas_call(
        _experts_kernel,
        grid_spec=pltpu.PrefetchScalarGridSpec(
            num_scalar_prefetch=2, grid=(n_rows // MOE_BLOCK,),
            in_specs=[rows, by_expert((d, f)), by_expert((d, f)), by_expert((f, d))],
            out_specs=rows,
            scratch_shapes=[pltpu.VMEM((d, f), BF16), pltpu.VMEM((d, f), BF16), pltpu.VMEM((f, d), BF16)]),
        out_shape=jax.ShapeDtypeStruct((n_rows, half), jnp.int32),
        compiler_params=_params("arbitrary"),
        name="moe_experts",
    )(expert, valid, x_sorted, wg, wu, wd)


def _combine_kernel(y_ref, wk_ref, hp_ref, h_ref, sg_ref, su_ref, sd_ref, lg_ref, lb_ref, o_ref):
    wk = wk_ref[...]
    mid = _swiglu(_unpack_halves(hp_ref[...]).astype(BF16), sg_ref[...], su_ref[...]).astype(BF16)
    acc = jnp.dot(mid, sd_ref[...], preferred_element_type=F32)
    for k in range(y_ref.shape[0]):
        acc = acc + wk[:, k:k + 1] * _unpack_halves(y_ref[k])
    o_ref[...] = _layer_norm(DN_ALPHA * h_ref[...] + acc, lg_ref[...], lb_ref[...])


def _combine_kernel_into(y_ref, wk_ref, hp_ref, h_ref, sg_ref, su_ref, sd_ref, lg_ref, lb_ref, prev_ref,
                         o_ref):
    del prev_ref
    _combine_kernel(y_ref, wk_ref, hp_ref, h_ref, sg_ref, su_ref, sd_ref, lg_ref, lb_ref, o_ref)


def _combine(y_tok, wk, hp, h, in_row0, t_out, out_row0, prev, sg, su, sd, lg, lb, tm):
    d = h.shape[1]
    k_n, t_part, half = y_tok.shape
    bi, bo = in_row0 // tm, out_row0 // tm
    part = lambda w: pl.BlockSpec((tm, w), lambda i: (i, 0))
    src = lambda w: pl.BlockSpec((tm, w), lambda i: (bi + i, 0))
    whole = lambda a: pl.BlockSpec(a.shape, lambda i: (0,) * a.ndim)
    in_specs = [pl.BlockSpec((k_n, tm, half), lambda i: (0, i, 0)), part(k_n), src(half), src(d),
                whole(sg), whole(su), whole(sd), whole(lg), whole(lb)]
    args = [y_tok, wk, hp, h, sg, su, sd, lg, lb]
    if prev is not None:
        in_specs.append(pl.BlockSpec(memory_space=pl.ANY))
        args.append(prev)
    return pl.pallas_call(
        _combine_kernel if prev is None else _combine_kernel_into,
        grid=(t_part // tm,),
        in_specs=in_specs,
        out_specs=pl.BlockSpec((tm, d), lambda i: (bo + i, 0)),
        out_shape=jax.ShapeDtypeStruct((t_out, d), F32),
        input_output_aliases={} if prev is None else {len(args) - 1: 0},
        compiler_params=_params("parallel"),
        name="moe_combine_ln",
    )(*args)


def _pick(n, *cands):
    for c in cands:
        if n % c == 0:
            return c
    return n


def _layer(h, w_in, b_in, hy_conv_w, hy_conv_b, hy_f_w1, hy_f_b1, hy_f_freq1, hy_f_w2, hy_f_b2,
           hy_f_freq2, hy_f_w3, hy_f_b3, hy_f_freq3, hy_f_w4, hy_bias, ssd_conv_w, ssd_conv_b,
           ssd_dt_bias, ssd_a_log, ssd_d, ssd_norm_w, w_hy_branch, w_ssd_branch, w_out, b_out,
           ln1_g, ln1_b, router_w, router_bias, exp_w_gate, exp_w_up, exp_w_down,
           sh_w_gate, sh_w_up, sh_w_down, ln2_g, ln2_b):
    bsz, seq, d = h.shape
    t = bsz * seq
    d_hy = hy_bias.shape[-1]
    d_ssm = ssd_norm_w.shape[-1]
    conv_dim = ssd_conv_w.shape[-1]
    heads = ssd_d.shape[-1]
    hy_cols = (HY_ORDER + 1) * d_hy
    o_z, o_dt, o_gate = hy_cols, hy_cols + d_ssm + conv_dim, hy_cols + d_ssm + conv_dim + 2 * heads
    o_b = o_z + 2 * d_ssm
    o_c = o_b + SSD_GROUPS * SSD_STATE
    row = lambda v: v.astype(F32).reshape(1, -1)
    col = lambda v: v.astype(F32).reshape(-1, 1)

    ch = lambda v: jnp.concatenate([v[..., :o_z], v[..., o_b:o_c]], axis=-1)
    w_ch, b_ch = ch(w_in).T.astype(BF16), col(ch(b_in))
    tmaj = lambda v: jnp.concatenate([v[..., o_z:o_b], v[..., o_c:o_dt], v[..., o_gate:]], axis=-1)
    w_main, b_main = tmaj(w_in).astype(BF16), row(tmaj(b_in))
    pad = 128 - 2 * heads
    hpg = heads // SSD_GROUPS
    perm = [dr * heads + g * hpg + hh for g in range(SSD_GROUPS) for dr in range(2) for hh in range(hpg)]
    by_group = lambda v: jnp.pad(v.astype(F32).reshape(-1)[jnp.array(perm)], (0, pad)).reshape(1, -1)
    w_dt = jnp.pad(w_in[:, o_dt:o_gate][:, jnp.array(perm)], ((0, 0), (0, pad))).astype(BF16)
    b_dt = by_group(b_in[o_dt:o_gate])
    p = _pick(seq, 512, 256, 128, 64)
    r = _pick(d_hy, 128, 64)
    spectra = _hyena_spectra(seq, p, hy_f_w1, hy_f_b1, hy_f_freq1, hy_f_w2, hy_f_b2, hy_f_freq2,
                             hy_f_w3, hy_f_b3, hy_f_freq3, hy_f_w4, d_hy, r, BF16)
    branch_w = (w_hy_branch.astype(BF16), w_ssd_branch.astype(BF16), w_out.astype(BF16),
                row(b_out), row(ln1_g), row(ln1_b), router_w.T.astype(F32))
    shared = (sh_w_gate.astype(BF16), sh_w_up.astype(BF16), sh_w_down.astype(BF16))
    experts_w = (exp_w_gate.astype(F32), exp_w_up.astype(F32), exp_w_down.astype(F32))
    x_res = h.reshape(t, d).astype(F32)

    sc_rows = V7X_SC_CORES * V7X_SC_SUBCORES * SC_WINDOW
    n_ranges = BATCH_RANGES if bsz % BATCH_RANGES == 0 and (t // BATCH_RANGES) % sc_rows == 0 else 1
    bp = bsz // n_ranges
    tp = bp * seq
    tm = _pick(tp, 1024, 512, 256, 128)
    tn = _pick(tp, 512, 256, 128)
    n_blocks = -(-tp * TOP_K // MOE_BLOCK) + router_w.shape[-1]
    out = None
    for part in range(n_ranges):
        xb = h[part * bp:(part + 1) * bp].astype(BF16)
        x2d = xb.reshape(tp, d)
        p_t = _matmul_channels_major(w_ch, xb, b_ch, BF16, _pick(hy_cols + o_c - o_b, 1024, 512, 384, 128))
        pm = _matmul_bias(x2d, w_main, b_main, BF16, tm, _pick(w_main.shape[1], 1024, 512, 256, 128))
        p_dt = _matmul_bias(x2d, w_dt, b_dt, F32, tm, 128)
        y_hy = _hyena_mixer(p_t, hy_conv_w.T.astype(F32), col(hy_conv_b), spectra, col(hy_bias),
                            d_hy, p, r, _pick(bp, 2, 1), BF16)
        cum_t, dt_t = _ssd_prep(p_dt.reshape(bp, seq, 128), by_group(ssd_dt_bias),
                                by_group(-jnp.exp(ssd_a_log.astype(F32))), hpg)
        y_ssd = _ssd_mixer(pm.reshape(bp, seq, -1), p_t, hy_cols // SSD_STATE, cum_t, dt_t,
                           ssd_conv_w.astype(F32), row(ssd_conv_b),
                           row(jnp.repeat(ssd_d, SSD_HEADDIM)), row(ssd_norm_w), d_ssm, BF16)
        h1, h1p, logits_t = _merge(y_hy.reshape(tp, d_hy), y_ssd.reshape(tp, d_ssm), pm,
                                   (o_dt - o_z - (o_c - o_b)) // d, x_res, part * tp, *branch_w, tn)
        w_t, sel_t = _route(logits_t, col(router_bias), tn)
        pos_t, cnt = _rank(sel_t, tn)
        slot_kt, wk = _slots(sel_t, w_t, pos_t, cnt, tn)
        expert, valid = _plan(cnt, n_blocks)
        x_sorted = _sc_scatter_rows(h1p, 0, slot_kt, n_blocks * MOE_BLOCK)
        y_sorted = _experts(x_sorted, expert, valid, *experts_w)
        y_tok = _sc_gather_rows(y_sorted, slot_kt)
        out = _combine(y_tok, wk, h1p, h1, 0, t, part * tp, out, *shared, row(ln2_g), row(ln2_b), tn)
    return out.reshape(bsz, seq, d)


def kernel(x, w_in, b_in, hy_conv_w, hy_conv_b, hy_f_w1, hy_f_b1, hy_f_freq1, hy_f_w2, hy_f_b2,
           hy_f_freq2, hy_f_w3, hy_f_b3, hy_f_freq3, hy_f_w4, hy_bias, ssd_conv_w, ssd_conv_b,
           ssd_dt_bias, ssd_a_log, ssd_d, ssd_norm_w, w_hy_branch, w_ssd_branch, w_out, b_out,
           ln1_g, ln1_b, router_w, router_bias, exp_w_gate, exp_w_up, exp_w_down,
           sh_w_gate, sh_w_up, sh_w_down, ln2_g, ln2_b):
    params = (w_in, b_in, hy_conv_w, hy_conv_b, hy_f_w1, hy_f_b1, hy_f_freq1, hy_f_w2, hy_f_b2,
              hy_f_freq2, hy_f_w3, hy_f_b3, hy_f_freq3, hy_f_w4, hy_bias, ssd_conv_w, ssd_conv_b,
              ssd_dt_bias, ssd_a_log, ssd_d, ssd_norm_w, w_hy_branch, w_ssd_branch, w_out, b_out,
              ln1_g, ln1_b, router_w, router_bias, exp_w_gate, exp_w_up, exp_w_down,
              sh_w_gate, sh_w_up, sh_w_down, ln2_g, ln2_b)
    h = x
    for i in range(DEPTH):
        h = _layer(h, *(p[i] for p in params))
    return h
```

```python
import functools
import math

import jax
import jax.numpy as jnp
from jax import lax
from jax.experimental import pallas as pl
from jax.experimental.pallas import tpu as pltpu
from jax.experimental.pallas import tpu_sc as plsc

F32 = jnp.float32
BF16 = jnp.bfloat16
HI = lax.Precision.HIGHEST

HY_ORDER = 2
HY_EMB_DIM = 33
HY_BANDS = (HY_EMB_DIM - 1) // 2
HY_FAST_DECAY_PCT = 0.3
HY_SLOW_DECAY_PCT = 1.5
HY_DECAY_TARGET = 1e-2
SSD_HEADDIM = 64
SSD_GROUPS = 8
SSD_STATE = 128
SSD_CONV = 5
SSD_CHUNK = 128
N_EXPERT_GROUPS = 8
TOPK_GROUPS = 4
TOP_K = 8
ROUTED_SCALE = 2.5
DEPTH = 1
DN_ALPHA = (2.0 * DEPTH) ** 0.25
LN_EPS = 1e-5
RMS_EPS = 1e-5

V7X_VMEM_LIMIT = 56 * 1024 * 1024


def _params(*sem):
    return pltpu.CompilerParams(dimension_semantics=sem, vmem_limit_bytes=V7X_VMEM_LIMIT)


def _sigmoid(v):
    return 0.5 + 0.5 * jnp.tanh(0.5 * v)


def _silu(v):
    hv = 0.5 * v
    return hv + hv * jnp.tanh(hv)


def _mm_bias_kernel(a_ref, b_ref, bias_ref, o_ref):
    acc = jnp.dot(a_ref[...], b_ref[...], preferred_element_type=F32)
    o_ref[...] = (acc + bias_ref[...]).astype(o_ref.dtype)


def _matmul_bias(a, b, bias, out_dtype, tm, tn):
    m, k = a.shape
    n = b.shape[1]
    return pl.pallas_call(
        _mm_bias_kernel,
        grid=(m // tm, n // tn),
        in_specs=[pl.BlockSpec((tm, k), lambda i, j: (i, 0)),
                  pl.BlockSpec((k, tn), lambda i, j: (0, j)),
                  pl.BlockSpec((1, tn), lambda i, j: (0, j))],
        out_specs=pl.BlockSpec((tm, tn), lambda i, j: (i, j)),
        out_shape=jax.ShapeDtypeStruct((m, n), out_dtype),
        compiler_params=_params("parallel", "parallel"),
        name="proj_rows",
    )(a, b, bias)


def _mm_nt_kernel(w_ref, x_ref, bias_ref, o_ref):
    acc = lax.dot_general(w_ref[...], x_ref[0], (((1,), (1,)), ((), ())),
                          preferred_element_type=F32)
    o_ref[0] = (acc + bias_ref[...]).astype(o_ref.dtype)


def _matmul_channels_major(w_t, x, bias_col, out_dtype, tc):
    c, k = w_t.shape
    bsz, seq, _ = x.shape
    return pl.pallas_call(
        _mm_nt_kernel,
        grid=(bsz, c // tc),
        in_specs=[pl.BlockSpec((tc, k), lambda b, i: (i, 0)),
                  pl.BlockSpec((1, seq, k), lambda b, i: (b, 0, 0)),
                  pl.BlockSpec((tc, 1), lambda b, i: (i, 0))],
        out_specs=pl.BlockSpec((1, tc, seq), lambda b, i: (b, i, 0)),
        out_shape=jax.ShapeDtypeStruct((bsz, c, seq), out_dtype),
        compiler_params=_params("parallel", "parallel"),
        name="proj_channels",
    )(w_t, x, bias_col)


def _hy_mlp_kernel(w1t_ref, b1_ref, f1_ref, w2t_ref, b2_ref, f2_ref, w3t_ref, b3_ref, f3_ref,
                   o_ref, *, seq, emb_rows):
    shape = (emb_rows, 2 * seq)
    lag = lax.broadcasted_iota(jnp.int32, shape, 1) - seq
    pos = jnp.abs(lag).astype(F32)
    row = lax.broadcasted_iota(jnp.int32, shape, 0)
    t = pos / (seq - 1)
    w = (2.0 * math.pi) * pos / seq
    band = jnp.where(row <= HY_BANDS, row - 1, row - 1 - HY_BANDS).astype(F32)
    freq = 1e-4 + band * ((HY_BANDS - 1 - 1e-4) / (HY_BANDS - 1))
    ang = freq * w
    z = jnp.where(row == 0, t,
                  jnp.where(row <= HY_BANDS, jnp.cos(ang),
                            jnp.where(row <= 2 * HY_BANDS, -jnp.sin(ang), 0.0)))
    h = jnp.sin(f1_ref[...] * (jnp.dot(w1t_ref[...], z, precision=HI, preferred_element_type=F32)
                               + b1_ref[...]))
    h = jnp.sin(f2_ref[...] * (jnp.dot(w2t_ref[...], h, precision=HI, preferred_element_type=F32)
                               + b2_ref[...]))
    h = jnp.sin(f3_ref[...] * (jnp.dot(w3t_ref[...], h, precision=HI, preferred_element_type=F32)
                               + b3_ref[...]))
    o_ref[...] = h


def _dft_matrices(p):
    e = lax.broadcasted_iota(jnp.int32, (p, 2 * p), 0)
    j = lax.broadcasted_iota(jnp.int32, (p, 2 * p), 1)
    k = jnp.where(j < p, j, j - p)
    theta = ((e * k) % (2 * p)).astype(F32) * (math.pi / p)
    sign_e = jnp.where(e % 2 == 0, 1.0, -1.0)
    fwd = jnp.where(j < p, jnp.cos(theta), jnp.where(j == p, sign_e, -jnp.sin(theta)))
    jj = lax.broadcasted_iota(jnp.int32, (2 * p, p), 0)
    tt = lax.broadcasted_iota(jnp.int32, (2 * p, p), 1)
    kk = jnp.where(jj < p, jj, jj - p)
    th = ((tt * kk) % (2 * p)).astype(F32) * (math.pi / p)
    sign_t = jnp.where(tt % 2 == 0, 1.0, -1.0)
    n = 2.0 * p
    inv = jnp.where(jj == 0, 1.0 / n,
                    jnp.where(jj < p, (2.0 / n) * jnp.cos(th),
                              jnp.where(jj == p, sign_t / n, (-2.0 / n) * jnp.sin(th))))
    return fwd.astype(F32), inv.astype(F32)


def _hy_spec_kernel(h_ref, w4f_ref, w4b_ref, fwd_ref, o_ref, *, seq, p, d_hy, rows):
    nb = seq // p
    h = h_ref[...]
    kpos = jnp.dot(w4f_ref[...], h[:, seq:], precision=HI, preferred_element_type=F32)
    kneg = jnp.dot(w4b_ref[...], h[:, :seq], precision=HI, preferred_element_type=F32)
    kb0 = jnp.dot(w4b_ref[...], h[:, seq:seq + 128], precision=HI, preferred_element_type=F32)
    lane128 = lax.broadcasted_iota(jnp.int32, kb0.shape, 1)
    kb0 = jnp.sum(jnp.where(lane128 == 0, kb0, 0.0), axis=1, keepdims=True)

    r0 = pl.program_id(0) * rows
    ch = (r0 + lax.broadcasted_iota(jnp.int32, (rows, 1), 0)) % d_hy
    max_decay = math.log(HY_DECAY_TARGET) / HY_FAST_DECAY_PCT
    min_decay = math.log(HY_DECAY_TARGET) / HY_SLOW_DECAY_PCT
    delta = jnp.abs(min_decay + ch.astype(F32) * ((max_decay - min_decay) / (d_hy - 1)))
    i = lax.broadcasted_iota(jnp.int32, (1, seq), 1)
    t_pos = i.astype(F32) / (seq - 1)
    t_neg = (seq - i).astype(F32) / (seq - 1)
    kpos = (kpos + jnp.where(i == 0, kb0, 0.0)) * jnp.exp(-t_pos * delta)
    kneg = jnp.where(i == 0, 0.0, kneg * jnp.exp(-t_neg * delta))
    klag = jnp.concatenate([kneg, kpos], axis=1)

    fwd = fwd_ref[...]
    lane = lax.broadcasted_iota(jnp.int32, (1, 2 * p), 1)
    sign = jnp.where(lane % 2 == 0, 1.0, -1.0)
    spec = []
    first = []
    for m in range(2 * nb):
        blk = klag[:, m * p:(m + 1) * p]
        spec.append(jnp.dot(blk, fwd, precision=HI, preferred_element_type=F32))
        first.append(blk[:, 0:1])
    half = lax.broadcasted_iota(jnp.int32, (1, p), 1)
    for m in range(1, 2 * nb):
        tail = spec[m - 1] - jnp.where(lane <= p, first[m - 1], 0.0)
        full = spec[m] + sign * tail
        re, im = full[:, :p], full[:, p:]
        o_ref[m - 1, 0] = re.astype(o_ref.dtype)
        o_ref[m - 1, 1] = jnp.where(half == 0, 0.0, im).astype(o_ref.dtype)
        o_ref[m - 1, 2] = jnp.where(half == 0, im, re).astype(o_ref.dtype)


def _hyena_spectra(seq, p, w1, b1, fr1, w2, b2, fr2, w3, b3, fr3, w4, d_hy, rows, spec_dtype):
    hidden = w1.shape[1]
    emb_rows = -(-HY_EMB_DIM // 8) * 8
    w1t = jnp.zeros((hidden, emb_rows), F32).at[:, :HY_EMB_DIM].set(w1.T.astype(F32))
    col = lambda v: v.astype(F32).reshape(hidden, 1)
    h = pl.pallas_call(
        functools.partial(_hy_mlp_kernel, seq=seq, emb_rows=emb_rows),
        out_shape=jax.ShapeDtypeStruct((hidden, 2 * seq), F32),
        name="hyena_filter_mlp",
    )(w1t, col(b1), col(fr1), w2.T.astype(F32), col(b2), col(fr2), w3.T.astype(F32), col(b3), col(fr3))
    w4r = w4.astype(F32).reshape(hidden, HY_ORDER, 2, d_hy)
    w4f = w4r[:, :, 0].reshape(hidden, HY_ORDER * d_hy).T
    w4b = w4r[:, :, 1].reshape(hidden, HY_ORDER * d_hy).T
    fwd, _ = _dft_matrices(p)
    nb = seq // p
    n_rows = HY_ORDER * d_hy
    return pl.pallas_call(
        functools.partial(_hy_spec_kernel, seq=seq, p=p, d_hy=d_hy, rows=rows),
        grid=(n_rows // rows,),
        in_specs=[pl.BlockSpec((hidden, 2 * seq), lambda r: (0, 0)),
                  pl.BlockSpec((rows, hidden), lambda r: (r, 0)),
                  pl.BlockSpec((rows, hidden), lambda r: (r, 0)),
                  pl.BlockSpec((p, 2 * p), lambda r: (0, 0))],
        out_specs=pl.BlockSpec((2 * nb - 1, 3, rows, p), lambda r: (0, 0, r, 0)),
        out_shape=jax.ShapeDtypeStruct((2 * nb - 1, 3, n_rows, p), spec_dtype),
        compiler_params=_params("parallel"),
        name="hyena_filter_spectra",
    )(h, w4f, w4b, fwd)


def _hy_conv_kernel(g1_ref, g2_ref, v_ref, w1_ref, w2_ref, wv_ref, b1_ref, b2_ref, bv_ref,
                    k0_ref, k1_ref, s0_ref, s1_ref, fwd_ref, inv_ref, o_ref, *, p):
    bt, r, seq = v_ref.shape
    nb = seq // p
    rows = bt * r
    lane = lax.broadcasted_iota(jnp.int32, (1, 1, seq), 2)

    def short_conv(x_ref, w_ref, b_ref):
        x = x_ref[...].astype(F32)
        x2 = x.reshape(rows, seq)
        prev = jnp.where(lane == 0, 0.0, pltpu.roll(x2, 1, axis=1).reshape(bt, r, seq))
        nxt = jnp.where(lane == seq - 1, 0.0, pltpu.roll(x2, seq - 1, axis=1).reshape(bt, r, seq))
        w = w_ref[...]
        return b_ref[...] + w[:, 0:1] * prev + w[:, 1:2] * x + w[:, 2:3] * nxt

    fwd = fwd_ref[...]
    inv = inv_ref[...]

    def long_conv(u, k_ref):
        ub = u.astype(BF16).reshape(rows, seq)
        spec = [jnp.dot(ub[:, j * p:(j + 1) * p], fwd, preferred_element_type=F32).reshape(bt, r, 2 * p)
                for j in range(nb)]
        outs = []
        for i in range(nb):
            acc_re = jnp.zeros((bt, r, p), F32)
            acc_im = jnp.zeros((bt, r, p), F32)
            for j in range(nb):
                m = i - j + nb - 1
                ka = k_ref[m, 0].astype(F32)
                kb = k_ref[m, 1].astype(F32)
                kc = k_ref[m, 2].astype(F32)
                u_re = spec[j][:, :, :p]
                u_im = spec[j][:, :, p:]
                acc_re = acc_re + ka * u_re - kb * u_im
                acc_im = acc_im + kc * u_im + kb * u_re
            y_spec = jnp.concatenate([acc_re, acc_im], axis=2).astype(BF16).reshape(rows, 2 * p)
            outs.append(jnp.dot(y_spec, inv, preferred_element_type=F32).reshape(bt, r, p))
        return jnp.concatenate(outs, axis=2)

    v = short_conv(v_ref, wv_ref, bv_ref)
    z = short_conv(g1_ref, w1_ref, b1_ref) * (long_conv(v, k0_ref) + v * s0_ref[...])
    y = short_conv(g2_ref, w2_ref, b2_ref) * (long_conv(z, k1_ref) + z * s1_ref[...])
    for b in range(bt):
        o_ref[b] = y[b].T.astype(o_ref.dtype)


def _hyena_mixer(p_t, conv_w_t, conv_b_col, spectra, skip_col, d_hy, p, r, bt, out_dtype):
    bsz, _, seq = p_t.shape
    nb = seq // p
    nblk = d_hy // r
    fwd, inv = _dft_matrices(p)
    x_spec = lambda off: pl.BlockSpec((bt, r, seq), lambda j, b: (b, off * nblk + j, 0))
    w_spec = lambda off: pl.BlockSpec((r, conv_w_t.shape[1]), lambda j, b: (off * nblk + j, 0))
    c_spec = lambda off: pl.BlockSpec((r, 1), lambda j, b: (off * nblk + j, 0))
    k_spec = lambda off: pl.BlockSpec((2 * nb - 1, 3, r, p), lambda j, b: (0, 0, off * nblk + j, 0))
    return pl.pallas_call(
        functools.partial(_hy_conv_kernel, p=p),
        grid=(nblk, bsz // bt),
        in_specs=[x_spec(0), x_spec(1), x_spec(2), w_spec(0), w_spec(1), w_spec(2),
                  c_spec(0), c_spec(1), c_spec(2), k_spec(0), k_spec(1), c_spec(0), c_spec(1),
                  pl.BlockSpec((p, 2 * p), lambda j, b: (0, 0)),
                  pl.BlockSpec((2 * p, p), lambda j, b: (0, 0))],
        out_specs=pl.BlockSpec((bt, seq, r), lambda j, b: (b, 0, j)),
        out_shape=jax.ShapeDtypeStruct((bsz, seq, d_hy), out_dtype),
        compiler_params=_params("parallel", "arbitrary"),
        name="hyena_mixer",
    )(p_t, p_t, p_t, conv_w_t, conv_w_t, conv_w_t, conv_b_col, conv_b_col, conv_b_col,
      spectra, spectra, skip_col, skip_col, fwd.astype(BF16), inv.astype(BF16))


def _ssd_prep_kernel(dt_ref, dtb_ref, a_ref, cum_ref, dtt_ref, *, r):
    seq = dt_ref.shape[1]
    q = SSD_CHUNK
    raw = dt_ref[0] + dtb_ref[...]
    dt = jnp.maximum(raw, 0.0) + jnp.log1p(jnp.exp(-jnp.abs(raw)))
    a = dt * a_ref[...]
    li = lax.broadcasted_iota(jnp.int32, (q, q), 0)
    si = lax.broadcasted_iota(jnp.int32, (q, q), 1)
    lower = jnp.where(li >= si, 1.0, 0.0).astype(F32)
    upper = jnp.where(li <= si, 1.0, 0.0).astype(F32)
    fwd_col = lax.broadcasted_iota(jnp.int32, (1, 128), 1) % (2 * r) < r
    for c in range(seq // q):
        ac = a[c * q:(c + 1) * q]
        cum = jnp.where(fwd_col,
                        jnp.dot(lower, ac, precision=HI, preferred_element_type=F32),
                        jnp.dot(upper, ac, precision=HI, preferred_element_type=F32))
        cum_ref[0, :, c * q:(c + 1) * q] = cum.T
        dtt_ref[0, :, c * q:(c + 1) * q] = dt[c * q:(c + 1) * q].T


def _ssd_prep(p_dt, dt_bias, a_neg, r):
    bsz, seq, _ = p_dt.shape
    vec = pl.BlockSpec((1, 128), lambda b: (0, 0))
    out = pl.BlockSpec((1, 128, seq), lambda b: (b, 0, 0))
    return pl.pallas_call(
        functools.partial(_ssd_prep_kernel, r=r),
        grid=(bsz,),
        in_specs=[pl.BlockSpec((1, seq, 128), lambda b: (b, 0, 0)), vec, vec],
        out_specs=[out, out],
        out_shape=[jax.ShapeDtypeStruct((bsz, 128, seq), F32)] * 2,
        compiler_params=_params("parallel"),
        name="ssd_prep",
    )(p_dt, dt_bias, a_neg)


def _ssd_kernel(z_ref, x_ref, bt_ref, c_ref, cum_ref, dtt_ref, wx_ref, wbt_ref, wc_ref,
                bx_ref, bbt_ref, bc_ref, dskip_ref, nw_ref, o_ref, xm_s, bt_s, cs_s, y_s, st_s):
    nbat, seq, gw = x_ref.shape
    n = bt_ref.shape[1]
    r = gw // SSD_HEADDIM
    q = SSD_CHUNK
    nc = seq // q
    half = SSD_CONV // 2
    shifts = [k - half for k in range(SSD_CONV) if k != half]
    halo_r, halo_l = 16, 128

    k_r = 2 * q
    sr = lax.broadcasted_iota(jnp.int32, (len(shifts) * q, k_r), 0)
    se = lax.broadcasted_iota(jnp.int32, (len(shifts) * q, k_r), 1)
    band_r = jnp.zeros(sr.shape, jnp.bool_)
    for j, sh in enumerate(shifts):
        band_r = jnp.logical_or(band_r, jnp.logical_and(sr // q == j, se == halo_r + sr % q + sh))
    band_r = jnp.where(band_r, 1.0, 0.0).astype(BF16)
    k_l = q + 2 * halo_l
    le = lax.broadcasted_iota(jnp.int32, (k_l, len(shifts) * q), 0)
    lc = lax.broadcasted_iota(jnp.int32, (k_l, len(shifts) * q), 1)
    band_l = jnp.zeros(le.shape, jnp.bool_)
    for j, sh in enumerate(shifts):
        band_l = jnp.logical_or(band_l, jnp.logical_and(lc // q == j, le == halo_l + lc % q + sh))
    band_l = jnp.where(band_l, 1.0, 0.0).astype(BF16)

    col_head = lax.broadcasted_iota(jnp.int32, (1, gw), 1) // SSD_HEADDIM
    wx, wc, wbt = wx_ref[...], wc_ref[...], wbt_ref[...]

    def conv_rows(ref, bb, c, off, w, bias):
        width = ref.shape[2]
        cur = ref[bb, pl.ds(off, q), :]
        prev = ref[bb, pl.ds(pl.multiple_of(jnp.maximum(off - halo_r, 0), halo_r), halo_r), :]
        nxt = ref[bb, pl.ds(pl.multiple_of(jnp.minimum(off + q, seq - halo_r), halo_r), halo_r), :]
        prev = jnp.where(c > 0, prev, jnp.zeros_like(prev))
        nxt = jnp.where(c < nc - 1, nxt, jnp.zeros_like(nxt))
        ext = jnp.concatenate([prev, cur, nxt, jnp.zeros((k_r - q - 2 * halo_r, width), BF16)], axis=0)
        moved = jnp.dot(band_r, ext, preferred_element_type=F32)
        acc = bias + w[half:half + 1] * cur.astype(F32)
        for j, sh in enumerate(shifts):
            acc = acc + w[half + sh:half + sh + 1] * moved[j * q:(j + 1) * q]
        return _silu(acc)

    def conv_one(bb, c, off):
        xs = conv_rows(x_ref, bb, c, off, wx, bx_ref[...])
        y_s[bb, 0, pl.ds(off, q), :] = xs * dskip_ref[...]
        for h in range(r):
            xm_s[bb, h, pl.ds(off, q), :] = jnp.where(col_head == h, xs, 0.0).astype(BF16)
        cs_s[bb, pl.ds(off, q), :] = conv_rows(c_ref, bb, c, off, wc, bc_ref[...]).astype(BF16)
        cur = bt_ref[bb, :, pl.ds(off, q)]
        prev = bt_ref[bb, :, pl.ds(pl.multiple_of(jnp.maximum(off - halo_l, 0), halo_l), halo_l)]
        nxt = bt_ref[bb, :, pl.ds(pl.multiple_of(jnp.minimum(off + q, seq - halo_l), halo_l), halo_l)]
        prev = jnp.where(c > 0, prev, jnp.zeros_like(prev))
        nxt = jnp.where(c < nc - 1, nxt, jnp.zeros_like(nxt))
        moved = jnp.dot(jnp.concatenate([prev, cur, nxt], axis=1), band_l, preferred_element_type=F32)
        acc = bbt_ref[...] + wbt[:, half:half + 1] * cur.astype(F32)
        for j, sh in enumerate(shifts):
            acc = acc + wbt[:, half + sh:half + sh + 1] * moved[:, j * q:(j + 1) * q]
        bt_s[bb, :, pl.ds(off, q)] = _silu(acc).astype(BF16)

    def conv_step(c, carry):
        off = pl.multiple_of(c * q, q)
        for bb in range(nbat):
            conv_one(bb, c, off)
        return carry

    lax.fori_loop(0, nc, conv_step, 0, unroll=2)
    st_s[...] = jnp.zeros_like(st_s)

    li = lax.broadcasted_iota(jnp.int32, (q, q), 0)
    si = lax.broadcasted_iota(jnp.int32, (q, q), 1)
    diag = li == si
    sub = lax.broadcasted_iota(jnp.int32, (2 * r, 1), 0)
    neg_inf = jnp.float32(-jnp.inf)
    nh = 2 * r
    assert 3 * nh + 16 <= q
    ones_rows = jnp.ones((8, q), F32)
    pad_rows = jnp.zeros((q - 3 * nh - 16, q), F32)
    pk = lax.broadcasted_iota(jnp.int32, (3 * nh + 8, nh * q), 0)
    pj = lax.broadcasted_iota(jnp.int32, (3 * nh + 8, nh * q), 1) // q
    seg_top = jnp.where(jnp.logical_and(pk < 3 * nh, pk % nh == pj), 1.0, 0.0).astype(BF16)
    seg_bottom = jnp.zeros((q - 3 * nh - 8 - 16, nh * q), BF16)

    def split3(v):
        hi = v.astype(BF16).astype(F32)
        mid = (v - hi).astype(BF16).astype(F32)
        lo = (v - hi - mid).astype(BF16).astype(F32)
        return hi, mid, lo

    def flat(v):
        return jnp.concatenate([v[j:j + 1, :] for j in range(nh)], axis=1)

    def chunk_step(i, carry):
        offs = (pl.multiple_of(i * q, q), pl.multiple_of((nc - 1 - i) * q, q))
        chains = [(bb, d) for bb in range(nbat) for d in range(2)]
        pre = []
        for bb in range(nbat):
            cum = jnp.where(sub < r, cum_ref[bb, :, pl.ds(offs[0], q)], cum_ref[bb, :, pl.ds(offs[1], q)])
            dt = jnp.where(sub < r, dtt_ref[bb, :, pl.ds(offs[0], q)], dtt_ref[bb, :, pl.ds(offs[1], q)])
            total = jnp.where(sub < r, cum[:, q - 1:q], cum[:, 0:1])
            hi, mid, lo = split3(cum)
            left = jnp.concatenate([hi, mid, lo, jnp.zeros((8, q), F32), ones_rows, pad_rows], axis=0)
            left = left.T.astype(BF16)
            minus = jnp.concatenate([-flat(hi), -flat(mid), -flat(lo), jnp.zeros((13, nh * q), F32)],
                                    axis=0)
            right = jnp.concatenate([seg_top, minus.astype(BF16), seg_bottom], axis=0)
            pre.append(dict(dt=dt, carry_decay=jnp.exp(total), in_decay=jnp.exp(cum),
                            out_w=jnp.exp(total - cum) * dt, left=left, right=right))
        seg_all = [jnp.dot(p_["left"], p_["right"], preferred_element_type=F32) for p_ in pre]
        btc = {c: bt_s[c[0], :, pl.ds(offs[c[1]], q)] for c in chains}
        cc = {c: cs_s[c[0], pl.ds(offs[c[1]], q), :] for c in chains}
        state = {c: st_s[c[0], c[1]] for c in chains}
        cb = {c: jnp.dot(cc[c], btc[c], preferred_element_type=F32) for c in chains}
        c_state = {c: jnp.dot(cc[c], state[c].astype(BF16), preferred_element_type=F32) for c in chains}
        lhs_y, rhs_y, lhs_s, rhs_s, decays = {}, {}, {}, {}, {}
        for c in chains:
            bb, d = c
            p_ = pre[bb]
            tri = (li >= si) if d == 0 else (li <= si)
            btf = btc[c].astype(F32)
            mats, diags, xs_h, cs_h, b_scaled = [], [], [], [], []
            state_decay = jnp.zeros((1, gw), F32)
            for h in range(r):
                ln = d * r + h
                seg = seg_all[bb][:, ln * q:(ln + 1) * q]
                mats.append((cb[c] * jnp.exp(jnp.where(tri, seg, neg_inf)) * p_["dt"][ln:ln + 1, :])
                            .astype(BF16))
                diags.append(jnp.where(diag, p_["in_decay"][ln:ln + 1, :], 0.0).astype(BF16))
                xs_h.append(xm_s[bb, h, pl.ds(offs[d], q), :])
                cs_h.append(jnp.where(col_head == h, c_state[c], 0.0).astype(BF16))
                b_scaled.append((btf * p_["out_w"][ln:ln + 1, :]).astype(BF16))
                state_decay = jnp.where(col_head == h, p_["carry_decay"][ln:ln + 1, :], state_decay)
            lhs_y[c] = jnp.concatenate(mats + diags, axis=1)
            rhs_y[c] = jnp.concatenate(xs_h + cs_h, axis=0)
            lhs_s[c] = jnp.concatenate(b_scaled, axis=1)
            rhs_s[c] = jnp.concatenate(xs_h, axis=0)
            decays[c] = state_decay
        y_new = {c: jnp.dot(lhs_y[c], rhs_y[c], preferred_element_type=F32) for c in chains}
        upd = {c: jnp.dot(lhs_s[c], rhs_s[c], preferred_element_type=F32) for c in chains}
        for c in chains:
            bb, d = c
            y_s[bb, 1 + d, pl.ds(offs[d], q), :] = y_new[c]
            st_s[bb, d] = state[c] * decays[c] + upd[c]
        return carry

    lax.fori_loop(0, nc, chunk_step, 0, unroll=4)

    def gate_step(c, carry):
        off = pl.multiple_of(c * q, q)
        for bb in range(nbat):
            zf = z_ref[bb, pl.ds(off, q), :].astype(F32)
            y = (y_s[bb, 0, pl.ds(off, q), :] + y_s[bb, 1, pl.ds(off, q), :]
                 + y_s[bb, 2, pl.ds(off, q), :])
            y = y * _silu(zf)
            y = y * lax.rsqrt(jnp.mean(jnp.square(y), axis=-1, keepdims=True) + RMS_EPS)
            o_ref[bb, pl.ds(off, q), :] = (y * nw_ref[...]).astype(o_ref.dtype)
        return carry

    lax.fori_loop(0, nc, gate_step, 0, unroll=2)


def _ssd_mixer(pm, p_t, bt_off, cum_t, dt_t, conv_w, conv_b, dskip_cols, norm_w, d_ssm, out_dtype):
    bsz, seq, _ = pm.shape
    gw = d_ssm // SSD_GROUPS
    r = gw // SSD_HEADDIM
    n = SSD_STATE
    xo, co = d_ssm // gw, 2 * d_ssm // n
    cw = lambda width, off: pl.BlockSpec((SSD_CONV, width), lambda b, g: (0, off + g))
    cb = lambda width, off: pl.BlockSpec((1, width), lambda b, g: (0, off + g))
    nbat = _pick(bsz, 2, 1)
    heads_rows = pl.BlockSpec((nbat, 2 * r, seq), lambda b, g: (b, g, 0))
    return pl.pallas_call(
        _ssd_kernel,
        grid=(bsz // nbat, SSD_GROUPS),
        in_specs=[pl.BlockSpec((nbat, seq, gw), lambda b, g: (b, 0, g)),
                  pl.BlockSpec((nbat, seq, gw), lambda b, g: (b, 0, xo + g)),
                  pl.BlockSpec((nbat, n, seq), lambda b, g: (b, bt_off + g, 0)),
                  pl.BlockSpec((nbat, seq, n), lambda b, g: (b, 0, co + g)),
                  heads_rows, heads_rows,
                  cw(gw, 0),
                  pl.BlockSpec((n, SSD_CONV), lambda b, g: (d_ssm // n + g, 0)),
                  cw(n, d_ssm // n + SSD_GROUPS),
                  cb(gw, 0),
                  pl.BlockSpec((n, 1), lambda b, g: (d_ssm // n + g, 0)),
                  cb(n, d_ssm // n + SSD_GROUPS),
                  cb(gw, 0), cb(gw, 0)],
        out_specs=pl.BlockSpec((nbat, seq, gw), lambda b, g: (b, 0, g)),
        out_shape=jax.ShapeDtypeStruct((bsz, seq, d_ssm), out_dtype),
        scratch_shapes=[pltpu.VMEM((nbat, r, seq, gw), BF16),
                        pltpu.VMEM((nbat, n, seq), BF16), pltpu.VMEM((nbat, seq, n), BF16),
                        pltpu.VMEM((nbat, 3, seq, gw), F32), pltpu.VMEM((nbat, 2, n, gw), F32)],
        compiler_params=_params("parallel", "parallel"),
        name="ssd_mixer",
    )(pm, pm, p_t, pm, cum_t, dt_t, conv_w, conv_w.T, conv_w, conv_b, conv_b.reshape(-1, 1), conv_b,
      dskip_cols, norm_w)


def _layer_norm(v, g, b):
    mu = jnp.mean(v, axis=-1, keepdims=True)
    c = v - mu
    var = jnp.mean(jnp.square(c), axis=-1, keepdims=True)
    return c * lax.rsqrt(var + LN_EPS) * g + b


def _merge_kernel(yh_ref, ys_ref, gh_ref, gs_ref, x_ref, whb_ref, wsb_ref, wo_ref, bo_ref,
                  lg_ref, lb_ref, rwt_ref, h_ref, hp_ref, lt_ref):
    a = jnp.dot(yh_ref[...], whb_ref[...], preferred_element_type=F32)
    b = jnp.dot(ys_ref[...], wsb_ref[...], preferred_element_type=F32)
    m = _sigmoid(gh_ref[...].astype(F32)) * a + _sigmoid(gs_ref[...].astype(F32)) * b
    mix = jnp.dot(m.astype(BF16), wo_ref[...], preferred_element_type=F32) + bo_ref[...]
    h = _layer_norm(DN_ALPHA * x_ref[...] + mix, lg_ref[...], lb_ref[...])
    h_ref[...] = h
    hp_ref[...] = _pack_halves(h)
    lt_ref[...] = lax.dot_general(rwt_ref[...], h, (((1,), (1,)), ((), ())), precision=HI,
                                  preferred_element_type=F32)


def _merge(yhy, yssd, pm, gate_off, x2d, whb, wsb, wo, bo, lg, lb, rwt, tm):
    t, d = x2d.shape
    c = yhy.shape[1]
    d_ssm = yssd.shape[1]
    e = rwt.shape[0]
    row = lambda w: pl.BlockSpec((tm, w), lambda i: (i, 0))
    whole = lambda a: pl.BlockSpec(a.shape, lambda i: (0,) * a.ndim)
    return pl.pallas_call(
        _merge_kernel,
        grid=(t // tm,),
        in_specs=[row(c), row(d_ssm),
                  pl.BlockSpec((tm, d), lambda i: (i, gate_off)),
                  pl.BlockSpec((tm, d), lambda i: (i, gate_off + 1)),
                  row(d), whole(whb), whole(wsb), whole(wo), whole(bo), whole(lg), whole(lb), whole(rwt)],
        out_specs=[row(d), row(d // 2), pl.BlockSpec((e, tm), lambda i: (0, i))],
        out_shape=[jax.ShapeDtypeStruct((t, d), F32), jax.ShapeDtypeStruct((t, d // 2), jnp.int32),
                   jax.ShapeDtypeStruct((e, t), F32)],
        compiler_params=_params("parallel"),
        name="merge_ln",
    )(yhy, yssd, pm, pm, x2d, whb, wsb, wo, bo, lg, lb, rwt)


def _first_max(v, axis, size):
    idx = lax.broadcasted_iota(jnp.int32, v.shape, axis)
    top = jnp.max(v, axis=axis, keepdims=True)
    first = jnp.min(jnp.where(v == top, idx, size), axis=axis, keepdims=True)
    return idx == first, top


def _route_kernel(lt_ref, bias_ref, w_ref, sel_ref):
    n_e, tn = lt_ref.shape
    per = n_e // N_EXPERT_GROUPS
    scores = jax.nn.sigmoid(lt_ref[...])
    biased = scores + bias_ref[...]
    b3 = biased.reshape(N_EXPERT_GROUPS, per, tn)
    hit, top1 = _first_max(b3, 1, per)
    top2 = jnp.max(jnp.where(hit, -jnp.inf, b3), axis=1, keepdims=True)
    cur = (top1 + top2).reshape(N_EXPERT_GROUPS, tn)
    grp = jnp.zeros(cur.shape, jnp.bool_)
    for _ in range(TOPK_GROUPS):
        hit, _ = _first_max(cur, 0, N_EXPERT_GROUPS)
        grp = jnp.logical_or(grp, hit)
        cur = jnp.where(hit, -jnp.inf, cur)
    cur = jnp.where(grp.reshape(N_EXPERT_GROUPS, 1, tn), b3, -jnp.inf).reshape(n_e, tn)
    chosen = jnp.zeros(cur.shape, jnp.bool_)
    for _ in range(TOP_K):
        hit, _ = _first_max(cur, 0, n_e)
        chosen = jnp.logical_or(chosen, hit)
        cur = jnp.where(hit, -jnp.inf, cur)
    w = jnp.where(chosen, scores, 0.0)
    w_ref[...] = w / jnp.sum(w, axis=0, keepdims=True) * ROUTED_SCALE
    sel_ref[...] = jnp.where(chosen, 1.0, 0.0).astype(sel_ref.dtype)


def _route(logits_t, bias_col, tn):
    e, t = logits_t.shape
    blk = pl.BlockSpec((e, tn), lambda i: (0, i))
    return pl.pallas_call(
        _route_kernel,
        grid=(t // tn,),
        in_specs=[blk, pl.BlockSpec((e, 1), lambda i: (0, 0))],
        out_specs=[blk, blk],
        out_shape=[jax.ShapeDtypeStruct((e, t), F32), jax.ShapeDtypeStruct((e, t), BF16)],
        compiler_params=_params("parallel"),
        name="route_topk",
    )(logits_t, bias_col)


MOE_BLOCK = 512
MOE_TOKEN_RANGES = 2


def _rank_kernel(sel_ref, pos_ref, cnt_ref, carry_ref):
    n_e, tn = sel_ref.shape

    @pl.when(pl.program_id(0) == 0)
    def _():
        carry_ref[...] = jnp.zeros_like(carry_ref)

    sel = sel_ref[...]
    s = lax.broadcasted_iota(jnp.int32, (tn, tn), 0)
    t = lax.broadcasted_iota(jnp.int32, (tn, tn), 1)
    earlier = jnp.where(s < t, 1.0, 0.0).astype(BF16)
    before = jnp.dot(sel, earlier, preferred_element_type=F32)
    carry = carry_ref[...]
    pos_ref[...] = carry[:, 0:1] + before
    carry = carry + jnp.sum(sel.astype(F32), axis=1, keepdims=True)
    carry_ref[...] = carry
    cnt_ref[...] = carry


def _rank(sel_t, tn):
    e, t = sel_t.shape
    return pl.pallas_call(
        _rank_kernel,
        grid=(t // tn,),
        in_specs=[pl.BlockSpec((e, tn), lambda i: (0, i))],
        out_specs=[pl.BlockSpec((e, tn), lambda i: (0, i)), pl.BlockSpec((e, 128), lambda i: (0, 0))],
        out_shape=[jax.ShapeDtypeStruct((e, t), F32), jax.ShapeDtypeStruct((e, 128), F32)],
        scratch_shapes=[pltpu.VMEM((e, 128), F32)],
        compiler_params=_params("arbitrary"),
        name="moe_rank",
    )(sel_t)


def _segment_starts(cnt):
    n_e = cnt.shape[0]
    padded = jnp.floor((cnt + (MOE_BLOCK - 1)) * (1.0 / MOE_BLOCK)) * MOE_BLOCK
    a = lax.broadcasted_iota(jnp.int32, (n_e, n_e), 0)
    b = lax.broadcasted_iota(jnp.int32, (n_e, n_e), 1)
    below = jnp.where(b < a, 1.0, 0.0).astype(F32)
    return padded, jnp.dot(below, padded, precision=HI, preferred_element_type=F32)


def _slots_kernel(sel_ref, w_ref, pos_ref, cnt_ref, slot_ref, wk_ref):
    n_e, tn = sel_ref.shape
    _, start = _segment_starts(cnt_ref[...])
    slot = start[:, 0:1] + pos_ref[...]
    w = w_ref[...]
    cur = sel_ref[...].astype(F32)
    idx = lax.broadcasted_iota(jnp.int32, (n_e, tn), 0)
    slots, weights = [], []
    for _ in range(TOP_K):
        first = jnp.min(jnp.where(cur > 0.0, idx, n_e), axis=0, keepdims=True)
        hit = idx == first
        slots.append(jnp.sum(jnp.where(hit, slot, 0.0), axis=0, keepdims=True))
        weights.append(jnp.sum(jnp.where(hit, w, 0.0), axis=0, keepdims=True))
        cur = jnp.where(hit, 0.0, cur)
    slot_ref[...] = jnp.concatenate(slots, axis=0).astype(jnp.int32)
    wk_ref[...] = jnp.concatenate(weights, axis=0).T


def _slots(sel_t, w_t, pos_t, cnt, tn):
    e, t = sel_t.shape
    blk = pl.BlockSpec((e, tn), lambda i: (0, i))
    return pl.pallas_call(
        _slots_kernel,
        grid=(t // tn,),
        in_specs=[blk, blk, blk, pl.BlockSpec((e, 128), lambda i: (0, 0))],
        out_specs=[pl.BlockSpec((TOP_K, tn), lambda i: (0, i)), pl.BlockSpec((tn, TOP_K), lambda i: (i, 0))],
        out_shape=[jax.ShapeDtypeStruct((TOP_K, t), jnp.int32), jax.ShapeDtypeStruct((t, TOP_K), F32)],
        compiler_params=_params("parallel"),
        name="moe_slots",
    )(sel_t, w_t, pos_t, cnt)


def _plan_kernel(cnt_ref, expert_ref, valid_ref):
    n_e = cnt_ref.shape[0]
    nbp = expert_ref.shape[1]
    cnt = cnt_ref[...]
    padded, start = _segment_starts(cnt)
    end = (start + padded)[:, 0:1]
    last = (start + cnt)[:, 0:1]
    row0 = (lax.broadcasted_iota(jnp.int32, (1, nbp), 1) * MOE_BLOCK).astype(F32)
    expert = jnp.minimum(jnp.sum(jnp.where(end <= row0, 1.0, 0.0), axis=0, keepdims=True), n_e - 1.0)
    eid = lax.broadcasted_iota(jnp.int32, (n_e, nbp), 0).astype(F32)
    last_b = jnp.sum(jnp.where(eid == expert, last, 0.0), axis=0, keepdims=True)
    expert_ref[...] = expert.astype(jnp.int32)
    valid_ref[...] = jnp.clip(last_b - row0, 0.0, float(MOE_BLOCK)).astype(jnp.int32)


def _plan(cnt, n_blocks):
    nbp = -(-n_blocks // 128) * 128
    return pl.pallas_call(
        _plan_kernel,
        out_shape=[jax.ShapeDtypeStruct((1, nbp), jnp.int32)] * 2,
        name="moe_plan",
    )(cnt)


V7X_SC_CORES = 2
V7X_SC_SUBCORES = 16
SC_WINDOW = 128


def _sc_mesh():
    return plsc.VectorSubcoreMesh(core_axis_name="c", subcore_axis_name="s",
                                  num_cores=V7X_SC_CORES, num_subcores=V7X_SC_SUBCORES)


def _sc_scatter_rows(rows, row0, idx, n_out):
    width = rows.shape[1]
    k_n, t = idx.shape
    workers = V7X_SC_CORES * V7X_SC_SUBCORES
    per_worker = t // workers
    assert per_worker * workers == t and per_worker % SC_WINDOW == 0

    def body(rows_hbm, idx_hbm, out_hbm, idx_v, rows_v, sem):
        wid = lax.axis_index("s") * V7X_SC_CORES + lax.axis_index("c")

        @pl.loop(0, per_worker // SC_WINDOW)
        def _(j):
            base = wid * per_worker + j * SC_WINDOW
            pltpu.sync_copy(rows_hbm.at[pl.ds(row0 + base, SC_WINDOW)], rows_v)
            for k in range(k_n):
                pltpu.sync_copy(idx_hbm.at[pl.ds(k * t + base, SC_WINDOW)], idx_v)
                pltpu.async_copy(rows_v, out_hbm.at[idx_v], sem).wait()

    return pl.kernel(
        body, out_type=jax.ShapeDtypeStruct((n_out, width), rows.dtype), mesh=_sc_mesh(),
        scratch_types=[pltpu.VMEM((SC_WINDOW,), jnp.int32), pltpu.VMEM((SC_WINDOW, width), rows.dtype),
                       pltpu.SemaphoreType.DMA],
        name="moe_dispatch",
    )(rows, idx.reshape(-1))


def _sc_gather_rows(table, idx):
    width = table.shape[1]
    k_n, t = idx.shape
    workers = V7X_SC_CORES * V7X_SC_SUBCORES
    per_worker = t // workers
    assert per_worker * workers == t and per_worker % SC_WINDOW == 0

    def body(table_hbm, idx_hbm, out_hbm, idx_v, rows_v, sem):
        wid = lax.axis_index("s") * V7X_SC_CORES + lax.axis_index("c")

        @pl.loop(0, per_worker // SC_WINDOW)
        def _(j):
            base = wid * per_worker + j * SC_WINDOW
            for k in range(k_n):
                pltpu.sync_copy(idx_hbm.at[pl.ds(k * t + base, SC_WINDOW)], idx_v)
                pltpu.async_copy(table_hbm.at[idx_v], rows_v, sem).wait()
                pltpu.sync_copy(rows_v, out_hbm.at[pl.ds(k * t + base, SC_WINDOW)])

    out = pl.kernel(
        body, out_type=jax.ShapeDtypeStruct((k_n * t, width), table.dtype), mesh=_sc_mesh(),
        scratch_types=[pltpu.VMEM((SC_WINDOW,), jnp.int32), pltpu.VMEM((SC_WINDOW, width), table.dtype),
                       pltpu.SemaphoreType.DMA],
        name="moe_collect",
    )(table, idx.reshape(-1))
    return out.reshape(k_n, t, width)


def _swiglu(xb, wg, wu):
    gate = jnp.dot(xb, wg, preferred_element_type=F32)
    up = jnp.dot(xb, wu, preferred_element_type=F32)
    return _silu(gate) * up


def _pack_halves(v):
    w = v.shape[1] // 2
    hi = pltpu.bitcast(v[:, :w].astype(BF16).astype(F32), jnp.int32)
    lo = pltpu.bitcast(v[:, w:].astype(BF16).astype(F32), jnp.int32)
    return jnp.bitwise_or(hi, lax.shift_right_logical(lo, 16))


def _unpack_halves(p):
    hi = pltpu.bitcast(jnp.bitwise_and(p, jnp.int32(-65536)), F32)
    lo = pltpu.bitcast(lax.shift_left(p, 16), F32)
    return jnp.concatenate([hi, lo], axis=1)


def _experts_kernel(expert_ref, valid_ref, x_ref, wg_ref, wu_ref, wd_ref, y_ref, wg_s, wu_s, wd_s):
    b = pl.program_id(0)
    valid = valid_ref[0, b]
    fresh = jnp.logical_or(b == 0, expert_ref[0, b] != expert_ref[0, jnp.maximum(b - 1, 0)])

    @pl.when(jnp.logical_and(valid > 0, fresh))
    def _():
        wg_s[...] = wg_ref[0].astype(BF16)
        wu_s[...] = wu_ref[0].astype(BF16)
        wd_s[...] = wd_ref[0].astype(BF16)

    @pl.when(valid > 0)
    def _():
        rows = lax.broadcasted_iota(jnp.int32, (x_ref.shape[0], 1), 0)
        x = jnp.where(rows < valid, _unpack_halves(x_ref[...]), 0.0).astype(BF16)
        mid = _swiglu(x, wg_s[...], wu_s[...]).astype(BF16)
        y_ref[...] = _pack_halves(jnp.dot(mid, wd_s[...], preferred_element_type=F32))

    @pl.when(valid == 0)
    def _():
        y_ref[...] = jnp.zeros_like(y_ref)


def _experts(x_sorted, expert, valid, wg, wu, wd):
    n_rows, half = x_sorted.shape
    n_e, d, f = wg.shape
    by_expert = lambda shape: pl.BlockSpec((1,) + shape, lambda b, e_ref, v_ref: (e_ref[0, b], 0, 0))
    rows = pl.BlockSpec((MOE_BLOCK, half), lambda b, e_ref, v_ref: (b, 0))
    return pl.pallas_call(
        _experts_kernel,
        grid_spec=pltpu.PrefetchScalarGridSpec(
            num_scalar_prefetch=2, grid=(n_rows // MOE_BLOCK,),
            in_specs=[rows, by_expert((d, f)), by_expert((d, f)), by_expert((f, d))],
            out_specs=rows,
            scratch_shapes=[pltpu.VMEM((d, f), BF16), pltpu.VMEM((d, f), BF16), pltpu.VMEM((f, d), BF16)]),
        out_shape=jax.ShapeDtypeStruct((n_rows, half), jnp.int32),
        compiler_params=_params("arbitrary"),
        name="moe_experts",
    )(expert, valid, x_sorted, wg, wu, wd)


def _combine_kernel(y_ref, wk_ref, hp_ref, h_ref, sg_ref, su_ref, sd_ref, lg_ref, lb_ref, o_ref):
    wk = wk_ref[...]
    mid = _swiglu(_unpack_halves(hp_ref[...]).astype(BF16), sg_ref[...], su_ref[...]).astype(BF16)
    acc = jnp.dot(mid, sd_ref[...], preferred_element_type=F32)
    for k in range(y_ref.shape[0]):
        acc = acc + wk[:, k:k + 1] * _unpack_halves(y_ref[k])
    o_ref[...] = _layer_norm(DN_ALPHA * h_ref[...] + acc, lg_ref[...], lb_ref[...])


def _combine_kernel_into(y_ref, wk_ref, hp_ref, h_ref, sg_ref, su_ref, sd_ref, lg_ref, lb_ref, prev_ref,
                         o_ref):
    del prev_ref
    _combine_kernel(y_ref, wk_ref, hp_ref, h_ref, sg_ref, su_ref, sd_ref, lg_ref, lb_ref, o_ref)


def _combine(y_tok, wk, hp, h, row0, prev, sg, su, sd, lg, lb, tm):
    t, d = h.shape
    k_n, t_part, half = y_tok.shape
    b0 = row0 // tm
    part = lambda w: pl.BlockSpec((tm, w), lambda i: (i, 0))
    full = lambda w: pl.BlockSpec((tm, w), lambda i: (b0 + i, 0))
    whole = lambda a: pl.BlockSpec(a.shape, lambda i: (0,) * a.ndim)
    in_specs = [pl.BlockSpec((k_n, tm, half), lambda i: (0, i, 0)), part(k_n), full(half), full(d),
                whole(sg), whole(su), whole(sd), whole(lg), whole(lb)]
    args = [y_tok, wk, hp, h, sg, su, sd, lg, lb]
    if prev is not None:
        in_specs.append(pl.BlockSpec(memory_space=pl.ANY))
        args.append(prev)
    return pl.pallas_call(
        _combine_kernel if prev is None else _combine_kernel_into,
        grid=(t_part // tm,),
        in_specs=in_specs,
        out_specs=full(d),
        out_shape=jax.ShapeDtypeStruct((t, d), F32),
        input_output_aliases={} if prev is None else {len(args) - 1: 0},
        compiler_params=_params("parallel"),
        name="moe_combine_ln",
    )(*args)


def _pick(n, *cands):
    for c in cands:
        if n % c == 0:
            return c
    return n


def _layer(h, w_in, b_in, hy_conv_w, hy_conv_b, hy_f_w1, hy_f_b1, hy_f_freq1, hy_f_w2, hy_f_b2,
           hy_f_freq2, hy_f_w3, hy_f_b3, hy_f_freq3, hy_f_w4, hy_bias, ssd_conv_w, ssd_conv_b,
           ssd_dt_bias, ssd_a_log, ssd_d, ssd_norm_w, w_hy_branch, w_ssd_branch, w_out, b_out,
           ln1_g, ln1_b, router_w, router_bias, exp_w_gate, exp_w_up, exp_w_down,
           sh_w_gate, sh_w_up, sh_w_down, ln2_g, ln2_b):
    bsz, seq, d = h.shape
    t = bsz * seq
    d_hy = hy_bias.shape[-1]
    d_ssm = ssd_norm_w.shape[-1]
    conv_dim = ssd_conv_w.shape[-1]
    heads = ssd_d.shape[-1]
    hy_cols = (HY_ORDER + 1) * d_hy
    o_z, o_dt, o_gate = hy_cols, hy_cols + d_ssm + conv_dim, hy_cols + d_ssm + conv_dim + 2 * heads
    o_b = o_z + 2 * d_ssm
    o_c = o_b + SSD_GROUPS * SSD_STATE
    row = lambda v: v.astype(F32).reshape(1, -1)
    col = lambda v: v.astype(F32).reshape(-1, 1)

    ch = lambda v: jnp.concatenate([v[..., :o_z], v[..., o_b:o_c]], axis=-1)
    w_ch, b_ch = ch(w_in).T.astype(BF16), col(ch(b_in))
    tmaj = lambda v: jnp.concatenate([v[..., o_z:o_b], v[..., o_c:o_dt], v[..., o_gate:]], axis=-1)
    w_main, b_main = tmaj(w_in).astype(BF16), row(tmaj(b_in))
    pad = 128 - 2 * heads
    hpg = heads // SSD_GROUPS
    perm = [dr * heads + g * hpg + hh for g in range(SSD_GROUPS) for dr in range(2) for hh in range(hpg)]
    by_group = lambda v: jnp.pad(v.astype(F32).reshape(-1)[jnp.array(perm)], (0, pad)).reshape(1, -1)
    w_dt = jnp.pad(w_in[:, o_dt:o_gate][:, jnp.array(perm)], ((0, 0), (0, pad))).astype(BF16)
    b_dt = by_group(b_in[o_dt:o_gate])
    p = _pick(seq, 512, 256, 128, 64)
    r = _pick(d_hy, 128, 64)
    spectra = _hyena_spectra(seq, p, hy_f_w1, hy_f_b1, hy_f_freq1, hy_f_w2, hy_f_b2, hy_f_freq2,
                             hy_f_w3, hy_f_b3, hy_f_freq3, hy_f_w4, d_hy, r, BF16)
    branch_w = (w_hy_branch.astype(BF16), w_ssd_branch.astype(BF16), w_out.astype(BF16),
                row(b_out), row(ln1_g), row(ln1_b), router_w.T.astype(F32))
    shared = (sh_w_gate.astype(BF16), sh_w_up.astype(BF16), sh_w_down.astype(BF16))
    experts_w = (exp_w_gate.astype(F32), exp_w_up.astype(F32), exp_w_down.astype(F32))
    x_res = h.reshape(t, d).astype(F32)

    xb = h.astype(BF16)
    x2d = xb.reshape(t, d)
    tm = _pick(t, 1024, 512, 256, 128)
    p_t = _matmul_channels_major(w_ch, xb, b_ch, BF16, _pick(hy_cols + o_c - o_b, 1024, 512, 384, 128))
    pm = _matmul_bias(x2d, w_main, b_main, BF16, tm, _pick(w_main.shape[1], 1024, 512, 256, 128))
    p_dt = _matmul_bias(x2d, w_dt, b_dt, F32, tm, 128)
    y_hy = _hyena_mixer(p_t, hy_conv_w.T.astype(F32), col(hy_conv_b), spectra, col(hy_bias),
                        d_hy, p, r, _pick(bsz, 2, 1), BF16)
    cum_t, dt_t = _ssd_prep(p_dt.reshape(bsz, seq, 128), by_group(ssd_dt_bias),
                            by_group(-jnp.exp(ssd_a_log.astype(F32))), hpg)
    y_ssd = _ssd_mixer(pm.reshape(bsz, seq, -1), p_t, hy_cols // SSD_STATE, cum_t, dt_t,
                       ssd_conv_w.astype(F32), row(ssd_conv_b),
                       row(jnp.repeat(ssd_d, SSD_HEADDIM)), row(ssd_norm_w), d_ssm, BF16)
    h1, h1p, logits_t = _merge(y_hy.reshape(t, d_hy), y_ssd.reshape(t, d_ssm), pm,
                               (o_dt - o_z - (o_c - o_b)) // d, x_res, *branch_w,
                               _pick(t, 512, 256, 128))

    sc_rows = V7X_SC_CORES * V7X_SC_SUBCORES * SC_WINDOW
    n_parts = MOE_TOKEN_RANGES if t % (MOE_TOKEN_RANGES * sc_rows) == 0 else 1
    tp = t // n_parts
    tn = _pick(tp, 512, 256, 128)
    w_t, sel_t = _route(logits_t, col(router_bias), tn)
    n_blocks = -(-tp * TOP_K // MOE_BLOCK) + router_w.shape[-1]
    out = None
    for part in range(n_parts):
        cols = slice(part * tp, (part + 1) * tp)
        sel_p, w_p = sel_t[:, cols], w_t[:, cols]
        pos_t, cnt = _rank(sel_p, tn)
        slot_kt, wk = _slots(sel_p, w_p, pos_t, cnt, tn)
        expert, valid = _plan(cnt, n_blocks)
        x_sorted = _sc_scatter_rows(h1p, part * tp, slot_kt, n_blocks * MOE_BLOCK)
        y_sorted = _experts(x_sorted, expert, valid, *experts_w)
        y_tok = _sc_gather_rows(y_sorted, slot_kt)
        out = _combine(y_tok, wk, h1p, h1, part * tp, out, *shared, row(ln2_g), row(ln2_b), tn)
    return out.reshape(bsz, seq, d)


def kernel(x, w_in, b_in, hy_conv_w, hy_conv_b, hy_f_w1, hy_f_b1, hy_f_freq1, hy_f_w2, hy_f_b2,
           hy_f_freq2, hy_f_w3, hy_f_b3, hy_f_freq3, hy_f_w4, hy_bias, ssd_conv_w, ssd_conv_b,
           ssd_dt_bias, ssd_a_log, ssd_d, ssd_norm_w, w_hy_branch, w_ssd_branch, w_out, b_out,
           ln1_g, ln1_b, router_w, router_bias, exp_w_gate, exp_w_up, exp_w_down,
           sh_w_gate, sh_w_up, sh_w_down, ln2_g, ln2_b):
    params = (w_in, b_in, hy_conv_w, hy_conv_b, hy_f_w1, hy_f_b1, hy_f_freq1, hy_f_w2, hy_f_b2,
              hy_f_freq2, hy_f_w3, hy_f_b3, hy_f_freq3, hy_f_w4, hy_bias, ssd_conv_w, ssd_conv_b,
              ssd_dt_bias, ssd_a_log, ssd_d, ssd_norm_w, w_hy_branch, w_ssd_branch, w_out, b_out,
              ln1_g, ln1_b, router_w, router_bias, exp_w_gate, exp_w_up, exp_w_down,
              sh_w_gate, sh_w_up, sh_w_down, ln2_g, ln2_b)
    h = x
    for i in range(DEPTH):
        h = _layer(h, *(p[i] for p in params))
    return h
```

```python
import functools
import math

import jax
import jax.numpy as jnp
from jax import lax
from jax.experimental import pallas as pl
from jax.experimental.pallas import tpu as pltpu
from jax.experimental.pallas import tpu_sc as plsc

F32 = jnp.float32
BF16 = jnp.bfloat16
HI = lax.Precision.HIGHEST

HY_ORDER = 2
HY_EMB_DIM = 33
HY_BANDS = (HY_EMB_DIM - 1) // 2
HY_FAST_DECAY_PCT = 0.3
HY_SLOW_DECAY_PCT = 1.5
HY_DECAY_TARGET = 1e-2
SSD_HEADDIM = 64
SSD_GROUPS = 8
SSD_STATE = 128
SSD_CONV = 5
SSD_CHUNK = 128
N_EXPERT_GROUPS = 8
TOPK_GROUPS = 4
TOP_K = 8
ROUTED_SCALE = 2.5
DEPTH = 1
DN_ALPHA = (2.0 * DEPTH) ** 0.25
LN_EPS = 1e-5
RMS_EPS = 1e-5

V7X_VMEM_LIMIT = 56 * 1024 * 1024


def _params(*sem):
    return pltpu.CompilerParams(dimension_semantics=sem, vmem_limit_bytes=V7X_VMEM_LIMIT)


def _sigmoid(v):
    return 0.5 + 0.5 * jnp.tanh(0.5 * v)


def _silu(v):
    hv = 0.5 * v
    return hv + hv * jnp.tanh(hv)


def _mm_bias_kernel(a_ref, b_ref, bias_ref, o_ref):
    acc = jnp.dot(a_ref[...].astype(b_ref.dtype), b_ref[...], preferred_element_type=F32)
    o_ref[...] = (acc + bias_ref[...]).astype(o_ref.dtype)


def _matmul_bias(a, b, bias, out_dtype, tm, tn):
    m, k = a.shape
    n = b.shape[1]
    return pl.pallas_call(
        _mm_bias_kernel,
        grid=(m // tm, n // tn),
        in_specs=[pl.BlockSpec((tm, k), lambda i, j: (i, 0)),
                  pl.BlockSpec((k, tn), lambda i, j: (0, j)),
                  pl.BlockSpec((1, tn), lambda i, j: (0, j))],
        out_specs=pl.BlockSpec((tm, tn), lambda i, j: (i, j)),
        out_shape=jax.ShapeDtypeStruct((m, n), out_dtype),
        compiler_params=_params("parallel", "parallel"),
        name="proj_rows",
    )(a, b, bias)


def _mm_nt_kernel(w_ref, x_ref, bias_ref, o_ref):
    acc = lax.dot_general(w_ref[...], x_ref[0].astype(w_ref.dtype), (((1,), (1,)), ((), ())),
                          preferred_element_type=F32)
    o_ref[0] = (acc + bias_ref[...]).astype(o_ref.dtype)


def _matmul_channels_major(w_t, x, bias_col, out_dtype, tc):
    c, k = w_t.shape
    bsz, seq, _ = x.shape
    return pl.pallas_call(
        _mm_nt_kernel,
        grid=(bsz, c // tc),
        in_specs=[pl.BlockSpec((tc, k), lambda b, i: (i, 0)),
                  pl.BlockSpec((1, seq, k), lambda b, i: (b, 0, 0)),
                  pl.BlockSpec((tc, 1), lambda b, i: (i, 0))],
        out_specs=pl.BlockSpec((1, tc, seq), lambda b, i: (b, i, 0)),
        out_shape=jax.ShapeDtypeStruct((bsz, c, seq), out_dtype),
        compiler_params=_params("parallel", "parallel"),
        name="proj_channels",
    )(w_t, x, bias_col)


def _hy_mlp_kernel(w1t_ref, b1_ref, f1_ref, w2t_ref, b2_ref, f2_ref, w3t_ref, b3_ref, f3_ref,
                   o_ref, *, seq, emb_rows):
    shape = (emb_rows, 2 * seq)
    lag = lax.broadcasted_iota(jnp.int32, shape, 1) - seq
    pos = jnp.abs(lag).astype(F32)
    row = lax.broadcasted_iota(jnp.int32, shape, 0)
    t = pos / (seq - 1)
    w = (2.0 * math.pi) * pos / seq
    band = jnp.where(row <= HY_BANDS, row - 1, row - 1 - HY_BANDS).astype(F32)
    freq = 1e-4 + band * ((HY_BANDS - 1 - 1e-4) / (HY_BANDS - 1))
    ang = freq * w
    z = jnp.where(row == 0, t,
                  jnp.where(row <= HY_BANDS, jnp.cos(ang),
                            jnp.where(row <= 2 * HY_BANDS, -jnp.sin(ang), 0.0)))
    h = jnp.sin(f1_ref[...] * (jnp.dot(w1t_ref[...], z, precision=HI, preferred_element_type=F32)
                               + b1_ref[...]))
    h = jnp.sin(f2_ref[...] * (jnp.dot(w2t_ref[...], h, precision=HI, preferred_element_type=F32)
                               + b2_ref[...]))
    h = jnp.sin(f3_ref[...] * (jnp.dot(w3t_ref[...], h, precision=HI, preferred_element_type=F32)
                               + b3_ref[...]))
    o_ref[...] = h


def _dft_matrices(p):
    e = lax.broadcasted_iota(jnp.int32, (p, 2 * p), 0)
    j = lax.broadcasted_iota(jnp.int32, (p, 2 * p), 1)
    k = jnp.where(j < p, j, j - p)
    theta = ((e * k) % (2 * p)).astype(F32) * (math.pi / p)
    sign_e = jnp.where(e % 2 == 0, 1.0, -1.0)
    fwd = jnp.where(j < p, jnp.cos(theta), jnp.where(j == p, sign_e, -jnp.sin(theta)))
    jj = lax.broadcasted_iota(jnp.int32, (2 * p, p), 0)
    tt = lax.broadcasted_iota(jnp.int32, (2 * p, p), 1)
    kk = jnp.where(jj < p, jj, jj - p)
    th = ((tt * kk) % (2 * p)).astype(F32) * (math.pi / p)
    sign_t = jnp.where(tt % 2 == 0, 1.0, -1.0)
    n = 2.0 * p
    inv = jnp.where(jj == 0, 1.0 / n,
                    jnp.where(jj < p, (2.0 / n) * jnp.cos(th),
                              jnp.where(jj == p, sign_t / n, (-2.0 / n) * jnp.sin(th))))
    return fwd.astype(F32), inv.astype(F32)


def _hy_spec_kernel(h_ref, w4f_ref, w4b_ref, fwd_ref, o_ref, *, seq, p, d_hy, rows):
    nb = seq // p
    h = h_ref[...]
    kpos = jnp.dot(w4f_ref[...], h[:, seq:], precision=HI, preferred_element_type=F32)
    kneg = jnp.dot(w4b_ref[...], h[:, :seq], precision=HI, preferred_element_type=F32)
    kb0 = jnp.dot(w4b_ref[...], h[:, seq:seq + 128], precision=HI, preferred_element_type=F32)
    lane128 = lax.broadcasted_iota(jnp.int32, kb0.shape, 1)
    kb0 = jnp.sum(jnp.where(lane128 == 0, kb0, 0.0), axis=1, keepdims=True)

    r0 = pl.program_id(0) * rows
    ch = (r0 + lax.broadcasted_iota(jnp.int32, (rows, 1), 0)) % d_hy
    max_decay = math.log(HY_DECAY_TARGET) / HY_FAST_DECAY_PCT
    min_decay = math.log(HY_DECAY_TARGET) / HY_SLOW_DECAY_PCT
    delta = jnp.abs(min_decay + ch.astype(F32) * ((max_decay - min_decay) / (d_hy - 1)))
    i = lax.broadcasted_iota(jnp.int32, (1, seq), 1)
    t_pos = i.astype(F32) / (seq - 1)
    t_neg = (seq - i).astype(F32) / (seq - 1)
    kpos = (kpos + jnp.where(i == 0, kb0, 0.0)) * jnp.exp(-t_pos * delta)
    kneg = jnp.where(i == 0, 0.0, kneg * jnp.exp(-t_neg * delta))
    klag = jnp.concatenate([kneg, kpos], axis=1)

    fwd = fwd_ref[...]
    lane = lax.broadcasted_iota(jnp.int32, (1, 2 * p), 1)
    sign = jnp.where(lane % 2 == 0, 1.0, -1.0)
    spec = []
    first = []
    for m in range(2 * nb):
        blk = klag[:, m * p:(m + 1) * p]
        spec.append(jnp.dot(blk, fwd, precision=HI, preferred_element_type=F32))
        first.append(blk[:, 0:1])
    half = lax.broadcasted_iota(jnp.int32, (1, p), 1)
    for m in range(1, 2 * nb):
        tail = spec[m - 1] - jnp.where(lane <= p, first[m - 1], 0.0)
        full = spec[m] + sign * tail
        re, im = full[:, :p], full[:, p:]
        o_ref[m - 1, 0] = re.astype(o_ref.dtype)
        o_ref[m - 1, 1] = jnp.where(half == 0, 0.0, im).astype(o_ref.dtype)
        o_ref[m - 1, 2] = jnp.where(half == 0, im, re).astype(o_ref.dtype)


def _hyena_spectra(seq, p, w1, b1, fr1, w2, b2, fr2, w3, b3, fr3, w4, d_hy, rows, spec_dtype):
    hidden = w1.shape[1]
    emb_rows = -(-HY_EMB_DIM // 8) * 8
    w1t = jnp.zeros((hidden, emb_rows), F32).at[:, :HY_EMB_DIM].set(w1.T.astype(F32))
    col = lambda v: v.astype(F32).reshape(hidden, 1)
    h = pl.pallas_call(
        functools.partial(_hy_mlp_kernel, seq=seq, emb_rows=emb_rows),
        out_shape=jax.ShapeDtypeStruct((hidden, 2 * seq), F32),
        name="hyena_filter_mlp",
    )(w1t, col(b1), col(fr1), w2.T.astype(F32), col(b2), col(fr2), w3.T.astype(F32), col(b3), col(fr3))
    w4r = w4.astype(F32).reshape(hidden, HY_ORDER, 2, d_hy)
    w4f = w4r[:, :, 0].reshape(hidden, HY_ORDER * d_hy).T
    w4b = w4r[:, :, 1].reshape(hidden, HY_ORDER * d_hy).T
    fwd, _ = _dft_matrices(p)
    nb = seq // p
    n_rows = HY_ORDER * d_hy
    return pl.pallas_call(
        functools.partial(_hy_spec_kernel, seq=seq, p=p, d_hy=d_hy, rows=rows),
        grid=(n_rows // rows,),
        in_specs=[pl.BlockSpec((hidden, 2 * seq), lambda r: (0, 0)),
                  pl.BlockSpec((rows, hidden), lambda r: (r, 0)),
                  pl.BlockSpec((rows, hidden), lambda r: (r, 0)),
                  pl.BlockSpec((p, 2 * p), lambda r: (0, 0))],
        out_specs=pl.BlockSpec((2 * nb - 1, 3, rows, p), lambda r: (0, 0, r, 0)),
        out_shape=jax.ShapeDtypeStruct((2 * nb - 1, 3, n_rows, p), spec_dtype),
        compiler_params=_params("parallel"),
        name="hyena_filter_spectra",
    )(h, w4f, w4b, fwd)


def _hy_conv_kernel(g1_ref, g2_ref, v_ref, w1_ref, w2_ref, wv_ref, b1_ref, b2_ref, bv_ref,
                    k0_ref, k1_ref, s0_ref, s1_ref, fwd_ref, inv_ref, o_ref, *, p):
    bt, r, seq = v_ref.shape
    nb = seq // p
    rows = bt * r
    lane = lax.broadcasted_iota(jnp.int32, (1, 1, seq), 2)

    def short_conv(x_ref, w_ref, b_ref):
        x = x_ref[...].astype(F32)
        x2 = x.reshape(rows, seq)
        prev = jnp.where(lane == 0, 0.0, pltpu.roll(x2, 1, axis=1).reshape(bt, r, seq))
        nxt = jnp.where(lane == seq - 1, 0.0, pltpu.roll(x2, seq - 1, axis=1).reshape(bt, r, seq))
        w = w_ref[...]
        return b_ref[...] + w[:, 0:1] * prev + w[:, 1:2] * x + w[:, 2:3] * nxt

    fwd = fwd_ref[...]
    inv = inv_ref[...]

    def long_conv(u, k_ref):
        ub = u.astype(BF16).reshape(rows, seq)
        spec = [jnp.dot(ub[:, j * p:(j + 1) * p], fwd, preferred_element_type=F32).reshape(bt, r, 2 * p)
                for j in range(nb)]
        outs = []
        for i in range(nb):
            acc_re = jnp.zeros((bt, r, p), F32)
            acc_im = jnp.zeros((bt, r, p), F32)
            for j in range(nb):
                m = i - j + nb - 1
                ka = k_ref[m, 0].astype(F32)
                kb = k_ref[m, 1].astype(F32)
                kc = k_ref[m, 2].astype(F32)
                u_re = spec[j][:, :, :p]
                u_im = spec[j][:, :, p:]
                acc_re = acc_re + ka * u_re - kb * u_im
                acc_im = acc_im + kc * u_im + kb * u_re
            y_spec = jnp.concatenate([acc_re, acc_im], axis=2).astype(BF16).reshape(rows, 2 * p)
            outs.append(jnp.dot(y_spec, inv, preferred_element_type=F32).reshape(bt, r, p))
        return jnp.concatenate(outs, axis=2)

    v = short_conv(v_ref, wv_ref, bv_ref)
    z = short_conv(g1_ref, w1_ref, b1_ref) * (long_conv(v, k0_ref) + v * s0_ref[...])
    y = short_conv(g2_ref, w2_ref, b2_ref) * (long_conv(z, k1_ref) + z * s1_ref[...])
    for b in range(bt):
        o_ref[b] = y[b].T.astype(o_ref.dtype)


def _hyena_mixer(p_t, conv_w_t, conv_b_col, spectra, skip_col, d_hy, p, r, bt, out_dtype):
    bsz, _, seq = p_t.shape
    nb = seq // p
    nblk = d_hy // r
    fwd, inv = _dft_matrices(p)
    x_spec = lambda off: pl.BlockSpec((bt, r, seq), lambda j, b: (b, off * nblk + j, 0))
    w_spec = lambda off: pl.BlockSpec((r, conv_w_t.shape[1]), lambda j, b: (off * nblk + j, 0))
    c_spec = lambda off: pl.BlockSpec((r, 1), lambda j, b: (off * nblk + j, 0))
    k_spec = lambda off: pl.BlockSpec((2 * nb - 1, 3, r, p), lambda j, b: (0, 0, off * nblk + j, 0))
    return pl.pallas_call(
        functools.partial(_hy_conv_kernel, p=p),
        grid=(nblk, bsz // bt),
        in_specs=[x_spec(0), x_spec(1), x_spec(2), w_spec(0), w_spec(1), w_spec(2),
                  c_spec(0), c_spec(1), c_spec(2), k_spec(0), k_spec(1), c_spec(0), c_spec(1),
                  pl.BlockSpec((p, 2 * p), lambda j, b: (0, 0)),
                  pl.BlockSpec((2 * p, p), lambda j, b: (0, 0))],
        out_specs=pl.BlockSpec((bt, seq, r), lambda j, b: (b, 0, j)),
        out_shape=jax.ShapeDtypeStruct((bsz, seq, d_hy), out_dtype),
        compiler_params=_params("parallel", "arbitrary"),
        name="hyena_mixer",
    )(p_t, p_t, p_t, conv_w_t, conv_w_t, conv_w_t, conv_b_col, conv_b_col, conv_b_col,
      spectra, spectra, skip_col, skip_col, fwd.astype(BF16), inv.astype(BF16))


def _ssd_prep_kernel(dt_ref, dtb_ref, a_ref, cum_ref, dtt_ref, *, r):
    seq = dt_ref.shape[1]
    q = SSD_CHUNK
    raw = dt_ref[0] + dtb_ref[...]
    dt = jnp.maximum(raw, 0.0) + jnp.log1p(jnp.exp(-jnp.abs(raw)))
    a = dt * a_ref[...]
    li = lax.broadcasted_iota(jnp.int32, (q, q), 0)
    si = lax.broadcasted_iota(jnp.int32, (q, q), 1)
    lower = jnp.where(li >= si, 1.0, 0.0).astype(F32)
    upper = jnp.where(li <= si, 1.0, 0.0).astype(F32)
    fwd_col = lax.broadcasted_iota(jnp.int32, (1, 128), 1) % (2 * r) < r
    for c in range(seq // q):
        ac = a[c * q:(c + 1) * q]
        cum = jnp.where(fwd_col,
                        jnp.dot(lower, ac, precision=HI, preferred_element_type=F32),
                        jnp.dot(upper, ac, precision=HI, preferred_element_type=F32))
        cum_ref[0, :, c * q:(c + 1) * q] = cum.T
        dtt_ref[0, :, c * q:(c + 1) * q] = dt[c * q:(c + 1) * q].T


def _ssd_prep(p_dt, dt_bias, a_neg, r):
    bsz, seq, _ = p_dt.shape
    vec = pl.BlockSpec((1, 128), lambda b: (0, 0))
    out = pl.BlockSpec((1, 128, seq), lambda b: (b, 0, 0))
    return pl.pallas_call(
        functools.partial(_ssd_prep_kernel, r=r),
        grid=(bsz,),
        in_specs=[pl.BlockSpec((1, seq, 128), lambda b: (b, 0, 0)), vec, vec],
        out_specs=[out, out],
        out_shape=[jax.ShapeDtypeStruct((bsz, 128, seq), F32)] * 2,
        compiler_params=_params("parallel"),
        name="ssd_prep",
    )(p_dt, dt_bias, a_neg)


def _ssd_kernel(z_ref, x_ref, bt_ref, c_ref, cum_ref, dtt_ref, wx_ref, wbt_ref, wc_ref,
                bx_ref, bbt_ref, bc_ref, dskip_ref, nw_ref, o_ref, xm_s, bt_s, cs_s, y_s, st_s):
    nbat, seq, gw = x_ref.shape
    n = bt_ref.shape[1]
    r = gw // SSD_HEADDIM
    q = SSD_CHUNK
    nc = seq // q
    half = SSD_CONV // 2
    shifts = [k - half for k in range(SSD_CONV) if k != half]
    halo_r, halo_l = 16, 128

    k_r = 2 * q
    sr = lax.broadcasted_iota(jnp.int32, (len(shifts) * q, k_r), 0)
    se = lax.broadcasted_iota(jnp.int32, (len(shifts) * q, k_r), 1)
    band_r = jnp.zeros(sr.shape, jnp.bool_)
    for j, sh in enumerate(shifts):
        band_r = jnp.logical_or(band_r, jnp.logical_and(sr // q == j, se == halo_r + sr % q + sh))
    band_r = jnp.where(band_r, 1.0, 0.0).astype(BF16)
    k_l = q + 2 * halo_l
    le = lax.broadcasted_iota(jnp.int32, (k_l, len(shifts) * q), 0)
    lc = lax.broadcasted_iota(jnp.int32, (k_l, len(shifts) * q), 1)
    band_l = jnp.zeros(le.shape, jnp.bool_)
    for j, sh in enumerate(shifts):
        band_l = jnp.logical_or(band_l, jnp.logical_and(lc // q == j, le == halo_l + lc % q + sh))
    band_l = jnp.where(band_l, 1.0, 0.0).astype(BF16)

    col_head = lax.broadcasted_iota(jnp.int32, (1, gw), 1) // SSD_HEADDIM
    wx, wc, wbt = wx_ref[...], wc_ref[...], wbt_ref[...]

    def conv_rows(ref, bb, c, off, w, bias):
        width = ref.shape[2]
        cur = ref[bb, pl.ds(off, q), :]
        prev = ref[bb, pl.ds(pl.multiple_of(jnp.maximum(off - halo_r, 0), halo_r), halo_r), :]
        nxt = ref[bb, pl.ds(pl.multiple_of(jnp.minimum(off + q, seq - halo_r), halo_r), halo_r), :]
        prev = jnp.where(c > 0, prev, jnp.zeros_like(prev))
        nxt = jnp.where(c < nc - 1, nxt, jnp.zeros_like(nxt))
        ext = jnp.concatenate([prev, cur, nxt, jnp.zeros((k_r - q - 2 * halo_r, width), BF16)], axis=0)
        moved = jnp.dot(band_r, ext, preferred_element_type=F32)
        acc = bias + w[half:half + 1] * cur.astype(F32)
        for j, sh in enumerate(shifts):
            acc = acc + w[half + sh:half + sh + 1] * moved[j * q:(j + 1) * q]
        return _silu(acc)

    def conv_one(bb, c, off):
        xs = conv_rows(x_ref, bb, c, off, wx, bx_ref[...])
        y_s[bb, 0, pl.ds(off, q), :] = xs * dskip_ref[...]
        for h in range(r):
            xm_s[bb, h, pl.ds(off, q), :] = jnp.where(col_head == h, xs, 0.0).astype(BF16)
        cs_s[bb, pl.ds(off, q), :] = conv_rows(c_ref, bb, c, off, wc, bc_ref[...]).astype(BF16)
        cur = bt_ref[bb, :, pl.ds(off, q)]
        prev = bt_ref[bb, :, pl.ds(pl.multiple_of(jnp.maximum(off - halo_l, 0), halo_l), halo_l)]
        nxt = bt_ref[bb, :, pl.ds(pl.multiple_of(jnp.minimum(off + q, seq - halo_l), halo_l), halo_l)]
        prev = jnp.where(c > 0, prev, jnp.zeros_like(prev))
        nxt = jnp.where(c < nc - 1, nxt, jnp.zeros_like(nxt))
        moved = jnp.dot(jnp.concatenate([prev, cur, nxt], axis=1), band_l, preferred_element_type=F32)
        acc = bbt_ref[...] + wbt[:, half:half + 1] * cur.astype(F32)
        for j, sh in enumerate(shifts):
            acc = acc + wbt[:, half + sh:half + sh + 1] * moved[:, j * q:(j + 1) * q]
        bt_s[bb, :, pl.ds(off, q)] = _silu(acc).astype(BF16)

    def conv_step(c, carry):
        off = pl.multiple_of(c * q, q)
        for bb in range(nbat):
            conv_one(bb, c, off)
        return carry

    lax.fori_loop(0, nc, conv_step, 0, unroll=4)
    st_s[...] = jnp.zeros_like(st_s)

    li = lax.broadcasted_iota(jnp.int32, (q, q), 0)
    si = lax.broadcasted_iota(jnp.int32, (q, q), 1)
    diag = li == si
    sub = lax.broadcasted_iota(jnp.int32, (2 * r, 1), 0)
    neg_inf = jnp.float32(-jnp.inf)
    nh = 2 * r
    assert 3 * nh + 16 <= q
    ones_rows = jnp.ones((8, q), F32)
    pad_rows = jnp.zeros((q - 3 * nh - 16, q), F32)
    pk = lax.broadcasted_iota(jnp.int32, (3 * nh + 8, nh * q), 0)
    pj = lax.broadcasted_iota(jnp.int32, (3 * nh + 8, nh * q), 1) // q
    seg_top = jnp.where(jnp.logical_and(pk < 3 * nh, pk % nh == pj), 1.0, 0.0).astype(BF16)
    seg_bottom = jnp.zeros((q - 3 * nh - 8 - 16, nh * q), BF16)

    def split3(v):
        hi = v.astype(BF16).astype(F32)
        mid = (v - hi).astype(BF16).astype(F32)
        lo = (v - hi - mid).astype(BF16).astype(F32)
        return hi, mid, lo

    def flat(v):
        return jnp.concatenate([v[j:j + 1, :] for j in range(nh)], axis=1)

    def chunk_step(i, carry):
        offs = (pl.multiple_of(i * q, q), pl.multiple_of((nc - 1 - i) * q, q))
        chains = [(bb, d) for bb in range(nbat) for d in range(2)]
        pre = []
        for bb in range(nbat):
            cum = jnp.where(sub < r, cum_ref[bb, :, pl.ds(offs[0], q)], cum_ref[bb, :, pl.ds(offs[1], q)])
            dt = jnp.where(sub < r, dtt_ref[bb, :, pl.ds(offs[0], q)], dtt_ref[bb, :, pl.ds(offs[1], q)])
            total = jnp.where(sub < r, cum[:, q - 1:q], cum[:, 0:1])
            hi, mid, lo = split3(cum)
            left = jnp.concatenate([hi, mid, lo, jnp.zeros((8, q), F32), ones_rows, pad_rows], axis=0)
            left = left.T.astype(BF16)
            minus = jnp.concatenate([-flat(hi), -flat(mid), -flat(lo), jnp.zeros((13, nh * q), F32)],
                                    axis=0)
            right = jnp.concatenate([seg_top, minus.astype(BF16), seg_bottom], axis=0)
            pre.append(dict(dt=dt, carry_decay=jnp.exp(total), in_decay=jnp.exp(cum),
                            out_w=jnp.exp(total - cum) * dt, left=left, right=right))
        seg_all = [jnp.dot(p_["left"], p_["right"], preferred_element_type=F32) for p_ in pre]
        btc = {c: bt_s[c[0], :, pl.ds(offs[c[1]], q)] for c in chains}
        cc = {c: cs_s[c[0], pl.ds(offs[c[1]], q), :] for c in chains}
        state = {c: st_s[c[0], c[1]] for c in chains}
        cb = {c: jnp.dot(cc[c], btc[c], preferred_element_type=F32) for c in chains}
        c_state = {c: jnp.dot(cc[c], state[c].astype(BF16), preferred_element_type=F32) for c in chains}
        lhs_y, rhs_y, lhs_s, rhs_s, decays = {}, {}, {}, {}, {}
        for c in chains:
            bb, d = c
            p_ = pre[bb]
            tri = (li >= si) if d == 0 else (li <= si)
            btf = btc[c].astype(F32)
            mats, diags, xs_h, cs_h, b_scaled = [], [], [], [], []
            state_decay = jnp.zeros((1, gw), F32)
            for h in range(r):
                ln = d * r + h
                seg = seg_all[bb][:, ln * q:(ln + 1) * q]
                mats.append((cb[c] * jnp.exp(jnp.where(tri, seg, neg_inf)) * p_["dt"][ln:ln + 1, :])
                            .astype(BF16))
                diags.append(jnp.where(diag, p_["in_decay"][ln:ln + 1, :], 0.0).astype(BF16))
                xs_h.append(xm_s[bb, h, pl.ds(offs[d], q), :])
                cs_h.append(jnp.where(col_head == h, c_state[c], 0.0).astype(BF16))
                b_scaled.append((btf * p_["out_w"][ln:ln + 1, :]).astype(BF16))
                state_decay = jnp.where(col_head == h, p_["carry_decay"][ln:ln + 1, :], state_decay)
            lhs_y[c] = jnp.concatenate(mats + diags, axis=1)
            rhs_y[c] = jnp.concatenate(xs_h + cs_h, axis=0)
            lhs_s[c] = jnp.concatenate(b_scaled, axis=1)
            rhs_s[c] = jnp.concatenate(xs_h, axis=0)
            decays[c] = state_decay
        y_new = {c: jnp.dot(lhs_y[c], rhs_y[c], preferred_element_type=F32) for c in chains}
        upd = {c: jnp.dot(lhs_s[c], rhs_s[c], preferred_element_type=F32) for c in chains}
        for c in chains:
            bb, d = c
            y_s[bb, 1 + d, pl.ds(offs[d], q), :] = y_new[c]
            st_s[bb, d] = state[c] * decays[c] + upd[c]
        return carry

    lax.fori_loop(0, nc, chunk_step, 0, unroll=4)

    def gate_step(c, carry):
        off = pl.multiple_of(c * q, q)
        for bb in range(nbat):
            zf = z_ref[bb, pl.ds(off, q), :].astype(F32)
            y = (y_s[bb, 0, pl.ds(off, q), :] + y_s[bb, 1, pl.ds(off, q), :]
                 + y_s[bb, 2, pl.ds(off, q), :])
            y = y * _silu(zf)
            y = y * lax.rsqrt(jnp.mean(jnp.square(y), axis=-1, keepdims=True) + RMS_EPS)
            o_ref[bb, pl.ds(off, q), :] = (y * nw_ref[...]).astype(o_ref.dtype)
        return carry

    lax.fori_loop(0, nc, gate_step, 0, unroll=4)


def _ssd_mixer(pm, p_t, bt_off, cum_t, dt_t, conv_w, conv_b, dskip_cols, norm_w, d_ssm, out_dtype):
    bsz, seq, _ = pm.shape
    gw = d_ssm // SSD_GROUPS
    r = gw // SSD_HEADDIM
    n = SSD_STATE
    xo, co = d_ssm // gw, 2 * d_ssm // n
    cw = lambda width, off: pl.BlockSpec((SSD_CONV, width), lambda b, g: (0, off + g))
    cb = lambda width, off: pl.BlockSpec((1, width), lambda b, g: (0, off + g))
    nbat = _pick(bsz, 2, 1)
    heads_rows = pl.BlockSpec((nbat, 2 * r, seq), lambda b, g: (b, g, 0))
    return pl.pallas_call(
        _ssd_kernel,
        grid=(bsz // nbat, SSD_GROUPS),
        in_specs=[pl.BlockSpec((nbat, seq, gw), lambda b, g: (b, 0, g)),
                  pl.BlockSpec((nbat, seq, gw), lambda b, g: (b, 0, xo + g)),
                  pl.BlockSpec((nbat, n, seq), lambda b, g: (b, bt_off + g, 0)),
                  pl.BlockSpec((nbat, seq, n), lambda b, g: (b, 0, co + g)),
                  heads_rows, heads_rows,
                  cw(gw, 0),
                  pl.BlockSpec((n, SSD_CONV), lambda b, g: (d_ssm // n + g, 0)),
                  cw(n, d_ssm // n + SSD_GROUPS),
                  cb(gw, 0),
                  pl.BlockSpec((n, 1), lambda b, g: (d_ssm // n + g, 0)),
                  cb(n, d_ssm // n + SSD_GROUPS),
                  cb(gw, 0), cb(gw, 0)],
        out_specs=pl.BlockSpec((nbat, seq, gw), lambda b, g: (b, 0, g)),
        out_shape=jax.ShapeDtypeStruct((bsz, seq, d_ssm), out_dtype),
        scratch_shapes=[pltpu.VMEM((nbat, r, seq, gw), BF16),
                        pltpu.VMEM((nbat, n, seq), BF16), pltpu.VMEM((nbat, seq, n), BF16),
                        pltpu.VMEM((nbat, 3, seq, gw), F32), pltpu.VMEM((nbat, 2, n, gw), F32)],
        compiler_params=_params("parallel", "parallel"),
        name="ssd_mixer",
    )(pm, pm, p_t, pm, cum_t, dt_t, conv_w, conv_w.T, conv_w, conv_b, conv_b.reshape(-1, 1), conv_b,
      dskip_cols, norm_w)


def _layer_norm(v, g, b):
    mu = jnp.mean(v, axis=-1, keepdims=True)
    c = v - mu
    var = jnp.mean(jnp.square(c), axis=-1, keepdims=True)
    return c * lax.rsqrt(var + LN_EPS) * g + b


def _merge_kernel(yh_ref, ys_ref, gh_ref, gs_ref, x_ref, whb_ref, wsb_ref, wo_ref, bo_ref,
                  lg_ref, lb_ref, rwt_ref, h_ref, hp_ref, lt_ref):
    a = jnp.dot(yh_ref[...], whb_ref[...], preferred_element_type=F32)
    b = jnp.dot(ys_ref[...], wsb_ref[...], preferred_element_type=F32)
    m = _sigmoid(gh_ref[...].astype(F32)) * a + _sigmoid(gs_ref[...].astype(F32)) * b
    mix = jnp.dot(m.astype(BF16), wo_ref[...], preferred_element_type=F32) + bo_ref[...]
    h = _layer_norm(DN_ALPHA * x_ref[...] + mix, lg_ref[...], lb_ref[...])
    h_ref[...] = h
    hp_ref[...] = _pack_halves(h)
    lt_ref[...] = lax.dot_general(rwt_ref[...], h, (((1,), (1,)), ((), ())), precision=HI,
                                  preferred_element_type=F32)


def _merge(yhy, yssd, pm, gate_off, x2d, whb, wsb, wo, bo, lg, lb, rwt, tm):
    t, d = x2d.shape
    c = yhy.shape[1]
    d_ssm = yssd.shape[1]
    e = rwt.shape[0]
    row = lambda w: pl.BlockSpec((tm, w), lambda i: (i, 0))
    whole = lambda a: pl.BlockSpec(a.shape, lambda i: (0,) * a.ndim)
    return pl.pallas_call(
        _merge_kernel,
        grid=(t // tm,),
        in_specs=[row(c), row(d_ssm),
                  pl.BlockSpec((tm, d), lambda i: (i, gate_off)),
                  pl.BlockSpec((tm, d), lambda i: (i, gate_off + 1)),
                  row(d), whole(whb), whole(wsb), whole(wo), whole(bo), whole(lg), whole(lb), whole(rwt)],
        out_specs=[row(d), row(d // 2), pl.BlockSpec((e, tm), lambda i: (0, i))],
        out_shape=[jax.ShapeDtypeStruct((t, d), F32), jax.ShapeDtypeStruct((t, d // 2), jnp.int32),
                   jax.ShapeDtypeStruct((e, t), F32)],
        compiler_params=_params("parallel"),
        name="merge_ln",
    )(yhy, yssd, pm, pm, x2d, whb, wsb, wo, bo, lg, lb, rwt)


def _first_max(v, axis, size):
    idx = lax.broadcasted_iota(jnp.int32, v.shape, axis)
    top = jnp.max(v, axis=axis, keepdims=True)
    first = jnp.min(jnp.where(v == top, idx, size), axis=axis, keepdims=True)
    return idx == first, top


def _route_kernel(lt_ref, bias_ref, w_ref, sel_ref):
    n_e, tn = lt_ref.shape
    per = n_e // N_EXPERT_GROUPS
    scores = jax.nn.sigmoid(lt_ref[...])
    biased = scores + bias_ref[...]
    b3 = biased.reshape(N_EXPERT_GROUPS, per, tn)
    hit, top1 = _first_max(b3, 1, per)
    top2 = jnp.max(jnp.where(hit, -jnp.inf, b3), axis=1, keepdims=True)
    cur = (top1 + top2).reshape(N_EXPERT_GROUPS, tn)
    grp = jnp.zeros(cur.shape, jnp.bool_)
    for _ in range(TOPK_GROUPS):
        hit, _ = _first_max(cur, 0, N_EXPERT_GROUPS)
        grp = jnp.logical_or(grp, hit)
        cur = jnp.where(hit, -jnp.inf, cur)
    cur = jnp.where(grp.reshape(N_EXPERT_GROUPS, 1, tn), b3, -jnp.inf).reshape(n_e, tn)
    chosen = jnp.zeros(cur.shape, jnp.bool_)
    for _ in range(TOP_K):
        hit, _ = _first_max(cur, 0, n_e)
        chosen = jnp.logical_or(chosen, hit)
        cur = jnp.where(hit, -jnp.inf, cur)
    w = jnp.where(chosen, scores, 0.0)
    w_ref[...] = w / jnp.sum(w, axis=0, keepdims=True) * ROUTED_SCALE
    sel_ref[...] = jnp.where(chosen, 1.0, 0.0).astype(sel_ref.dtype)


def _route(logits_t, bias_col, tn):
    e, t = logits_t.shape
    blk = pl.BlockSpec((e, tn), lambda i: (0, i))
    return pl.pallas_call(
        _route_kernel,
        grid=(t // tn,),
        in_specs=[blk, pl.BlockSpec((e, 1), lambda i: (0, 0))],
        out_specs=[blk, blk],
        out_shape=[jax.ShapeDtypeStruct((e, t), F32), jax.ShapeDtypeStruct((e, t), BF16)],
        compiler_params=_params("parallel"),
        name="route_topk",
    )(logits_t, bias_col)


MOE_BLOCK = 512
MOE_TOKEN_RANGES = 2


def _rank_kernel(sel_ref, pos_ref, cnt_ref, carry_ref):
    n_e, tn = sel_ref.shape

    @pl.when(pl.program_id(0) == 0)
    def _():
        carry_ref[...] = jnp.zeros_like(carry_ref)

    sel = sel_ref[...]
    s = lax.broadcasted_iota(jnp.int32, (tn, tn), 0)
    t = lax.broadcasted_iota(jnp.int32, (tn, tn), 1)
    earlier = jnp.where(s < t, 1.0, 0.0).astype(BF16)
    before = jnp.dot(sel, earlier, preferred_element_type=F32)
    carry = carry_ref[...]
    pos_ref[...] = carry[:, 0:1] + before
    carry = carry + jnp.sum(sel.astype(F32), axis=1, keepdims=True)
    carry_ref[...] = carry
    cnt_ref[...] = carry


def _rank(sel_t, tn):
    e, t = sel_t.shape
    return pl.pallas_call(
        _rank_kernel,
        grid=(t // tn,),
        in_specs=[pl.BlockSpec((e, tn), lambda i: (0, i))],
        out_specs=[pl.BlockSpec((e, tn), lambda i: (0, i)), pl.BlockSpec((e, 128), lambda i: (0, 0))],
        out_shape=[jax.ShapeDtypeStruct((e, t), F32), jax.ShapeDtypeStruct((e, 128), F32)],
        scratch_shapes=[pltpu.VMEM((e, 128), F32)],
        compiler_params=_params("arbitrary"),
        name="moe_rank",
    )(sel_t)


def _segment_starts(cnt):
    n_e = cnt.shape[0]
    padded = jnp.floor((cnt + (MOE_BLOCK - 1)) * (1.0 / MOE_BLOCK)) * MOE_BLOCK
    a = lax.broadcasted_iota(jnp.int32, (n_e, n_e), 0)
    b = lax.broadcasted_iota(jnp.int32, (n_e, n_e), 1)
    below = jnp.where(b < a, 1.0, 0.0).astype(F32)
    return padded, jnp.dot(below, padded, precision=HI, preferred_element_type=F32)


def _slots_kernel(sel_ref, w_ref, pos_ref, cnt_ref, slot_ref, wk_ref):
    n_e, tn = sel_ref.shape
    _, start = _segment_starts(cnt_ref[...])
    slot = start[:, 0:1] + pos_ref[...]
    w = w_ref[...]
    cur = sel_ref[...].astype(F32)
    idx = lax.broadcasted_iota(jnp.int32, (n_e, tn), 0)
    slots, weights = [], []
    for _ in range(TOP_K):
        first = jnp.min(jnp.where(cur > 0.0, idx, n_e), axis=0, keepdims=True)
        hit = idx == first
        slots.append(jnp.sum(jnp.where(hit, slot, 0.0), axis=0, keepdims=True))
        weights.append(jnp.sum(jnp.where(hit, w, 0.0), axis=0, keepdims=True))
        cur = jnp.where(hit, 0.0, cur)
    slot_ref[...] = jnp.concatenate(slots, axis=0).astype(jnp.int32)
    wk_ref[...] = jnp.concatenate(weights, axis=0).T


def _slots(sel_t, w_t, pos_t, cnt, tn):
    e, t = sel_t.shape
    blk = pl.BlockSpec((e, tn), lambda i: (0, i))
    return pl.pallas_call(
        _slots_kernel,
        grid=(t // tn,),
        in_specs=[blk, blk, blk, pl.BlockSpec((e, 128), lambda i: (0, 0))],
        out_specs=[pl.BlockSpec((TOP_K, tn), lambda i: (0, i)), pl.BlockSpec((tn, TOP_K), lambda i: (i, 0))],
        out_shape=[jax.ShapeDtypeStruct((TOP_K, t), jnp.int32), jax.ShapeDtypeStruct((t, TOP_K), F32)],
        compiler_params=_params("parallel"),
        name="moe_slots",
    )(sel_t, w_t, pos_t, cnt)


def _plan_kernel(cnt_ref, expert_ref, valid_ref):
    n_e = cnt_ref.shape[0]
    nbp = expert_ref.shape[1]
    cnt = cnt_ref[...]
    padded, start = _segment_starts(cnt)
    end = (start + padded)[:, 0:1]
    last = (start + cnt)[:, 0:1]
    row0 = (lax.broadcasted_iota(jnp.int32, (1, nbp), 1) * MOE_BLOCK).astype(F32)
    expert = jnp.minimum(jnp.sum(jnp.where(end <= row0, 1.0, 0.0), axis=0, keepdims=True), n_e - 1.0)
    eid = lax.broadcasted_iota(jnp.int32, (n_e, nbp), 0).astype(F32)
    last_b = jnp.sum(jnp.where(eid == expert, last, 0.0), axis=0, keepdims=True)
    expert_ref[...] = expert.astype(jnp.int32)
    valid_ref[...] = jnp.clip(last_b - row0, 0.0, float(MOE_BLOCK)).astype(jnp.int32)


def _plan(cnt, n_blocks):
    nbp = -(-n_blocks // 128) * 128
    return pl.pallas_call(
        _plan_kernel,
        out_shape=[jax.ShapeDtypeStruct((1, nbp), jnp.int32)] * 2,
        name="moe_plan",
    )(cnt)


V7X_SC_CORES = 2
V7X_SC_SUBCORES = 16
SC_WINDOW = 128


def _sc_mesh():
    return plsc.VectorSubcoreMesh(core_axis_name="c", subcore_axis_name="s",
                                  num_cores=V7X_SC_CORES, num_subcores=V7X_SC_SUBCORES)


def _sc_scatter_rows(rows, row0, idx, n_out):
    width = rows.shape[1]
    k_n, t = idx.shape
    workers = V7X_SC_CORES * V7X_SC_SUBCORES
    per_worker = t // workers
    assert per_worker * workers == t and per_worker % SC_WINDOW == 0

    def body(rows_hbm, idx_hbm, out_hbm, idx_v, rows_v, sem):
        wid = lax.axis_index("s") * V7X_SC_CORES + lax.axis_index("c")

        @pl.loop(0, per_worker // SC_WINDOW)
        def _(j):
            base = wid * per_worker + j * SC_WINDOW
            pltpu.sync_copy(rows_hbm.at[pl.ds(row0 + base, SC_WINDOW)], rows_v)
            for k in range(k_n):
                pltpu.sync_copy(idx_hbm.at[pl.ds(k * t + base, SC_WINDOW)], idx_v.at[k])
            copies = [pltpu.async_copy(rows_v, out_hbm.at[idx_v.at[k]], sem) for k in range(k_n)]
            for copy in copies:
                copy.wait()

    return pl.kernel(
        body, out_type=jax.ShapeDtypeStruct((n_out, width), rows.dtype), mesh=_sc_mesh(),
        scratch_types=[pltpu.VMEM((k_n, SC_WINDOW), jnp.int32), pltpu.VMEM((SC_WINDOW, width), rows.dtype),
                       pltpu.SemaphoreType.DMA],
        name="moe_dispatch",
    )(rows, idx.reshape(-1))


def _sc_gather_rows(table, idx):
    width = table.shape[1]
    k_n, t = idx.shape
    workers = V7X_SC_CORES * V7X_SC_SUBCORES
    per_worker = t // workers
    assert per_worker * workers == t and per_worker % SC_WINDOW == 0

    def body(table_hbm, idx_hbm, out_hbm, idx_v, rows_v, sem):
        wid = lax.axis_index("s") * V7X_SC_CORES + lax.axis_index("c")

        @pl.loop(0, per_worker // SC_WINDOW)
        def _(j):
            base = wid * per_worker + j * SC_WINDOW
            for k in range(k_n):
                pltpu.sync_copy(idx_hbm.at[pl.ds(k * t + base, SC_WINDOW)], idx_v)
                pltpu.async_copy(table_hbm.at[idx_v], rows_v, sem).wait()
                pltpu.sync_copy(rows_v, out_hbm.at[pl.ds(k * t + base, SC_WINDOW)])

    out = pl.kernel(
        body, out_type=jax.ShapeDtypeStruct((k_n * t, width), table.dtype), mesh=_sc_mesh(),
        scratch_types=[pltpu.VMEM((SC_WINDOW,), jnp.int32), pltpu.VMEM((SC_WINDOW, width), table.dtype),
                       pltpu.SemaphoreType.DMA],
        name="moe_collect",
    )(table, idx.reshape(-1))
    return out.reshape(k_n, t, width)


def _swiglu(xb, wg, wu):
    gate = jnp.dot(xb, wg, preferred_element_type=F32)
    up = jnp.dot(xb, wu, preferred_element_type=F32)
    return _silu(gate) * up


def _pack_halves(v):
    w = v.shape[1] // 2
    hi = pltpu.bitcast(v[:, :w].astype(BF16).astype(F32), jnp.int32)
    lo = pltpu.bitcast(v[:, w:].astype(BF16).astype(F32), jnp.int32)
    return jnp.bitwise_or(hi, lax.shift_right_logical(lo, 16))


def _unpack_halves(p):
    hi = pltpu.bitcast(jnp.bitwise_and(p, jnp.int32(-65536)), F32)
    lo = pltpu.bitcast(lax.shift_left(p, 16), F32)
    return jnp.concatenate([hi, lo], axis=1)


def _experts_kernel(expert_ref, valid_ref, x_ref, wg_ref, wu_ref, wd_ref, y_ref, wg_s, wu_s, wd_s):
    b = pl.program_id(0)
    valid = valid_ref[0, b]
    fresh = jnp.logical_or(b == 0, expert_ref[0, b] != expert_ref[0, jnp.maximum(b - 1, 0)])

    @pl.when(jnp.logical_and(valid > 0, fresh))
    def _():
        wg_s[...] = wg_ref[0].astype(BF16)
        wu_s[...] = wu_ref[0].astype(BF16)
        wd_s[...] = wd_ref[0].astype(BF16)

    @pl.when(valid > 0)
    def _():
        rows = lax.broadcasted_iota(jnp.int32, (x_ref.shape[0], 1), 0)
        x = jnp.where(rows < valid, _unpack_halves(x_ref[...]), 0.0).astype(BF16)
        mid = _swiglu(x, wg_s[...], wu_s[...]).astype(BF16)
        y_ref[...] = _pack_halves(jnp.dot(mid, wd_s[...], preferred_element_type=F32))

    @pl.when(valid == 0)
    def _():
        y_ref[...] = jnp.zeros_like(y_ref)


def _experts(x_sorted, expert, valid, wg, wu, wd):
    n_rows, half = x_sorted.shape
    n_e, d, f = wg.shape
    by_expert = lambda shape: pl.BlockSpec((1,) + shape, lambda b, e_ref, v_ref: (e_ref[0, b], 0, 0))
    rows = pl.BlockSpec((MOE_BLOCK, half), lambda b, e_ref, v_ref: (b, 0))
    return pl.pallas_call(
        _experts_kernel,
        grid_spec=pltpu.PrefetchScalarGridSpec(
            num_scalar_prefetch=2, grid=(n_rows // MOE_BLOCK,),
            in_specs=[rows, by_expert((d, f)), by_expert((d, f)), by_expert((f, d))],
            out_specs=rows,
            scratch_shapes=[pltpu.VMEM((d, f), BF16), pltpu.VMEM((d, f), BF16), pltpu.VMEM((f, d), BF16)]),
        out_shape=jax.ShapeDtypeStruct((n_rows, half), jnp.int32),
        compiler_params=_params("arbitrary"),
        name="moe_experts",
    )(expert, valid, x_sorted, wg, wu, wd)


def _combine_kernel(y_ref, wk_ref, hp_ref, h_ref, sg_ref, su_ref, sd_ref, lg_ref, lb_ref, o_ref):
    wk = wk_ref[...]
    mid = _swiglu(_unpack_halves(hp_ref[...]).astype(BF16), sg_ref[...], su_ref[...]).astype(BF16)
    acc = jnp.dot(mid, sd_ref[...], preferred_element_type=F32)
    for k in range(y_ref.shape[0]):
        acc = acc + wk[:, k:k + 1] * _unpack_halves(y_ref[k])
    o_ref[...] = _layer_norm(DN_ALPHA * h_ref[...] + acc, lg_ref[...], lb_ref[...])


def _combine_kernel_into(y_ref, wk_ref, hp_ref, h_ref, sg_ref, su_ref, sd_ref, lg_ref, lb_ref, prev_ref,
                         o_ref):
    del prev_ref
    _combine_kernel(y_ref, wk_ref, hp_ref, h_ref, sg_ref, su_ref, sd_ref, lg_ref, lb_ref, o_ref)


def _combine(y_tok, wk, hp, h, row0, prev, sg, su, sd, lg, lb, tm):
    t, d = h.shape
    k_n, t_part, half = y_tok.shape
    b0 = row0 // tm
    part = lambda w: pl.BlockSpec((tm, w), lambda i: (i, 0))
    full = lambda w: pl.BlockSpec((tm, w), lambda i: (b0 + i, 0))
    whole = lambda a: pl.BlockSpec(a.shape, lambda i: (0,) * a.ndim)
    in_specs = [pl.BlockSpec((k_n, tm, half), lambda i: (0, i, 0)), part(k_n), full(half), full(d),
                whole(sg), whole(su), whole(sd), whole(lg), whole(lb)]
    args = [y_tok, wk, hp, h, sg, su, sd, lg, lb]
    if prev is not None:
        in_specs.append(pl.BlockSpec(memory_space=pl.ANY))
        args.append(prev)
    return pl.pallas_call(
        _combine_kernel if prev is None else _combine_kernel_into,
        grid=(t_part // tm,),
        in_specs=in_specs,
        out_specs=full(d),
        out_shape=jax.ShapeDtypeStruct((t, d), F32),
        input_output_aliases={} if prev is None else {len(args) - 1: 0},
        compiler_params=_params("parallel"),
        name="moe_combine_ln",
    )(*args)


def _pick(n, *cands):
    for c in cands:
        if n % c == 0:
            return c
    return n


def _layer(h, w_in, b_in, hy_conv_w, hy_conv_b, hy_f_w1, hy_f_b1, hy_f_freq1, hy_f_w2, hy_f_b2,
           hy_f_freq2, hy_f_w3, hy_f_b3, hy_f_freq3, hy_f_w4, hy_bias, ssd_conv_w, ssd_conv_b,
           ssd_dt_bias, ssd_a_log, ssd_d, ssd_norm_w, w_hy_branch, w_ssd_branch, w_out, b_out,
           ln1_g, ln1_b, router_w, router_bias, exp_w_gate, exp_w_up, exp_w_down,
           sh_w_gate, sh_w_up, sh_w_down, ln2_g, ln2_b):
    bsz, seq, d = h.shape
    t = bsz * seq
    d_hy = hy_bias.shape[-1]
    d_ssm = ssd_norm_w.shape[-1]
    conv_dim = ssd_conv_w.shape[-1]
    heads = ssd_d.shape[-1]
    hy_cols = (HY_ORDER + 1) * d_hy
    o_z, o_dt, o_gate = hy_cols, hy_cols + d_ssm + conv_dim, hy_cols + d_ssm + conv_dim + 2 * heads
    o_b = o_z + 2 * d_ssm
    o_c = o_b + SSD_GROUPS * SSD_STATE
    row = lambda v: v.astype(F32).reshape(1, -1)
    col = lambda v: v.astype(F32).reshape(-1, 1)

    ch = lambda v: jnp.concatenate([v[..., :o_z], v[..., o_b:o_c]], axis=-1)
    w_ch, b_ch = ch(w_in).T.astype(BF16), col(ch(b_in))
    tmaj = lambda v: jnp.concatenate([v[..., o_z:o_b], v[..., o_c:o_dt], v[..., o_gate:]], axis=-1)
    w_main, b_main = tmaj(w_in).astype(BF16), row(tmaj(b_in))
    pad = 128 - 2 * heads
    hpg = heads // SSD_GROUPS
    perm = [dr * heads + g * hpg + hh for g in range(SSD_GROUPS) for dr in range(2) for hh in range(hpg)]
    by_group = lambda v: jnp.pad(v.astype(F32).reshape(-1)[jnp.array(perm)], (0, pad)).reshape(1, -1)
    w_dt = jnp.pad(w_in[:, o_dt:o_gate][:, jnp.array(perm)], ((0, 0), (0, pad))).astype(BF16)
    b_dt = by_group(b_in[o_dt:o_gate])
    p = _pick(seq, 512, 256, 128, 64)
    r = _pick(d_hy, 128, 64)
    spectra = _hyena_spectra(seq, p, hy_f_w1, hy_f_b1, hy_f_freq1, hy_f_w2, hy_f_b2, hy_f_freq2,
                             hy_f_w3, hy_f_b3, hy_f_freq3, hy_f_w4, d_hy, r, BF16)
    branch_w = (w_hy_branch.astype(BF16), w_ssd_branch.astype(BF16), w_out.astype(BF16),
                row(b_out), row(ln1_g), row(ln1_b), router_w.T.astype(F32))
    shared = (sh_w_gate.astype(BF16), sh_w_up.astype(BF16), sh_w_down.astype(BF16))
    experts_w = (exp_w_gate.astype(F32), exp_w_up.astype(F32), exp_w_down.astype(F32))
    x_res = h.reshape(t, d).astype(F32)

    xb = h.astype(F32)
    x2d = x_res
    tm = _pick(t, 1024, 512, 256, 128)
    p_t = _matmul_channels_major(w_ch, xb, b_ch, BF16, _pick(hy_cols + o_c - o_b, 1024, 512, 384, 128))
    pm = _matmul_bias(x2d, w_main, b_main, BF16, tm, _pick(w_main.shape[1], 1024, 512, 256, 128))
    p_dt = _matmul_bias(x2d, w_dt, b_dt, F32, tm, 128)
    y_hy = _hyena_mixer(p_t, hy_conv_w.T.astype(F32), col(hy_conv_b), spectra, col(hy_bias),
                        d_hy, p, r, _pick(bsz, 2, 1), BF16)
    cum_t, dt_t = _ssd_prep(p_dt.reshape(bsz, seq, 128), by_group(ssd_dt_bias),
                            by_group(-jnp.exp(ssd_a_log.astype(F32))), hpg)
    y_ssd = _ssd_mixer(pm.reshape(bsz, seq, -1), p_t, hy_cols // SSD_STATE, cum_t, dt_t,
                       ssd_conv_w.astype(F32), row(ssd_conv_b),
                       row(jnp.repeat(ssd_d, SSD_HEADDIM)), row(ssd_norm_w), d_ssm, BF16)
    h1, h1p, logits_t = _merge(y_hy.reshape(t, d_hy), y_ssd.reshape(t, d_ssm), pm,
                               (o_dt - o_z - (o_c - o_b)) // d, x_res, *branch_w,
                               _pick(t, 512, 256, 128))

    sc_rows = V7X_SC_CORES * V7X_SC_SUBCORES * SC_WINDOW
    n_parts = MOE_TOKEN_RANGES if t % (MOE_TOKEN_RANGES * sc_rows) == 0 else 1
    tp = t // n_parts
    tn = _pick(tp, 512, 256, 128)
    w_t, sel_t = _route(logits_t, col(router_bias), tn)
    n_blocks = -(-tp * TOP_K // MOE_BLOCK) + router_w.shape[-1]
    out = None
    for part in range(n_parts):
        cols = slice(part * tp, (part + 1) * tp)
        sel_p, w_p = sel_t[:, cols], w_t[:, cols]
        pos_t, cnt = _rank(sel_p, tn)
        slot_kt, wk = _slots(sel_p, w_p, pos_t, cnt, tn)
        expert, valid = _plan(cnt, n_blocks)
        x_sorted = _sc_scatter_rows(h1p, part * tp, slot_kt, n_blocks * MOE_BLOCK)
        y_sorted = _experts(x_sorted, expert, valid, *experts_w)
        y_tok = _sc_gather_rows(y_sorted, slot_kt)
        out = _combine(y_tok, wk, h1p, h1, part * tp, out, *shared, row(ln2_g), row(ln2_b), tn)
    return out.reshape(bsz, seq, d)


def kernel(x, w_in, b_in, hy_conv_w, hy_conv_b, hy_f_w1, hy_f_b1, hy_f_freq1, hy_f_w2, hy_f_b2,
           hy_f_freq2, hy_f_w3, hy_f_b3, hy_f_freq3, hy_f_w4, hy_bias, ssd_conv_w, ssd_conv_b,
           ssd_dt_bias, ssd_a_log, ssd_d, ssd_norm_w, w_hy_branch, w_ssd_branch, w_out, b_out,
           ln1_g, ln1_b, router_w, router_bias, exp_w_gate, exp_w_up, exp_w_down,
           sh_w_gate, sh_w_up, sh_w_down, ln2_g, ln2_b):
    params = (w_in, b_in, hy_conv_w, hy_conv_b, hy_f_w1, hy_f_b1, hy_f_freq1, hy_f_w2, hy_f_b2,
              hy_f_freq2, hy_f_w3, hy_f_b3, hy_f_freq3, hy_f_w4, hy_bias, ssd_conv_w, ssd_conv_b,
              ssd_dt_bias, ssd_a_log, ssd_d, ssd_norm_w, w_hy_branch, w_ssd_branch, w_out, b_out,
              ln1_g, ln1_b, router_w, router_bias, exp_w_gate, exp_w_up, exp_w_down,
              sh_w_gate, sh_w_up, sh_w_down, ln2_g, ln2_b)
    h = x
    for i in range(DEPTH):
        h = _layer(h, *(p[i] for p in params))
    return h
```

```python
import functools
import math

import jax
import jax.numpy as jnp
from jax import lax
from jax.experimental import pallas as pl
from jax.experimental.pallas import tpu as pltpu
from jax.experimental.pallas import tpu_sc as plsc

F32 = jnp.float32
BF16 = jnp.bfloat16
HI = lax.Precision.HIGHEST

HY_ORDER = 2
HY_EMB_DIM = 33
HY_BANDS = (HY_EMB_DIM - 1) // 2
HY_FAST_DECAY_PCT = 0.3
HY_SLOW_DECAY_PCT = 1.5
HY_DECAY_TARGET = 1e-2
SSD_HEADDIM = 64
SSD_GROUPS = 8
SSD_STATE = 128
SSD_CONV = 5
SSD_CHUNK = 128
N_EXPERT_GROUPS = 8
TOPK_GROUPS = 4
TOP_K = 8
ROUTED_SCALE = 2.5
DEPTH = 1
DN_ALPHA = (2.0 * DEPTH) ** 0.25
LN_EPS = 1e-5
RMS_EPS = 1e-5

V7X_VMEM_LIMIT = 56 * 1024 * 1024


def _params(*sem):
    return pltpu.CompilerParams(dimension_semantics=sem, vmem_limit_bytes=V7X_VMEM_LIMIT)


def _sigmoid(v):
    return 0.5 + 0.5 * jnp.tanh(0.5 * v)


def _silu(v):
    hv = 0.5 * v
    return hv + hv * jnp.tanh(hv)


def _mm_bias_kernel(a_ref, b_ref, bias_ref, o_ref):
    acc = jnp.dot(a_ref[...].astype(b_ref.dtype), b_ref[...], preferred_element_type=F32)
    o_ref[...] = (acc + bias_ref[...]).astype(o_ref.dtype)


def _matmul_bias(a, b, bias, out_dtype, tm, tn):
    m, k = a.shape
    n = b.shape[1]
    return pl.pallas_call(
        _mm_bias_kernel,
        grid=(m // tm, n // tn),
        in_specs=[pl.BlockSpec((tm, k), lambda i, j: (i, 0)),
                  pl.BlockSpec((k, tn), lambda i, j: (0, j)),
                  pl.BlockSpec((1, tn), lambda i, j: (0, j))],
        out_specs=pl.BlockSpec((tm, tn), lambda i, j: (i, j)),
        out_shape=jax.ShapeDtypeStruct((m, n), out_dtype),
        compiler_params=_params("parallel", "parallel"),
        name="proj_rows",
    )(a, b, bias)


def _mm_nt_kernel(w_ref, x_ref, bias_ref, o_ref):
    acc = lax.dot_general(w_ref[...], x_ref[0].astype(w_ref.dtype), (((1,), (1,)), ((), ())),
                          preferred_element_type=F32)
    o_ref[0] = (acc + bias_ref[...]).astype(o_ref.dtype)


def _matmul_channels_major(w_t, x, bias_col, out_dtype, tc):
    c, k = w_t.shape
    bsz, seq, _ = x.shape
    return pl.pallas_call(
        _mm_nt_kernel,
        grid=(bsz, c // tc),
        in_specs=[pl.BlockSpec((tc, k), lambda b, i: (i, 0)),
                  pl.BlockSpec((1, seq, k), lambda b, i: (b, 0, 0)),
                  pl.BlockSpec((tc, 1), lambda b, i: (i, 0))],
        out_specs=pl.BlockSpec((1, tc, seq), lambda b, i: (b, i, 0)),
        out_shape=jax.ShapeDtypeStruct((bsz, c, seq), out_dtype),
        compiler_params=_params("parallel", "parallel"),
        name="proj_channels",
    )(w_t, x, bias_col)


def _hy_mlp_kernel(w1t_ref, b1_ref, f1_ref, w2t_ref, b2_ref, f2_ref, w3t_ref, b3_ref, f3_ref,
                   o_ref, *, seq, emb_rows):
    shape = (emb_rows, 2 * seq)
    lag = lax.broadcasted_iota(jnp.int32, shape, 1) - seq
    pos = jnp.abs(lag).astype(F32)
    row = lax.broadcasted_iota(jnp.int32, shape, 0)
    t = pos / (seq - 1)
    w = (2.0 * math.pi) * pos / seq
    band = jnp.where(row <= HY_BANDS, row - 1, row - 1 - HY_BANDS).astype(F32)
    freq = 1e-4 + band * ((HY_BANDS - 1 - 1e-4) / (HY_BANDS - 1))
    ang = freq * w
    z = jnp.where(row == 0, t,
                  jnp.where(row <= HY_BANDS, jnp.cos(ang),
                            jnp.where(row <= 2 * HY_BANDS, -jnp.sin(ang), 0.0)))
    h = jnp.sin(f1_ref[...] * (jnp.dot(w1t_ref[...], z, precision=HI, preferred_element_type=F32)
                               + b1_ref[...]))
    h = jnp.sin(f2_ref[...] * (jnp.dot(w2t_ref[...], h, precision=HI, preferred_element_type=F32)
                               + b2_ref[...]))
    h = jnp.sin(f3_ref[...] * (jnp.dot(w3t_ref[...], h, precision=HI, preferred_element_type=F32)
                               + b3_ref[...]))
    o_ref[...] = h


def _dft_matrices(p):
    e = lax.broadcasted_iota(jnp.int32, (p, 2 * p), 0)
    j = lax.broadcasted_iota(jnp.int32, (p, 2 * p), 1)
    k = jnp.where(j < p, j, j - p)
    theta = ((e * k) % (2 * p)).astype(F32) * (math.pi / p)
    sign_e = jnp.where(e % 2 == 0, 1.0, -1.0)
    fwd = jnp.where(j < p, jnp.cos(theta), jnp.where(j == p, sign_e, -jnp.sin(theta)))
    jj = lax.broadcasted_iota(jnp.int32, (2 * p, p), 0)
    tt = lax.broadcasted_iota(jnp.int32, (2 * p, p), 1)
    kk = jnp.where(jj < p, jj, jj - p)
    th = ((tt * kk) % (2 * p)).astype(F32) * (math.pi / p)
    sign_t = jnp.where(tt % 2 == 0, 1.0, -1.0)
    n = 2.0 * p
    inv = jnp.where(jj == 0, 1.0 / n,
                    jnp.where(jj < p, (2.0 / n) * jnp.cos(th),
                              jnp.where(jj == p, sign_t / n, (-2.0 / n) * jnp.sin(th))))
    return fwd.astype(F32), inv.astype(F32)


def _hy_spec_kernel(h_ref, w4f_ref, w4b_ref, fwd_hi_ref, fwd_lo_ref, o_ref, *, seq, p, d_hy, rows):
    nb = seq // p
    h = h_ref[...]
    kpos = jnp.dot(w4f_ref[...], h[:, seq:], precision=HI, preferred_element_type=F32)
    kneg = jnp.dot(w4b_ref[...], h[:, :seq], precision=HI, preferred_element_type=F32)
    kb0 = jnp.dot(w4b_ref[...], h[:, seq:seq + 128], precision=HI, preferred_element_type=F32)
    lane128 = lax.broadcasted_iota(jnp.int32, kb0.shape, 1)
    kb0 = jnp.sum(jnp.where(lane128 == 0, kb0, 0.0), axis=1, keepdims=True)

    r0 = pl.program_id(0) * rows
    ch = (r0 + lax.broadcasted_iota(jnp.int32, (rows, 1), 0)) % d_hy
    max_decay = math.log(HY_DECAY_TARGET) / HY_FAST_DECAY_PCT
    min_decay = math.log(HY_DECAY_TARGET) / HY_SLOW_DECAY_PCT
    delta = jnp.abs(min_decay + ch.astype(F32) * ((max_decay - min_decay) / (d_hy - 1)))
    i = lax.broadcasted_iota(jnp.int32, (1, seq), 1)
    t_pos = i.astype(F32) / (seq - 1)
    t_neg = (seq - i).astype(F32) / (seq - 1)
    kpos = (kpos + jnp.where(i == 0, kb0, 0.0)) * jnp.exp(-t_pos * delta)
    kneg = jnp.where(i == 0, 0.0, kneg * jnp.exp(-t_neg * delta))
    klag = jnp.concatenate([kneg, kpos], axis=1)

    fwd_hi = fwd_hi_ref[...]
    fwd_lo = fwd_lo_ref[...]
    lane = lax.broadcasted_iota(jnp.int32, (1, 2 * p), 1)
    sign = jnp.where(lane % 2 == 0, 1.0, -1.0)
    spec = []
    first = []
    for m in range(2 * nb):
        blk = klag[:, m * p:(m + 1) * p]
        b_hi = blk.astype(BF16)
        b_lo = (blk - b_hi.astype(F32)).astype(BF16)
        spec.append(jnp.dot(b_hi, fwd_hi, preferred_element_type=F32)
                    + jnp.dot(b_lo, fwd_hi, preferred_element_type=F32)
                    + jnp.dot(b_hi, fwd_lo, preferred_element_type=F32))
        first.append(blk[:, 0:1])
    half = lax.broadcasted_iota(jnp.int32, (1, p), 1)
    for m in range(1, 2 * nb):
        tail = spec[m - 1] - jnp.where(lane <= p, first[m - 1], 0.0)
        full = spec[m] + sign * tail
        re, im = full[:, :p], full[:, p:]
        o_ref[m - 1, 0] = re.astype(o_ref.dtype)
        o_ref[m - 1, 1] = jnp.where(half == 0, 0.0, im).astype(o_ref.dtype)
        o_ref[m - 1, 2] = jnp.where(half == 0, im, re).astype(o_ref.dtype)


def _hyena_spectra(seq, p, w1, b1, fr1, w2, b2, fr2, w3, b3, fr3, w4, d_hy, rows, spec_dtype):
    hidden = w1.shape[1]
    emb_rows = -(-HY_EMB_DIM // 8) * 8
    w1t = jnp.zeros((hidden, emb_rows), F32).at[:, :HY_EMB_DIM].set(w1.T.astype(F32))
    col = lambda v: v.astype(F32).reshape(hidden, 1)
    h = pl.pallas_call(
        functools.partial(_hy_mlp_kernel, seq=seq, emb_rows=emb_rows),
        out_shape=jax.ShapeDtypeStruct((hidden, 2 * seq), F32),
        name="hyena_filter_mlp",
    )(w1t, col(b1), col(fr1), w2.T.astype(F32), col(b2), col(fr2), w3.T.astype(F32), col(b3), col(fr3))
    w4r = w4.astype(F32).reshape(hidden, HY_ORDER, 2, d_hy)
    w4f = w4r[:, :, 0].reshape(hidden, HY_ORDER * d_hy).T
    w4b = w4r[:, :, 1].reshape(hidden, HY_ORDER * d_hy).T
    fwd, _ = _dft_matrices(p)
    fwd_hi = fwd.astype(BF16)
    fwd_lo = (fwd - fwd_hi.astype(F32)).astype(BF16)
    nb = seq // p
    n_rows = HY_ORDER * d_hy
    return pl.pallas_call(
        functools.partial(_hy_spec_kernel, seq=seq, p=p, d_hy=d_hy, rows=rows),
        grid=(n_rows // rows,),
        in_specs=[pl.BlockSpec((hidden, 2 * seq), lambda r: (0, 0)),
                  pl.BlockSpec((rows, hidden), lambda r: (r, 0)),
                  pl.BlockSpec((rows, hidden), lambda r: (r, 0)),
                  pl.BlockSpec((p, 2 * p), lambda r: (0, 0)),
                  pl.BlockSpec((p, 2 * p), lambda r: (0, 0))],
        out_specs=pl.BlockSpec((2 * nb - 1, 3, rows, p), lambda r: (0, 0, r, 0)),
        out_shape=jax.ShapeDtypeStruct((2 * nb - 1, 3, n_rows, p), spec_dtype),
        compiler_params=_params("parallel"),
        name="hyena_filter_spectra",
    )(h, w4f, w4b, fwd_hi, fwd_lo)


def _hy_conv_kernel(g1_ref, g2_ref, v_ref, w1_ref, w2_ref, wv_ref, b1_ref, b2_ref, bv_ref,
                    k0_ref, k1_ref, s0_ref, s1_ref, fwd_ref, inv_ref, o_ref, *, p):
    bt, r, seq = v_ref.shape
    nb = seq // p
    rows = bt * r
    lane = lax.broadcasted_iota(jnp.int32, (1, 1, seq), 2)

    def short_conv(x_ref, w_ref, b_ref):
        x = x_ref[...].astype(F32)
        x2 = x.reshape(rows, seq)
        prev = jnp.where(lane == 0, 0.0, pltpu.roll(x2, 1, axis=1).reshape(bt, r, seq))
        nxt = jnp.where(lane == seq - 1, 0.0, pltpu.roll(x2, seq - 1, axis=1).reshape(bt, r, seq))
        w = w_ref[...]
        return b_ref[...] + w[:, 0:1] * prev + w[:, 1:2] * x + w[:, 2:3] * nxt

    fwd = fwd_ref[...]
    inv = inv_ref[...]

    def long_conv(u, k_ref):
        ub = u.astype(BF16).reshape(rows, seq)
        spec = [jnp.dot(ub[:, j * p:(j + 1) * p], fwd, preferred_element_type=F32).reshape(bt, r, 2 * p)
                for j in range(nb)]
        outs = []
        for i in range(nb):
            acc_re = jnp.zeros((bt, r, p), F32)
            acc_im = jnp.zeros((bt, r, p), F32)
            for j in range(nb):
                m = i - j + nb - 1
                ka = k_ref[m, 0].astype(F32)
                kb = k_ref[m, 1].astype(F32)
                kc = k_ref[m, 2].astype(F32)
                u_re = spec[j][:, :, :p]
                u_im = spec[j][:, :, p:]
                acc_re = acc_re + ka * u_re - kb * u_im
                acc_im = acc_im + kc * u_im + kb * u_re
            y_spec = jnp.concatenate([acc_re, acc_im], axis=2).astype(BF16).reshape(rows, 2 * p)
            outs.append(jnp.dot(y_spec, inv, preferred_element_type=F32).reshape(bt, r, p))
        return jnp.concatenate(outs, axis=2)

    v = short_conv(v_ref, wv_ref, bv_ref)
    z = short_conv(g1_ref, w1_ref, b1_ref) * (long_conv(v, k0_ref) + v * s0_ref[...])
    y = short_conv(g2_ref, w2_ref, b2_ref) * (long_conv(z, k1_ref) + z * s1_ref[...])
    for b in range(bt):
        o_ref[b] = y[b].T.astype(o_ref.dtype)


def _hyena_mixer(p_t, conv_w_t, conv_b_col, spectra, skip_col, d_hy, p, r, bt, out_dtype):
    bsz, _, seq = p_t.shape
    nb = seq // p
    nblk = d_hy // r
    fwd, inv = _dft_matrices(p)
    x_spec = lambda off: pl.BlockSpec((bt, r, seq), lambda j, b: (b, off * nblk + j, 0))
    w_spec = lambda off: pl.BlockSpec((r, conv_w_t.shape[1]), lambda j, b: (off * nblk + j, 0))
    c_spec = lambda off: pl.BlockSpec((r, 1), lambda j, b: (off * nblk + j, 0))
    k_spec = lambda off: pl.BlockSpec((2 * nb - 1, 3, r, p), lambda j, b: (0, 0, off * nblk + j, 0))
    return pl.pallas_call(
        functools.partial(_hy_conv_kernel, p=p),
        grid=(nblk, bsz // bt),
        in_specs=[x_spec(0), x_spec(1), x_spec(2), w_spec(0), w_spec(1), w_spec(2),
                  c_spec(0), c_spec(1), c_spec(2), k_spec(0), k_spec(1), c_spec(0), c_spec(1),
                  pl.BlockSpec((p, 2 * p), lambda j, b: (0, 0)),
                  pl.BlockSpec((2 * p, p), lambda j, b: (0, 0))],
        out_specs=pl.BlockSpec((bt, seq, r), lambda j, b: (b, 0, j)),
        out_shape=jax.ShapeDtypeStruct((bsz, seq, d_hy), out_dtype),
        compiler_params=_params("parallel", "arbitrary"),
        name="hyena_mixer",
    )(p_t, p_t, p_t, conv_w_t, conv_w_t, conv_w_t, conv_b_col, conv_b_col, conv_b_col,
      spectra, spectra, skip_col, skip_col, fwd.astype(BF16), inv.astype(BF16))


def _ssd_prep_kernel(dt_ref, dtb_ref, a_ref, cum_ref, dtt_ref, *, r):
    seq = dt_ref.shape[1]
    q = SSD_CHUNK
    raw = dt_ref[0] + dtb_ref[...]
    dt = jnp.maximum(raw, 0.0) + jnp.log1p(jnp.exp(-jnp.abs(raw)))
    a = dt * a_ref[...]
    li = lax.broadcasted_iota(jnp.int32, (q, q), 0)
    si = lax.broadcasted_iota(jnp.int32, (q, q), 1)
    lower = jnp.where(li >= si, 1.0, 0.0).astype(F32)
    upper = jnp.where(li <= si, 1.0, 0.0).astype(F32)
    fwd_col = lax.broadcasted_iota(jnp.int32, (1, 128), 1) % (2 * r) < r
    for c in range(seq // q):
        ac = a[c * q:(c + 1) * q]
        cum = jnp.where(fwd_col,
                        jnp.dot(lower, ac, precision=HI, preferred_element_type=F32),
                        jnp.dot(upper, ac, precision=HI, preferred_element_type=F32))
        cum_ref[0, :, c * q:(c + 1) * q] = cum.T
        dtt_ref[0, :, c * q:(c + 1) * q] = dt[c * q:(c + 1) * q].T


def _ssd_prep(p_dt, dt_bias, a_neg, r):
    bsz, seq, _ = p_dt.shape
    vec = pl.BlockSpec((1, 128), lambda b: (0, 0))
    out = pl.BlockSpec((1, 128, seq), lambda b: (b, 0, 0))
    return pl.pallas_call(
        functools.partial(_ssd_prep_kernel, r=r),
        grid=(bsz,),
        in_specs=[pl.BlockSpec((1, seq, 128), lambda b: (b, 0, 0)), vec, vec],
        out_specs=[out, out],
        out_shape=[jax.ShapeDtypeStruct((bsz, 128, seq), F32)] * 2,
        compiler_params=_params("parallel"),
        name="ssd_prep",
    )(p_dt, dt_bias, a_neg)


def _ssd_kernel(z_ref, x_ref, bt_ref, c_ref, cum_ref, dtt_ref, wx_ref, wbt_ref, wc_ref,
                bx_ref, bbt_ref, bc_ref, dskip_ref, nw_ref, o_ref, xm_s, bt_s, cs_s, y_s, st_s):
    nbat, seq, gw = x_ref.shape
    n = bt_ref.shape[1]
    r = gw // SSD_HEADDIM
    q = SSD_CHUNK
    nc = seq // q
    half = SSD_CONV // 2
    shifts = [k - half for k in range(SSD_CONV) if k != half]
    halo_r, halo_l = 16, 128

    k_r = 2 * q
    sr = lax.broadcasted_iota(jnp.int32, (len(shifts) * q, k_r), 0)
    se = lax.broadcasted_iota(jnp.int32, (len(shifts) * q, k_r), 1)
    band_r = jnp.zeros(sr.shape, jnp.bool_)
    for j, sh in enumerate(shifts):
        band_r = jnp.logical_or(band_r, jnp.logical_and(sr // q == j, se == halo_r + sr % q + sh))
    band_r = jnp.where(band_r, 1.0, 0.0).astype(BF16)
    k_l = q + 2 * halo_l
    le = lax.broadcasted_iota(jnp.int32, (k_l, len(shifts) * q), 0)
    lc = lax.broadcasted_iota(jnp.int32, (k_l, len(shifts) * q), 1)
    band_l = jnp.zeros(le.shape, jnp.bool_)
    for j, sh in enumerate(shifts):
        band_l = jnp.logical_or(band_l, jnp.logical_and(lc // q == j, le == halo_l + lc % q + sh))
    band_l = jnp.where(band_l, 1.0, 0.0).astype(BF16)

    col_head = lax.broadcasted_iota(jnp.int32, (1, gw), 1) // SSD_HEADDIM
    wx, wc, wbt = wx_ref[...], wc_ref[...], wbt_ref[...]

    def conv_rows(ref, bb, c, off, w, bias):
        width = ref.shape[2]
        cur = ref[bb, pl.ds(off, q), :]
        prev = ref[bb, pl.ds(pl.multiple_of(jnp.maximum(off - halo_r, 0), halo_r), halo_r), :]
        nxt = ref[bb, pl.ds(pl.multiple_of(jnp.minimum(off + q, seq - halo_r), halo_r), halo_r), :]
        prev = jnp.where(c > 0, prev, jnp.zeros_like(prev))
        nxt = jnp.where(c < nc - 1, nxt, jnp.zeros_like(nxt))
        ext = jnp.concatenate([prev, cur, nxt, jnp.zeros((k_r - q - 2 * halo_r, width), BF16)], axis=0)
        moved = jnp.dot(band_r, ext, preferred_element_type=F32)
        acc = bias + w[half:half + 1] * cur.astype(F32)
        for j, sh in enumerate(shifts):
            acc = acc + w[half + sh:half + sh + 1] * moved[j * q:(j + 1) * q]
        return _silu(acc)

    def conv_one(bb, c, off):
        xs = conv_rows(x_ref, bb, c, off, wx, bx_ref[...])
        y_s[bb, 0, pl.ds(off, q), :] = xs * dskip_ref[...]
        for h in range(r):
            xm_s[bb, h, pl.ds(off, q), :] = jnp.where(col_head == h, xs, 0.0).astype(BF16)
        cs_s[bb, pl.ds(off, q), :] = conv_rows(c_ref, bb, c, off, wc, bc_ref[...]).astype(BF16)
        cur = bt_ref[bb, :, pl.ds(off, q)]
        prev = bt_ref[bb, :, pl.ds(pl.multiple_of(jnp.maximum(off - halo_l, 0), halo_l), halo_l)]
        nxt = bt_ref[bb, :, pl.ds(pl.multiple_of(jnp.minimum(off + q, seq - halo_l), halo_l), halo_l)]
        prev = jnp.where(c > 0, prev, jnp.zeros_like(prev))
        nxt = jnp.where(c < nc - 1, nxt, jnp.zeros_like(nxt))
        moved = jnp.dot(jnp.concatenate([prev, cur, nxt], axis=1), band_l, preferred_element_type=F32)
        acc = bbt_ref[...] + wbt[:, half:half + 1] * cur.astype(F32)
        for j, sh in enumerate(shifts):
            acc = acc + wbt[:, half + sh:half + sh + 1] * moved[:, j * q:(j + 1) * q]
        bt_s[bb, :, pl.ds(off, q)] = _silu(acc).astype(BF16)

    def conv_step(c, carry):
        off = pl.multiple_of(c * q, q)
        for bb in range(nbat):
            conv_one(bb, c, off)
        return carry

    lax.fori_loop(0, nc, conv_step, 0, unroll=4)
    st_s[...] = jnp.zeros_like(st_s)

    li = lax.broadcasted_iota(jnp.int32, (q, q), 0)
    si = lax.broadcasted_iota(jnp.int32, (q, q), 1)
    diag = li == si
    sub = lax.broadcasted_iota(jnp.int32, (2 * r, 1), 0)
    neg_inf = jnp.float32(-jnp.inf)
    nh = 2 * r
    assert 3 * nh + 16 <= q
    ones_rows = jnp.ones((8, q), F32)
    pad_rows = jnp.zeros((q - 3 * nh - 16, q), F32)
    pk = lax.broadcasted_iota(jnp.int32, (3 * nh + 8, nh * q), 0)
    pj = lax.broadcasted_iota(jnp.int32, (3 * nh + 8, nh * q), 1) // q
    seg_top = jnp.where(jnp.logical_and(pk < 3 * nh, pk % nh == pj), 1.0, 0.0).astype(BF16)
    seg_bottom = jnp.zeros((q - 3 * nh - 8 - 16, nh * q), BF16)

    def split3(v):
        hi = v.astype(BF16).astype(F32)
        mid = (v - hi).astype(BF16).astype(F32)
        lo = (v - hi - mid).astype(BF16).astype(F32)
        return hi, mid, lo

    def flat(v):
        return jnp.concatenate([v[j:j + 1, :] for j in range(nh)], axis=1)

    def chunk_step(i, carry):
        offs = (pl.multiple_of(i * q, q), pl.multiple_of((nc - 1 - i) * q, q))
        chains = [(bb, d) for bb in range(nbat) for d in range(2)]
        pre = []
        for bb in range(nbat):
            cum = jnp.where(sub < r, cum_ref[bb, :, pl.ds(offs[0], q)], cum_ref[bb, :, pl.ds(offs[1], q)])
            dt = jnp.where(sub < r, dtt_ref[bb, :, pl.ds(offs[0], q)], dtt_ref[bb, :, pl.ds(offs[1], q)])
            total = jnp.where(sub < r, cum[:, q - 1:q], cum[:, 0:1])
            hi, mid, lo = split3(cum)
            left = jnp.concatenate([hi, mid, lo, jnp.zeros((8, q), F32), ones_rows, pad_rows], axis=0)
            left = left.T.astype(BF16)
            minus = jnp.concatenate([-flat(hi), -flat(mid), -flat(lo), jnp.zeros((13, nh * q), F32)],
                                    axis=0)
            right = jnp.concatenate([seg_top, minus.astype(BF16), seg_bottom], axis=0)
            pre.append(dict(dt=dt, carry_decay=jnp.exp(total), in_decay=jnp.exp(cum),
                            out_w=jnp.exp(total - cum) * dt, left=left, right=right))
        seg_all = [jnp.dot(p_["left"], p_["right"], preferred_element_type=F32) for p_ in pre]
        btc = {c: bt_s[c[0], :, pl.ds(offs[c[1]], q)] for c in chains}
        cc = {c: cs_s[c[0], pl.ds(offs[c[1]], q), :] for c in chains}
        state = {c: st_s[c[0], c[1]] for c in chains}
        cb = {c: jnp.dot(cc[c], btc[c], preferred_element_type=F32) for c in chains}
        c_state = {c: jnp.dot(cc[c], state[c].astype(BF16), preferred_element_type=F32) for c in chains}
        lhs_y, rhs_y, lhs_s, rhs_s, decays = {}, {}, {}, {}, {}
        for c in chains:
            bb, d = c
            p_ = pre[bb]
            tri = (li >= si) if d == 0 else (li <= si)
            btf = btc[c].astype(F32)
            mats, diags, xs_h, cs_h, b_scaled = [], [], [], [], []
            state_decay = jnp.zeros((1, gw), F32)
            for h in range(r):
                ln = d * r + h
                seg = seg_all[bb][:, ln * q:(ln + 1) * q]
                mats.append((cb[c] * jnp.exp(jnp.where(tri, seg, neg_inf)) * p_["dt"][ln:ln + 1, :])
                            .astype(BF16))
                diags.append(jnp.where(diag, p_["in_decay"][ln:ln + 1, :], 0.0).astype(BF16))
                xs_h.append(xm_s[bb, h, pl.ds(offs[d], q), :])
                cs_h.append(jnp.where(col_head == h, c_state[c], 0.0).astype(BF16))
                b_scaled.append((btf * p_["out_w"][ln:ln + 1, :]).astype(BF16))
                state_decay = jnp.where(col_head == h, p_["carry_decay"][ln:ln + 1, :], state_decay)
            lhs_y[c] = jnp.concatenate(mats + diags, axis=1)
            rhs_y[c] = jnp.concatenate(xs_h + cs_h, axis=0)
            lhs_s[c] = jnp.concatenate(b_scaled, axis=1)
            rhs_s[c] = jnp.concatenate(xs_h, axis=0)
            decays[c] = state_decay
        y_new = {c: jnp.dot(lhs_y[c], rhs_y[c], preferred_element_type=F32) for c in chains}
        upd = {c: jnp.dot(lhs_s[c], rhs_s[c], preferred_element_type=F32) for c in chains}
        for c in chains:
            bb, d = c
            y_s[bb, 1 + d, pl.ds(offs[d], q), :] = y_new[c]
            st_s[bb, d] = state[c] * decays[c] + upd[c]
        return carry

    lax.fori_loop(0, nc, chunk_step, 0, unroll=4)

    def gate_step(c, carry):
        off = pl.multiple_of(c * q, q)
        for bb in range(nbat):
            zf = z_ref[bb, pl.ds(off, q), :].astype(F32)
            y = (y_s[bb, 0, pl.ds(off, q), :] + y_s[bb, 1, pl.ds(off, q), :]
                 + y_s[bb, 2, pl.ds(off, q), :])
            y = y * _silu(zf)
            y = y * lax.rsqrt(jnp.mean(jnp.square(y), axis=-1, keepdims=True) + RMS_EPS)
            o_ref[bb, pl.ds(off, q), :] = (y * nw_ref[...]).astype(o_ref.dtype)
        return carry

    lax.fori_loop(0, nc, gate_step, 0, unroll=4)


def _ssd_mixer(pm, p_t, bt_off, cum_t, dt_t, conv_w, conv_b, dskip_cols, norm_w, d_ssm, out_dtype):
    bsz, seq, _ = pm.shape
    gw = d_ssm // SSD_GROUPS
    r = gw // SSD_HEADDIM
    n = SSD_STATE
    xo, co = d_ssm // gw, 2 * d_ssm // n
    cw = lambda width, off: pl.BlockSpec((SSD_CONV, width), lambda b, g: (0, off + g))
    cb = lambda width, off: pl.BlockSpec((1, width), lambda b, g: (0, off + g))
    nbat = _pick(bsz, 2, 1)
    heads_rows = pl.BlockSpec((nbat, 2 * r, seq), lambda b, g: (b, g, 0))
    return pl.pallas_call(
        _ssd_kernel,
        grid=(bsz // nbat, SSD_GROUPS),
        in_specs=[pl.BlockSpec((nbat, seq, gw), lambda b, g: (b, 0, g)),
                  pl.BlockSpec((nbat, seq, gw), lambda b, g: (b, 0, xo + g)),
                  pl.BlockSpec((nbat, n, seq), lambda b, g: (b, bt_off + g, 0)),
                  pl.BlockSpec((nbat, seq, n), lambda b, g: (b, 0, co + g)),
                  heads_rows, heads_rows,
                  cw(gw, 0),
                  pl.BlockSpec((n, SSD_CONV), lambda b, g: (d_ssm // n + g, 0)),
                  cw(n, d_ssm // n + SSD_GROUPS),
                  cb(gw, 0),
                  pl.BlockSpec((n, 1), lambda b, g: (d_ssm // n + g, 0)),
                  cb(n, d_ssm // n + SSD_GROUPS),
                  cb(gw, 0), cb(gw, 0)],
        out_specs=pl.BlockSpec((nbat, seq, gw), lambda b, g: (b, 0, g)),
        out_shape=jax.ShapeDtypeStruct((bsz, seq, d_ssm), out_dtype),
        scratch_shapes=[pltpu.VMEM((nbat, r, seq, gw), BF16),
                        pltpu.VMEM((nbat, n, seq), BF16), pltpu.VMEM((nbat, seq, n), BF16),
                        pltpu.VMEM((nbat, 3, seq, gw), F32), pltpu.VMEM((nbat, 2, n, gw), F32)],
        compiler_params=_params("parallel", "parallel"),
        name="ssd_mixer",
    )(pm, pm, p_t, pm, cum_t, dt_t, conv_w, conv_w.T, conv_w, conv_b, conv_b.reshape(-1, 1), conv_b,
      dskip_cols, norm_w)


def _layer_norm(v, g, b):
    mu = jnp.mean(v, axis=-1, keepdims=True)
    c = v - mu
    var = jnp.mean(jnp.square(c), axis=-1, keepdims=True)
    return c * lax.rsqrt(var + LN_EPS) * g + b


def _merge_kernel(yh_ref, ys_ref, gh_ref, gs_ref, x_ref, whb_ref, wsb_ref, wo_ref, bo_ref,
                  lg_ref, lb_ref, rwt_ref, h_ref, hp_ref, lt_ref):
    a = jnp.dot(yh_ref[...], whb_ref[...], preferred_element_type=F32)
    b = jnp.dot(ys_ref[...], wsb_ref[...], preferred_element_type=F32)
    m = _sigmoid(gh_ref[...].astype(F32)) * a + _sigmoid(gs_ref[...].astype(F32)) * b
    mix = jnp.dot(m.astype(BF16), wo_ref[...], preferred_element_type=F32) + bo_ref[...]
    h = _layer_norm(DN_ALPHA * x_ref[...] + mix, lg_ref[...], lb_ref[...])
    h_ref[...] = h
    hp_ref[...] = _pack_halves(h)
    lt_ref[...] = lax.dot_general(rwt_ref[...], h, (((1,), (1,)), ((), ())), precision=HI,
                                  preferred_element_type=F32)


def _merge(yhy, yssd, pm, gate_off, x2d, whb, wsb, wo, bo, lg, lb, rwt, tm):
    t, d = x2d.shape
    c = yhy.shape[1]
    d_ssm = yssd.shape[1]
    e = rwt.shape[0]
    row = lambda w: pl.BlockSpec((tm, w), lambda i: (i, 0))
    whole = lambda a: pl.BlockSpec(a.shape, lambda i: (0,) * a.ndim)
    return pl.pallas_call(
        _merge_kernel,
        grid=(t // tm,),
        in_specs=[row(c), row(d_ssm),
                  pl.BlockSpec((tm, d), lambda i: (i, gate_off)),
                  pl.BlockSpec((tm, d), lambda i: (i, gate_off + 1)),
                  row(d), whole(whb), whole(wsb), whole(wo), whole(bo), whole(lg), whole(lb), whole(rwt)],
        out_specs=[row(d), row(d // 2), pl.BlockSpec((e, tm), lambda i: (0, i))],
        out_shape=[jax.ShapeDtypeStruct((t, d), F32), jax.ShapeDtypeStruct((t, d // 2), jnp.int32),
                   jax.ShapeDtypeStruct((e, t), F32)],
        compiler_params=_params("parallel"),
        name="merge_ln",
    )(yhy, yssd, pm, pm, x2d, whb, wsb, wo, bo, lg, lb, rwt)


def _first_max(v, axis, size):
    idx = lax.broadcasted_iota(jnp.int32, v.shape, axis)
    top = jnp.max(v, axis=axis, keepdims=True)
    first = jnp.min(jnp.where(v == top, idx, size), axis=axis, keepdims=True)
    return idx == first, top


def _route_kernel(lt_ref, bias_ref, w_ref, sel_ref):
    n_e, tn = lt_ref.shape
    per = n_e // N_EXPERT_GROUPS
    scores = jax.nn.sigmoid(lt_ref[...])
    biased = scores + bias_ref[...]
    b3 = biased.reshape(N_EXPERT_GROUPS, per, tn)
    hit, top1 = _first_max(b3, 1, per)
    top2 = jnp.max(jnp.where(hit, -jnp.inf, b3), axis=1, keepdims=True)
    cur = (top1 + top2).reshape(N_EXPERT_GROUPS, tn)
    grp = jnp.zeros(cur.shape, jnp.bool_)
    for _ in range(TOPK_GROUPS):
        hit, _ = _first_max(cur, 0, N_EXPERT_GROUPS)
        grp = jnp.logical_or(grp, hit)
        cur = jnp.where(hit, -jnp.inf, cur)
    cur = jnp.where(grp.reshape(N_EXPERT_GROUPS, 1, tn), b3, -jnp.inf).reshape(n_e, tn)
    chosen = jnp.zeros(cur.shape, jnp.bool_)
    for _ in range(TOP_K):
        hit, _ = _first_max(cur, 0, n_e)
        chosen = jnp.logical_or(chosen, hit)
        cur = jnp.where(hit, -jnp.inf, cur)
    w = jnp.where(chosen, scores, 0.0)
    w_ref[...] = w / jnp.sum(w, axis=0, keepdims=True) * ROUTED_SCALE
    sel_ref[...] = jnp.where(chosen, 1.0, 0.0).astype(sel_ref.dtype)


def _route(logits_t, bias_col, tn):
    e, t = logits_t.shape
    blk = pl.BlockSpec((e, tn), lambda i: (0, i))
    return pl.pallas_call(
        _route_kernel,
        grid=(t // tn,),
        in_specs=[blk, pl.BlockSpec((e, 1), lambda i: (0, 0))],
        out_specs=[blk, blk],
        out_shape=[jax.ShapeDtypeStruct((e, t), F32), jax.ShapeDtypeStruct((e, t), BF16)],
        compiler_params=_params("parallel"),
        name="route_topk",
    )(logits_t, bias_col)


MOE_BLOCK = 512
MOE_TOKEN_RANGES = 2


def _rank_kernel(sel_ref, pos_ref, cnt_ref, carry_ref):
    n_e, tn = sel_ref.shape

    @pl.when(pl.program_id(0) == 0)
    def _():
        carry_ref[...] = jnp.zeros_like(carry_ref)

    sel = sel_ref[...]
    s = lax.broadcasted_iota(jnp.int32, (tn, tn), 0)
    t = lax.broadcasted_iota(jnp.int32, (tn, tn), 1)
    earlier = jnp.where(s < t, 1.0, 0.0).astype(BF16)
    before = jnp.dot(sel, earlier, preferred_element_type=F32)
    carry = carry_ref[...]
    pos_ref[...] = carry[:, 0:1] + before
    carry = carry + jnp.sum(sel.astype(F32), axis=1, keepdims=True)
    carry_ref[...] = carry
    cnt_ref[...] = carry


def _rank(sel_t, tn):
    e, t = sel_t.shape
    return pl.pallas_call(
        _rank_kernel,
        grid=(t // tn,),
        in_specs=[pl.BlockSpec((e, tn), lambda i: (0, i))],
        out_specs=[pl.BlockSpec((e, tn), lambda i: (0, i)), pl.BlockSpec((e, 128), lambda i: (0, 0))],
        out_shape=[jax.ShapeDtypeStruct((e, t), F32), jax.ShapeDtypeStruct((e, 128), F32)],
        scratch_shapes=[pltpu.VMEM((e, 128), F32)],
        compiler_params=_params("arbitrary"),
        name="moe_rank",
    )(sel_t)


def _segment_starts(cnt):
    n_e = cnt.shape[0]
    padded = jnp.floor((cnt + (MOE_BLOCK - 1)) * (1.0 / MOE_BLOCK)) * MOE_BLOCK
    a = lax.broadcasted_iota(jnp.int32, (n_e, n_e), 0)
    b = lax.broadcasted_iota(jnp.int32, (n_e, n_e), 1)
    below = jnp.where(b < a, 1.0, 0.0).astype(F32)
    return padded, jnp.dot(below, padded, precision=HI, preferred_element_type=F32)


def _slots_kernel(sel_ref, w_ref, pos_ref, cnt_ref, slot_ref, wk_ref):
    n_e, tn = sel_ref.shape
    _, start = _segment_starts(cnt_ref[...])
    slot = start[:, 0:1] + pos_ref[...]
    w = w_ref[...]
    cur = sel_ref[...].astype(F32)
    idx = lax.broadcasted_iota(jnp.int32, (n_e, tn), 0)
    slots, weights = [], []
    for _ in range(TOP_K):
        first = jnp.min(jnp.where(cur > 0.0, idx, n_e), axis=0, keepdims=True)
        hit = idx == first
        slots.append(jnp.sum(jnp.where(hit, slot, 0.0), axis=0, keepdims=True))
        weights.append(jnp.sum(jnp.where(hit, w, 0.0), axis=0, keepdims=True))
        cur = jnp.where(hit, 0.0, cur)
    slot_ref[...] = jnp.concatenate(slots, axis=0).astype(jnp.int32)
    wk_ref[...] = jnp.concatenate(weights, axis=0).T


def _slots(sel_t, w_t, pos_t, cnt, tn):
    e, t = sel_t.shape
    blk = pl.BlockSpec((e, tn), lambda i: (0, i))
    return pl.pallas_call(
        _slots_kernel,
        grid=(t // tn,),
        in_specs=[blk, blk, blk, pl.BlockSpec((e, 128), lambda i: (0, 0))],
        out_specs=[pl.BlockSpec((TOP_K, tn), lambda i: (0, i)), pl.BlockSpec((tn, TOP_K), lambda i: (i, 0))],
        out_shape=[jax.ShapeDtypeStruct((TOP_K, t), jnp.int32), jax.ShapeDtypeStruct((t, TOP_K), F32)],
        compiler_params=_params("parallel"),
        name="moe_slots",
    )(sel_t, w_t, pos_t, cnt)


def _plan_kernel(cnt_ref, expert_ref, valid_ref):
    n_e = cnt_ref.shape[0]
    nbp = expert_ref.shape[1]
    cnt = cnt_ref[...]
    padded, start = _segment_starts(cnt)
    end = (start + padded)[:, 0:1]
    last = (start + cnt)[:, 0:1]
    row0 = (lax.broadcasted_iota(jnp.int32, (1, nbp), 1) * MOE_BLOCK).astype(F32)
    expert = jnp.minimum(jnp.sum(jnp.where(end <= row0, 1.0, 0.0), axis=0, keepdims=True), n_e - 1.0)
    eid = lax.broadcasted_iota(jnp.int32, (n_e, nbp), 0).astype(F32)
    last_b = jnp.sum(jnp.where(eid == expert, last, 0.0), axis=0, keepdims=True)
    expert_ref[...] = expert.astype(jnp.int32)
    valid_ref[...] = jnp.clip(last_b - row0, 0.0, float(MOE_BLOCK)).astype(jnp.int32)


def _plan(cnt, n_blocks):
    nbp = -(-n_blocks // 128) * 128
    return pl.pallas_call(
        _plan_kernel,
        out_shape=[jax.ShapeDtypeStruct((1, nbp), jnp.int32)] * 2,
        name="moe_plan",
    )(cnt)


V7X_SC_CORES = 2
V7X_SC_SUBCORES = 16
SC_WINDOW = 128


def _sc_mesh():
    return plsc.VectorSubcoreMesh(core_axis_name="c", subcore_axis_name="s",
                                  num_cores=V7X_SC_CORES, num_subcores=V7X_SC_SUBCORES)


def _sc_scatter_rows(rows, row0, idx, n_out):
    width = rows.shape[1]
    k_n, t = idx.shape
    workers = V7X_SC_CORES * V7X_SC_SUBCORES
    per_worker = t // workers
    assert per_worker * workers == t and per_worker % SC_WINDOW == 0

    def body(rows_hbm, idx_hbm, out_hbm, idx_v, rows_v, sem):
        wid = lax.axis_index("s") * V7X_SC_CORES + lax.axis_index("c")

        @pl.loop(0, per_worker // SC_WINDOW)
        def _(j):
            base = wid * per_worker + j * SC_WINDOW
            pltpu.sync_copy(rows_hbm.at[pl.ds(row0 + base, SC_WINDOW)], rows_v)
            for k in range(k_n):
                pltpu.sync_copy(idx_hbm.at[pl.ds(k * t + base, SC_WINDOW)], idx_v.at[k])
            copies = [pltpu.async_copy(rows_v, out_hbm.at[idx_v.at[k]], sem) for k in range(k_n)]
            for copy in copies:
                copy.wait()

    return pl.kernel(
        body, out_type=jax.ShapeDtypeStruct((n_out, width), rows.dtype), mesh=_sc_mesh(),
        scratch_types=[pltpu.VMEM((k_n, SC_WINDOW), jnp.int32), pltpu.VMEM((SC_WINDOW, width), rows.dtype),
                       pltpu.SemaphoreType.DMA],
        name="moe_dispatch",
    )(rows, idx.reshape(-1))


def _sc_gather_rows(table, idx):
    width = table.shape[1]
    k_n, t = idx.shape
    workers = V7X_SC_CORES * V7X_SC_SUBCORES
    per_worker = t // workers
    assert per_worker * workers == t and per_worker % SC_WINDOW == 0

    def body(table_hbm, idx_hbm, out_hbm, idx_v, rows_v, sem):
        wid = lax.axis_index("s") * V7X_SC_CORES + lax.axis_index("c")

        @pl.loop(0, per_worker // SC_WINDOW)
        def _(j):
            base = wid * per_worker + j * SC_WINDOW
            for k in range(k_n):
                pltpu.sync_copy(idx_hbm.at[pl.ds(k * t + base, SC_WINDOW)], idx_v)
                pltpu.async_copy(table_hbm.at[idx_v], rows_v, sem).wait()
                pltpu.sync_copy(rows_v, out_hbm.at[pl.ds(k * t + base, SC_WINDOW)])

    out = pl.kernel(
        body, out_type=jax.ShapeDtypeStruct((k_n * t, width), table.dtype), mesh=_sc_mesh(),
        scratch_types=[pltpu.VMEM((SC_WINDOW,), jnp.int32), pltpu.VMEM((SC_WINDOW, width), table.dtype),
                       pltpu.SemaphoreType.DMA],
        name="moe_collect",
    )(table, idx.reshape(-1))
    return out.reshape(k_n, t, width)


def _swiglu(xb, wg, wu):
    gate = jnp.dot(xb, wg, preferred_element_type=F32)
    up = jnp.dot(xb, wu, preferred_element_type=F32)
    return _silu(gate) * up


def _pack_halves(v):
    w = v.shape[1] // 2
    hi = pltpu.bitcast(v[:, :w].astype(BF16).astype(F32), jnp.int32)
    lo = pltpu.bitcast(v[:, w:].astype(BF16).astype(F32), jnp.int32)
    return jnp.bitwise_or(hi, lax.shift_right_logical(lo, 16))


def _unpack_halves(p):
    hi = pltpu.bitcast(jnp.bitwise_and(p, jnp.int32(-65536)), F32)
    lo = pltpu.bitcast(lax.shift_left(p, 16), F32)
    return jnp.concatenate([hi, lo], axis=1)


def _experts_kernel(expert_ref, valid_ref, x_ref, wg_ref, wu_ref, wd_ref, y_ref, wg_s, wu_s, wd_s):
    b = pl.program_id(0)
    valid = valid_ref[0, b]
    fresh = jnp.logical_or(b == 0, expert_ref[0, b] != expert_ref[0, jnp.maximum(b - 1, 0)])

    @pl.when(jnp.logical_and(valid > 0, fresh))
    def _():
        wg_s[...] = wg_ref[0].astype(BF16)
        wu_s[...] = wu_ref[0].astype(BF16)
        wd_s[...] = wd_ref[0].astype(BF16)

    @pl.when(valid > 0)
    def _():
        rows = lax.broadcasted_iota(jnp.int32, (x_ref.shape[0], 1), 0)
        x = jnp.where(rows < valid, _unpack_halves(x_ref[...]), 0.0).astype(BF16)
        mid = _swiglu(x, wg_s[...], wu_s[...]).astype(BF16)
        y_ref[...] = _pack_halves(jnp.dot(mid, wd_s[...], preferred_element_type=F32))

    @pl.when(valid == 0)
    def _():
        y_ref[...] = jnp.zeros_like(y_ref)


def _experts(x_sorted, expert, valid, wg, wu, wd):
    n_rows, half = x_sorted.shape
    n_e, d, f = wg.shape
    by_expert = lambda shape: pl.BlockSpec((1,) + shape, lambda b, e_ref, v_ref: (e_ref[0, b], 0, 0))
    rows = pl.BlockSpec((MOE_BLOCK, half), lambda b, e_ref, v_ref: (b, 0))
    return pl.pallas_call(
        _experts_kernel,
        grid_spec=pltpu.PrefetchScalarGridSpec(
            num_scalar_prefetch=2, grid=(n_rows // MOE_BLOCK,),
            in_specs=[rows, by_expert((d, f)), by_expert((d, f)), by_expert((f, d))],
            out_specs=rows,
            scratch_shapes=[pltpu.VMEM((d, f), BF16), pltpu.VMEM((d, f), BF16), pltpu.VMEM((f, d), BF16)]),
        out_shape=jax.ShapeDtypeStruct((n_rows, half), jnp.int32),
        compiler_params=_params("arbitrary"),
        name="moe_experts",
    )(expert, valid, x_sorted, wg, wu, wd)


def _combine_kernel(y_ref, wk_ref, hp_ref, h_ref, sg_ref, su_ref, sd_ref, lg_ref, lb_ref, o_ref):
    wk = wk_ref[...]
    mid = _swiglu(_unpack_halves(hp_ref[...]).astype(BF16), sg_ref[...], su_ref[...]).astype(BF16)
    acc = jnp.dot(mid, sd_ref[...], preferred_element_type=F32)
    for k in range(y_ref.shape[0]):
        acc = acc + wk[:, k:k + 1] * _unpack_halves(y_ref[k])
    o_ref[...] = _layer_norm(DN_ALPHA * h_ref[...] + acc, lg_ref[...], lb_ref[...])


def _combine_kernel_into(y_ref, wk_ref, hp_ref, h_ref, sg_ref, su_ref, sd_ref, lg_ref, lb_ref, prev_ref,
                         o_ref):
    del prev_ref
    _combine_kernel(y_ref, wk_ref, hp_ref, h_ref, sg_ref, su_ref, sd_ref, lg_ref, lb_ref, o_ref)


def _combine(y_tok, wk, hp, h, row0, prev, sg, su, sd, lg, lb, tm):
    t, d = h.shape
    k_n, t_part, half = y_tok.shape
    b0 = row0 // tm
    part = lambda w: pl.BlockSpec((tm, w), lambda i: (i, 0))
    full = lambda w: pl.BlockSpec((tm, w), lambda i: (b0 + i, 0))
    whole = lambda a: pl.BlockSpec(a.shape, lambda i: (0,) * a.ndim)
    in_specs = [pl.BlockSpec((k_n, tm, half), lambda i: (0, i, 0)), part(k_n), full(half), full(d),
                whole(sg), whole(su), whole(sd), whole(lg), whole(lb)]
    args = [y_tok, wk, hp, h, sg, su, sd, lg, lb]
    if prev is not None:
        in_specs.append(pl.BlockSpec(memory_space=pl.ANY))
        args.append(prev)
    return pl.pallas_call(
        _combine_kernel if prev is None else _combine_kernel_into,
        grid=(t_part // tm,),
        in_specs=in_specs,
        out_specs=full(d),
        out_shape=jax.ShapeDtypeStruct((t, d), F32),
        input_output_aliases={} if prev is None else {len(args) - 1: 0},
        compiler_params=_params("parallel"),
        name="moe_combine_ln",
    )(*args)


def _pick(n, *cands):
    for c in cands:
        if n % c == 0:
            return c
    return n


def _layer(h, w_in, b_in, hy_conv_w, hy_conv_b, hy_f_w1, hy_f_b1, hy_f_freq1, hy_f_w2, hy_f_b2,
           hy_f_freq2, hy_f_w3, hy_f_b3, hy_f_freq3, hy_f_w4, hy_bias, ssd_conv_w, ssd_conv_b,
           ssd_dt_bias, ssd_a_log, ssd_d, ssd_norm_w, w_hy_branch, w_ssd_branch, w_out, b_out,
           ln1_g, ln1_b, router_w, router_bias, exp_w_gate, exp_w_up, exp_w_down,
           sh_w_gate, sh_w_up, sh_w_down, ln2_g, ln2_b):
    bsz, seq, d = h.shape
    t = bsz * seq
    d_hy = hy_bias.shape[-1]
    d_ssm = ssd_norm_w.shape[-1]
    conv_dim = ssd_conv_w.shape[-1]
    heads = ssd_d.shape[-1]
    hy_cols = (HY_ORDER + 1) * d_hy
    o_z, o_dt, o_gate = hy_cols, hy_cols + d_ssm + conv_dim, hy_cols + d_ssm + conv_dim + 2 * heads
    o_b = o_z + 2 * d_ssm
    o_c = o_b + SSD_GROUPS * SSD_STATE
    row = lambda v: v.astype(F32).reshape(1, -1)
    col = lambda v: v.astype(F32).reshape(-1, 1)

    ch = lambda v: jnp.concatenate([v[..., :o_z], v[..., o_b:o_c]], axis=-1)
    w_ch, b_ch = ch(w_in).T.astype(BF16), col(ch(b_in))
    tmaj = lambda v: jnp.concatenate([v[..., o_z:o_b], v[..., o_c:o_dt], v[..., o_gate:]], axis=-1)
    w_main, b_main = tmaj(w_in).astype(BF16), row(tmaj(b_in))
    pad = 128 - 2 * heads
    hpg = heads // SSD_GROUPS
    perm = [dr * heads + g * hpg + hh for g in range(SSD_GROUPS) for dr in range(2) for hh in range(hpg)]
    by_group = lambda v: jnp.pad(v.astype(F32).reshape(-1)[jnp.array(perm)], (0, pad)).reshape(1, -1)
    w_dt = jnp.pad(w_in[:, o_dt:o_gate][:, jnp.array(perm)], ((0, 0), (0, pad))).astype(BF16)
    b_dt = by_group(b_in[o_dt:o_gate])
    p = _pick(seq, 512, 256, 128, 64)
    r = _pick(d_hy, 128, 64)
    spectra = _hyena_spectra(seq, p, hy_f_w1, hy_f_b1, hy_f_freq1, hy_f_w2, hy_f_b2, hy_f_freq2,
                             hy_f_w3, hy_f_b3, hy_f_freq3, hy_f_w4, d_hy, r, BF16)
    branch_w = (w_hy_branch.astype(BF16), w_ssd_branch.astype(BF16), w_out.astype(BF16),
                row(b_out), row(ln1_g), row(ln1_b), router_w.T.astype(F32))
    shared = (sh_w_gate.astype(BF16), sh_w_up.astype(BF16), sh_w_down.astype(BF16))
    experts_w = (exp_w_gate.astype(F32), exp_w_up.astype(F32), exp_w_down.astype(F32))
    x_res = h.reshape(t, d).astype(F32)

    xb = h.astype(F32)
    x2d = x_res
    tm = _pick(t, 1024, 512, 256, 128)
    p_t = _matmul_channels_major(w_ch, xb, b_ch, BF16, _pick(hy_cols + o_c - o_b, 1024, 512, 384, 128))
    pm = _matmul_bias(x2d, w_main, b_main, BF16, tm, _pick(w_main.shape[1], 1024, 512, 256, 128))
    p_dt = _matmul_bias(x2d, w_dt, b_dt, F32, tm, 128)
    y_hy = _hyena_mixer(p_t, hy_conv_w.T.astype(F32), col(hy_conv_b), spectra, col(hy_bias),
                        d_hy, p, r, _pick(bsz, 2, 1), BF16)
    cum_t, dt_t = _ssd_prep(p_dt.reshape(bsz, seq, 128), by_group(ssd_dt_bias),
                            by_group(-jnp.exp(ssd_a_log.astype(F32))), hpg)
    y_ssd = _ssd_mixer(pm.reshape(bsz, seq, -1), p_t, hy_cols // SSD_STATE, cum_t, dt_t,
                       ssd_conv_w.astype(F32), row(ssd_conv_b),
                       row(jnp.repeat(ssd_d, SSD_HEADDIM)), row(ssd_norm_w), d_ssm, BF16)
    h1, h1p, logits_t = _merge(y_hy.reshape(t, d_hy), y_ssd.reshape(t, d_ssm), pm,
                               (o_dt - o_z - (o_c - o_b)) // d, x_res, *branch_w,
                               _pick(t, 512, 256, 128))

    sc_rows = V7X_SC_CORES * V7X_SC_SUBCORES * SC_WINDOW
    n_parts = MOE_TOKEN_RANGES if t % (MOE_TOKEN_RANGES * sc_rows) == 0 else 1
    tp = t // n_parts
    tn = _pick(tp, 512, 256, 128)
    w_t, sel_t = _route(logits_t, col(router_bias), tn)
    n_blocks = -(-tp * TOP_K // MOE_BLOCK) + router_w.shape[-1]
    out = None
    for part in range(n_parts):
        cols = slice(part * tp, (part + 1) * tp)
        sel_p, w_p = sel_t[:, cols], w_t[:, cols]
        pos_t, cnt = _rank(sel_p, tn)
        slot_kt, wk = _slots(sel_p, w_p, pos_t, cnt, tn)
        expert, valid = _plan(cnt, n_blocks)
        x_sorted = _sc_scatter_rows(h1p, part * tp, slot_kt, n_blocks * MOE_BLOCK)
        y_sorted = _experts(x_sorted, expert, valid, *experts_w)
        y_tok = _sc_gather_rows(y_sorted, slot_kt)
        out = _combine(y_tok, wk, h1p, h1, part * tp, out, *shared, row(ln2_g), row(ln2_b), tn)
    return out.reshape(bsz, seq, d)


def kernel(x, w_in, b_in, hy_conv_w, hy_conv_b, hy_f_w1, hy_f_b1, hy_f_freq1, hy_f_w2, hy_f_b2,
           hy_f_freq2, hy_f_w3, hy_f_b3, hy_f_freq3, hy_f_w4, hy_bias, ssd_conv_w, ssd_conv_b,
           ssd_dt_bias, ssd_a_log, ssd_d, ssd_norm_w, w_hy_branch, w_ssd_branch, w_out, b_out,
           ln1_g, ln1_b, router_w, router_bias, exp_w_gate, exp_w_up, exp_w_down,
           sh_w_gate, sh_w_up, sh_w_down, ln2_g, ln2_b):
    params = (w_in, b_in, hy_conv_w, hy_conv_b, hy_f_w1, hy_f_b1, hy_f_freq1, hy_f_w2, hy_f_b2,
              hy_f_freq2, hy_f_w3, hy_f_b3, hy_f_freq3, hy_f_w4, hy_bias, ssd_conv_w, ssd_conv_b,
              ssd_dt_bias, ssd_a_log, ssd_d, ssd_norm_w, w_hy_branch, w_ssd_branch, w_out, b_out,
              ln1_g, ln1_b, router_w, router_bias, exp_w_gate, exp_w_up, exp_w_down,
              sh_w_gate, sh_w_up, sh_w_down, ln2_g, ln2_b)
    h = x
    for i in range(DEPTH):
        h = _layer(h, *(p[i] for p in params))
    return h
```

```python
import functools
import math

import jax
import jax.numpy as jnp
from jax import lax
from jax.experimental import pallas as pl
from jax.experimental.pallas import tpu as pltpu
from jax.experimental.pallas import tpu_sc as plsc

F32 = jnp.float32
BF16 = jnp.bfloat16
HI = lax.Precision.HIGHEST

HY_ORDER = 2
HY_EMB_DIM = 33
HY_BANDS = (HY_EMB_DIM - 1) // 2
HY_FAST_DECAY_PCT = 0.3
HY_SLOW_DECAY_PCT = 1.5
HY_DECAY_TARGET = 1e-2
SSD_HEADDIM = 64
SSD_GROUPS = 8
SSD_STATE = 128
SSD_CONV = 5
SSD_CHUNK = 128
N_EXPERT_GROUPS = 8
TOPK_GROUPS = 4
TOP_K = 8
ROUTED_SCALE = 2.5
DEPTH = 1
DN_ALPHA = (2.0 * DEPTH) ** 0.25
LN_EPS = 1e-5
RMS_EPS = 1e-5

V7X_VMEM_LIMIT = 56 * 1024 * 1024


def _params(*sem):
    return pltpu.CompilerParams(dimension_semantics=sem, vmem_limit_bytes=V7X_VMEM_LIMIT)


def _sigmoid(v):
    return 0.5 + 0.5 * jnp.tanh(0.5 * v)


def _silu(v):
    hv = 0.5 * v
    return hv + hv * jnp.tanh(hv)


def _mm_bias_kernel(a_ref, b_ref, bias_ref, o_ref):
    acc = jnp.dot(a_ref[...].astype(b_ref.dtype), b_ref[...], preferred_element_type=F32)
    o_ref[...] = (acc + bias_ref[...]).astype(o_ref.dtype)


def _matmul_bias(a, b, bias, out_dtype, tm, tn):
    m, k = a.shape
    n = b.shape[1]
    return pl.pallas_call(
        _mm_bias_kernel,
        grid=(m // tm, n // tn),
        in_specs=[pl.BlockSpec((tm, k), lambda i, j: (i, 0)),
                  pl.BlockSpec((k, tn), lambda i, j: (0, j)),
                  pl.BlockSpec((1, tn), lambda i, j: (0, j))],
        out_specs=pl.BlockSpec((tm, tn), lambda i, j: (i, j)),
        out_shape=jax.ShapeDtypeStruct((m, n), out_dtype),
        compiler_params=_params("parallel", "parallel"),
        name="proj_rows",
    )(a, b, bias)


def _mm_nt_kernel(w_ref, x_ref, bias_ref, o_ref):
    acc = lax.dot_general(w_ref[...], x_ref[0].astype(w_ref.dtype), (((1,), (1,)), ((), ())),
                          preferred_element_type=F32)
    o_ref[0] = (acc + bias_ref[...]).astype(o_ref.dtype)


def _matmul_channels_major(w_t, x, bias_col, out_dtype, tc):
    c, k = w_t.shape
    bsz, seq, _ = x.shape
    return pl.pallas_call(
        _mm_nt_kernel,
        grid=(bsz, c // tc),
        in_specs=[pl.BlockSpec((tc, k), lambda b, i: (i, 0)),
                  pl.BlockSpec((1, seq, k), lambda b, i: (b, 0, 0)),
                  pl.BlockSpec((tc, 1), lambda b, i: (i, 0))],
        out_specs=pl.BlockSpec((1, tc, seq), lambda b, i: (b, i, 0)),
        out_shape=jax.ShapeDtypeStruct((bsz, c, seq), out_dtype),
        compiler_params=_params("parallel", "parallel"),
        name="proj_channels",
    )(w_t, x, bias_col)


def _hy_mlp_kernel(w1t_ref, b1_ref, f1_ref, w2t_ref, b2_ref, f2_ref, w3t_ref, b3_ref, f3_ref,
                   o_ref, *, seq, emb_rows):
    shape = (emb_rows, 2 * seq)
    lag = lax.broadcasted_iota(jnp.int32, shape, 1) - seq
    pos = jnp.abs(lag).astype(F32)
    row = lax.broadcasted_iota(jnp.int32, shape, 0)
    t = pos / (seq - 1)
    w = (2.0 * math.pi) * pos / seq
    band = jnp.where(row <= HY_BANDS, row - 1, row - 1 - HY_BANDS).astype(F32)
    freq = 1e-4 + band * ((HY_BANDS - 1 - 1e-4) / (HY_BANDS - 1))
    ang = freq * w
    z = jnp.where(row == 0, t,
                  jnp.where(row <= HY_BANDS, jnp.cos(ang),
                            jnp.where(row <= 2 * HY_BANDS, -jnp.sin(ang), 0.0)))
    h = jnp.sin(f1_ref[...] * (jnp.dot(w1t_ref[...], z, precision=HI, preferred_element_type=F32)
                               + b1_ref[...]))
    h = jnp.sin(f2_ref[...] * (jnp.dot(w2t_ref[...], h, precision=HI, preferred_element_type=F32)
                               + b2_ref[...]))
    h = jnp.sin(f3_ref[...] * (jnp.dot(w3t_ref[...], h, precision=HI, preferred_element_type=F32)
                               + b3_ref[...]))
    o_ref[...] = h


def _dft_matrices(p):
    e = lax.broadcasted_iota(jnp.int32, (p, 2 * p), 0)
    j = lax.broadcasted_iota(jnp.int32, (p, 2 * p), 1)
    k = jnp.where(j < p, j, j - p)
    theta = ((e * k) % (2 * p)).astype(F32) * (math.pi / p)
    sign_e = jnp.where(e % 2 == 0, 1.0, -1.0)
    fwd = jnp.where(j < p, jnp.cos(theta), jnp.where(j == p, sign_e, -jnp.sin(theta)))
    jj = lax.broadcasted_iota(jnp.int32, (2 * p, p), 0)
    tt = lax.broadcasted_iota(jnp.int32, (2 * p, p), 1)
    kk = jnp.where(jj < p, jj, jj - p)
    th = ((tt * kk) % (2 * p)).astype(F32) * (math.pi / p)
    sign_t = jnp.where(tt % 2 == 0, 1.0, -1.0)
    n = 2.0 * p
    inv = jnp.where(jj == 0, 1.0 / n,
                    jnp.where(jj < p, (2.0 / n) * jnp.cos(th),
                              jnp.where(jj == p, sign_t / n, (-2.0 / n) * jnp.sin(th))))
    return fwd.astype(F32), inv.astype(F32)


def _hy_spec_kernel(h_ref, w4f_ref, w4b_ref, fwd_hi_ref, fwd_lo_ref, o_ref, *, seq, p, d_hy, rows):
    nb = seq // p
    h = h_ref[...]
    kpos = jnp.dot(w4f_ref[...], h[:, seq:], precision=HI, preferred_element_type=F32)
    kneg = jnp.dot(w4b_ref[...], h[:, :seq], precision=HI, preferred_element_type=F32)
    kb0 = jnp.dot(w4b_ref[...], h[:, seq:seq + 128], precision=HI, preferred_element_type=F32)
    lane128 = lax.broadcasted_iota(jnp.int32, kb0.shape, 1)
    kb0 = jnp.sum(jnp.where(lane128 == 0, kb0, 0.0), axis=1, keepdims=True)

    r0 = pl.program_id(0) * rows
    ch = (r0 + lax.broadcasted_iota(jnp.int32, (rows, 1), 0)) % d_hy
    max_decay = math.log(HY_DECAY_TARGET) / HY_FAST_DECAY_PCT
    min_decay = math.log(HY_DECAY_TARGET) / HY_SLOW_DECAY_PCT
    delta = jnp.abs(min_decay + ch.astype(F32) * ((max_decay - min_decay) / (d_hy - 1)))
    i = lax.broadcasted_iota(jnp.int32, (1, seq), 1)
    t_pos = i.astype(F32) / (seq - 1)
    t_neg = (seq - i).astype(F32) / (seq - 1)
    kpos = (kpos + jnp.where(i == 0, kb0, 0.0)) * jnp.exp(-t_pos * delta)
    kneg = jnp.where(i == 0, 0.0, kneg * jnp.exp(-t_neg * delta))
    klag = jnp.concatenate([kneg, kpos], axis=1)

    fwd_hi = fwd_hi_ref[...]
    fwd_lo = fwd_lo_ref[...]
    lane = lax.broadcasted_iota(jnp.int32, (1, 2 * p), 1)
    sign = jnp.where(lane % 2 == 0, 1.0, -1.0)
    spec = []
    first = []
    for m in range(2 * nb):
        blk = klag[:, m * p:(m + 1) * p]
        b_hi = blk.astype(BF16)
        b_lo = (blk - b_hi.astype(F32)).astype(BF16)
        spec.append(jnp.dot(b_hi, fwd_hi, preferred_element_type=F32)
                    + jnp.dot(b_lo, fwd_hi, preferred_element_type=F32)
                    + jnp.dot(b_hi, fwd_lo, preferred_element_type=F32))
        first.append(blk[:, 0:1])
    half = lax.broadcasted_iota(jnp.int32, (1, p), 1)
    for m in range(1, 2 * nb):
        tail = spec[m - 1] - jnp.where(lane <= p, first[m - 1], 0.0)
        full = spec[m] + sign * tail
        re, im = full[:, :p], full[:, p:]
        o_ref[m - 1, 0] = re.astype(o_ref.dtype)
        o_ref[m - 1, 1] = jnp.where(half == 0, 0.0, im).astype(o_ref.dtype)
        o_ref[m - 1, 2] = jnp.where(half == 0, im, re).astype(o_ref.dtype)


def _hyena_spectra(seq, p, w1, b1, fr1, w2, b2, fr2, w3, b3, fr3, w4, d_hy, rows, spec_dtype):
    hidden = w1.shape[1]
    emb_rows = -(-HY_EMB_DIM // 8) * 8
    w1t = jnp.zeros((hidden, emb_rows), F32).at[:, :HY_EMB_DIM].set(w1.T.astype(F32))
    col = lambda v: v.astype(F32).reshape(hidden, 1)
    h = pl.pallas_call(
        functools.partial(_hy_mlp_kernel, seq=seq, emb_rows=emb_rows),
        out_shape=jax.ShapeDtypeStruct((hidden, 2 * seq), F32),
        name="hyena_filter_mlp",
    )(w1t, col(b1), col(fr1), w2.T.astype(F32), col(b2), col(fr2), w3.T.astype(F32), col(b3), col(fr3))
    w4r = w4.astype(F32).reshape(hidden, HY_ORDER, 2, d_hy)
    w4f = w4r[:, :, 0].reshape(hidden, HY_ORDER * d_hy).T
    w4b = w4r[:, :, 1].reshape(hidden, HY_ORDER * d_hy).T
    fwd, _ = _dft_matrices(p)
    fwd_hi = fwd.astype(BF16)
    fwd_lo = (fwd - fwd_hi.astype(F32)).astype(BF16)
    nb = seq // p
    n_rows = HY_ORDER * d_hy
    return pl.pallas_call(
        functools.partial(_hy_spec_kernel, seq=seq, p=p, d_hy=d_hy, rows=rows),
        grid=(n_rows // rows,),
        in_specs=[pl.BlockSpec((hidden, 2 * seq), lambda r: (0, 0)),
                  pl.BlockSpec((rows, hidden), lambda r: (r, 0)),
                  pl.BlockSpec((rows, hidden), lambda r: (r, 0)),
                  pl.BlockSpec((p, 2 * p), lambda r: (0, 0)),
                  pl.BlockSpec((p, 2 * p), lambda r: (0, 0))],
        out_specs=pl.BlockSpec((2 * nb - 1, 3, rows, p), lambda r: (0, 0, r, 0)),
        out_shape=jax.ShapeDtypeStruct((2 * nb - 1, 3, n_rows, p), spec_dtype),
        compiler_params=_params("parallel"),
        name="hyena_filter_spectra",
    )(h, w4f, w4b, fwd_hi, fwd_lo)


def _hy_conv_kernel(g1_ref, g2_ref, v_ref, w1_ref, w2_ref, wv_ref, b1_ref, b2_ref, bv_ref,
                    k0_ref, k1_ref, s0_ref, s1_ref, fwd_ref, inv_ref, o_ref, *, p):
    bt, r, seq = v_ref.shape
    nb = seq // p
    rows = bt * r
    lane = lax.broadcasted_iota(jnp.int32, (1, 1, seq), 2)

    def short_conv(x_ref, w_ref, b_ref):
        x = x_ref[...].astype(F32)
        x2 = x.reshape(rows, seq)
        prev = jnp.where(lane == 0, 0.0, pltpu.roll(x2, 1, axis=1).reshape(bt, r, seq))
        nxt = jnp.where(lane == seq - 1, 0.0, pltpu.roll(x2, seq - 1, axis=1).reshape(bt, r, seq))
        w = w_ref[...]
        return b_ref[...] + w[:, 0:1] * prev + w[:, 1:2] * x + w[:, 2:3] * nxt

    fwd = fwd_ref[...]
    inv = inv_ref[...]

    def long_conv(u, k_ref):
        ub = u.astype(BF16).reshape(rows, seq)
        spec = [jnp.dot(ub[:, j * p:(j + 1) * p], fwd, preferred_element_type=F32).reshape(bt, r, 2 * p)
                for j in range(nb)]
        outs = []
        for i in range(nb):
            acc_re = jnp.zeros((bt, r, p), F32)
            acc_im = jnp.zeros((bt, r, p), F32)
            for j in range(nb):
                m = i - j + nb - 1
                ka = k_ref[m, 0].astype(F32)
                kb = k_ref[m, 1].astype(F32)
                kc = k_ref[m, 2].astype(F32)
                u_re = spec[j][:, :, :p]
                u_im = spec[j][:, :, p:]
                acc_re = acc_re + ka * u_re - kb * u_im
                acc_im = acc_im + kc * u_im + kb * u_re
            y_spec = jnp.concatenate([acc_re, acc_im], axis=2).astype(BF16).reshape(rows, 2 * p)
            outs.append(jnp.dot(y_spec, inv, preferred_element_type=F32).reshape(bt, r, p))
        return jnp.concatenate(outs, axis=2)

    v = short_conv(v_ref, wv_ref, bv_ref)
    z = short_conv(g1_ref, w1_ref, b1_ref) * (long_conv(v, k0_ref) + v * s0_ref[...])
    y = short_conv(g2_ref, w2_ref, b2_ref) * (long_conv(z, k1_ref) + z * s1_ref[...])
    for b in range(bt):
        o_ref[b] = y[b].T.astype(o_ref.dtype)


def _hyena_mixer(p_t, conv_w_t, conv_b_col, spectra, skip_col, d_hy, p, r, bt, out_dtype):
    bsz, _, seq = p_t.shape
    nb = seq // p
    nblk = d_hy // r
    fwd, inv = _dft_matrices(p)
    x_spec = lambda off: pl.BlockSpec((bt, r, seq), lambda j, b: (b, off * nblk + j, 0))
    w_spec = lambda off: pl.BlockSpec((r, conv_w_t.shape[1]), lambda j, b: (off * nblk + j, 0))
    c_spec = lambda off: pl.BlockSpec((r, 1), lambda j, b: (off * nblk + j, 0))
    k_spec = lambda off: pl.BlockSpec((2 * nb - 1, 3, r, p), lambda j, b: (0, 0, off * nblk + j, 0))
    return pl.pallas_call(
        functools.partial(_hy_conv_kernel, p=p),
        grid=(nblk, bsz // bt),
        in_specs=[x_spec(0), x_spec(1), x_spec(2), w_spec(0), w_spec(1), w_spec(2),
                  c_spec(0), c_spec(1), c_spec(2), k_spec(0), k_spec(1), c_spec(0), c_spec(1),
                  pl.BlockSpec((p, 2 * p), lambda j, b: (0, 0)),
                  pl.BlockSpec((2 * p, p), lambda j, b: (0, 0))],
        out_specs=pl.BlockSpec((bt, seq, r), lambda j, b: (b, 0, j)),
        out_shape=jax.ShapeDtypeStruct((bsz, seq, d_hy), out_dtype),
        compiler_params=_params("parallel", "arbitrary"),
        name="hyena_mixer",
    )(p_t, p_t, p_t, conv_w_t, conv_w_t, conv_w_t, conv_b_col, conv_b_col, conv_b_col,
      spectra, spectra, skip_col, skip_col, fwd.astype(BF16), inv.astype(BF16))


def _ssd_prep_kernel(dt_ref, dtb_ref, a_ref, cum_ref, dtt_ref, *, r):
    seq = dt_ref.shape[1]
    q = SSD_CHUNK
    raw = dt_ref[0] + dtb_ref[...]
    dt = jnp.maximum(raw, 0.0) + jnp.log1p(jnp.exp(-jnp.abs(raw)))
    a = dt * a_ref[...]
    li = lax.broadcasted_iota(jnp.int32, (q, q), 0)
    si = lax.broadcasted_iota(jnp.int32, (q, q), 1)
    lower = jnp.where(li >= si, 1.0, 0.0).astype(F32)
    upper = jnp.where(li <= si, 1.0, 0.0).astype(F32)
    fwd_col = lax.broadcasted_iota(jnp.int32, (1, 128), 1) % (2 * r) < r
    for c in range(seq // q):
        ac = a[c * q:(c + 1) * q]
        cum = jnp.where(fwd_col,
                        jnp.dot(lower, ac, precision=HI, preferred_element_type=F32),
                        jnp.dot(upper, ac, precision=HI, preferred_element_type=F32))
        cum_ref[0, :, c * q:(c + 1) * q] = cum.T
        dtt_ref[0, :, c * q:(c + 1) * q] = dt[c * q:(c + 1) * q].T


def _ssd_prep(p_dt, dt_bias, a_neg, r):
    bsz, seq, _ = p_dt.shape
    vec = pl.BlockSpec((1, 128), lambda b: (0, 0))
    out = pl.BlockSpec((1, 128, seq), lambda b: (b, 0, 0))
    return pl.pallas_call(
        functools.partial(_ssd_prep_kernel, r=r),
        grid=(bsz,),
        in_specs=[pl.BlockSpec((1, seq, 128), lambda b: (b, 0, 0)), vec, vec],
        out_specs=[out, out],
        out_shape=[jax.ShapeDtypeStruct((bsz, 128, seq), F32)] * 2,
        compiler_params=_params("parallel"),
        name="ssd_prep",
    )(p_dt, dt_bias, a_neg)


def _ssd_kernel(z_ref, x_ref, bt_ref, c_ref, cum_ref, dtt_ref, wx_ref, wbt_ref, wc_ref,
                bx_ref, bbt_ref, bc_ref, dskip_ref, nw_ref, o_ref, xm_s, bt_s, cs_s, y_s, st_s):
    nbat, seq, gw = x_ref.shape
    n = bt_ref.shape[1]
    r = gw // SSD_HEADDIM
    q = SSD_CHUNK
    nc = seq // q
    half = SSD_CONV // 2
    shifts = [k - half for k in range(SSD_CONV) if k != half]
    halo_r, halo_l = 16, 128

    k_r = 2 * q
    sr = lax.broadcasted_iota(jnp.int32, (len(shifts) * q, k_r), 0)
    se = lax.broadcasted_iota(jnp.int32, (len(shifts) * q, k_r), 1)
    band_r = jnp.zeros(sr.shape, jnp.bool_)
    for j, sh in enumerate(shifts):
        band_r = jnp.logical_or(band_r, jnp.logical_and(sr // q == j, se == halo_r + sr % q + sh))
    band_r = jnp.where(band_r, 1.0, 0.0).astype(BF16)
    k_l = q + 2 * halo_l
    le = lax.broadcasted_iota(jnp.int32, (k_l, len(shifts) * q), 0)
    lc = lax.broadcasted_iota(jnp.int32, (k_l, len(shifts) * q), 1)
    band_l = jnp.zeros(le.shape, jnp.bool_)
    for j, sh in enumerate(shifts):
        band_l = jnp.logical_or(band_l, jnp.logical_and(lc // q == j, le == halo_l + lc % q + sh))
    band_l = jnp.where(band_l, 1.0, 0.0).astype(BF16)

    col_head = lax.broadcasted_iota(jnp.int32, (1, gw), 1) // SSD_HEADDIM
    wx, wc, wbt = wx_ref[...], wc_ref[...], wbt_ref[...]

    def conv_rows(ref, bb, c, off, w, bias):
        width = ref.shape[2]
        cur = ref[bb, pl.ds(off, q), :]
        prev = ref[bb, pl.ds(pl.multiple_of(jnp.maximum(off - halo_r, 0), halo_r), halo_r), :]
        nxt = ref[bb, pl.ds(pl.multiple_of(jnp.minimum(off + q, seq - halo_r), halo_r), halo_r), :]
        prev = jnp.where(c > 0, prev, jnp.zeros_like(prev))
        nxt = jnp.where(c < nc - 1, nxt, jnp.zeros_like(nxt))
        ext = jnp.concatenate([prev, cur, nxt, jnp.zeros((k_r - q - 2 * halo_r, width), BF16)], axis=0)
        moved = jnp.dot(band_r, ext, preferred_element_type=F32)
        acc = bias + w[half:half + 1] * cur.astype(F32)
        for j, sh in enumerate(shifts):
            acc = acc + w[half + sh:half + sh + 1] * moved[j * q:(j + 1) * q]
        return _silu(acc)

    def conv_one(bb, c, off):
        xs = conv_rows(x_ref, bb, c, off, wx, bx_ref[...])
        y_s[bb, 0, pl.ds(off, q), :] = xs * dskip_ref[...]
        for h in range(r):
            xm_s[bb, h, pl.ds(off, q), :] = jnp.where(col_head == h, xs, 0.0).astype(BF16)
        cs_s[bb, pl.ds(off, q), :] = conv_rows(c_ref, bb, c, off, wc, bc_ref[...]).astype(BF16)
        cur = bt_ref[bb, :, pl.ds(off, q)]
        prev = bt_ref[bb, :, pl.ds(pl.multiple_of(jnp.maximum(off - halo_l, 0), halo_l), halo_l)]
        nxt = bt_ref[bb, :, pl.ds(pl.multiple_of(jnp.minimum(off + q, seq - halo_l), halo_l), halo_l)]
        prev = jnp.where(c > 0, prev, jnp.zeros_like(prev))
        nxt = jnp.where(c < nc - 1, nxt, jnp.zeros_like(nxt))
        moved = jnp.dot(jnp.concatenate([prev, cur, nxt], axis=1), band_l, preferred_element_type=F32)
        acc = bbt_ref[...] + wbt[:, half:half + 1] * cur.astype(F32)
        for j, sh in enumerate(shifts):
            acc = acc + wbt[:, half + sh:half + sh + 1] * moved[:, j * q:(j + 1) * q]
        bt_s[bb, :, pl.ds(off, q)] = _silu(acc).astype(BF16)

    def conv_step(c, carry):
        off = pl.multiple_of(c * q, q)
        for bb in range(nbat):
            conv_one(bb, c, off)
        return carry

    lax.fori_loop(0, nc, conv_step, 0, unroll=4)
    st_s[...] = jnp.zeros_like(st_s)

    li = lax.broadcasted_iota(jnp.int32, (q, q), 0)
    si = lax.broadcasted_iota(jnp.int32, (q, q), 1)
    diag = li == si
    sub = lax.broadcasted_iota(jnp.int32, (2 * r, 1), 0)
    neg_inf = jnp.float32(-jnp.inf)
    nh = 2 * r
    assert 3 * nh + 16 <= q
    ones_rows = jnp.ones((8, q), F32)
    pad_rows = jnp.zeros((q - 3 * nh - 16, q), F32)
    pk = lax.broadcasted_iota(jnp.int32, (3 * nh + 8, nh * q), 0)
    pj = lax.broadcasted_iota(jnp.int32, (3 * nh + 8, nh * q), 1) // q
    seg_top = jnp.where(jnp.logical_and(pk < 3 * nh, pk % nh == pj), 1.0, 0.0).astype(BF16)
    seg_bottom = jnp.zeros((q - 3 * nh - 8 - 16, nh * q), BF16)

    def split3(v):
        hi = v.astype(BF16).astype(F32)
        mid = (v - hi).astype(BF16).astype(F32)
        lo = (v - hi - mid).astype(BF16).astype(F32)
        return hi, mid, lo

    def flat(v):
        return jnp.concatenate([v[j:j + 1, :] for j in range(nh)], axis=1)

    def chunk_step(i, carry):
        offs = (pl.multiple_of(i * q, q), pl.multiple_of((nc - 1 - i) * q, q))
        chains = [(bb, d) for bb in range(nbat) for d in range(2)]
        pre = []
        for bb in range(nbat):
            cum = jnp.where(sub < r, cum_ref[bb, :, pl.ds(offs[0], q)], cum_ref[bb, :, pl.ds(offs[1], q)])
            dt = jnp.where(sub < r, dtt_ref[bb, :, pl.ds(offs[0], q)], dtt_ref[bb, :, pl.ds(offs[1], q)])
            total = jnp.where(sub < r, cum[:, q - 1:q], cum[:, 0:1])
            hi, mid, lo = split3(cum)
            left = jnp.concatenate([hi, mid, lo, jnp.zeros((8, q), F32), ones_rows, pad_rows], axis=0)
            left = left.T.astype(BF16)
            minus = jnp.concatenate([-flat(hi), -flat(mid), -flat(lo), jnp.zeros((13, nh * q), F32)],
                                    axis=0)
            right = jnp.concatenate([seg_top, minus.astype(BF16), seg_bottom], axis=0)
            pre.append(dict(dt=dt, carry_decay=jnp.exp(total), in_decay=jnp.exp(cum),
                            out_w=jnp.exp(total - cum) * dt, left=left, right=right))
        seg_all = [jnp.dot(p_["left"], p_["right"], preferred_element_type=F32) for p_ in pre]
        btc = {c: bt_s[c[0], :, pl.ds(offs[c[1]], q)] for c in chains}
        cc = {c: cs_s[c[0], pl.ds(offs[c[1]], q), :] for c in chains}
        state = {c: st_s[c[0], c[1]] for c in chains}
        cb = {c: jnp.dot(cc[c], btc[c], preferred_element_type=F32) for c in chains}
        c_state = {c: jnp.dot(cc[c], state[c].astype(BF16), preferred_element_type=F32) for c in chains}
        lhs_y, rhs_y, lhs_s, rhs_s, decays = {}, {}, {}, {}, {}
        for c in chains:
            bb, d = c
            p_ = pre[bb]
            tri = (li >= si) if d == 0 else (li <= si)
            btf = btc[c].astype(F32)
            mats, diags, xs_h, cs_h, b_scaled = [], [], [], [], []
            state_decay = jnp.zeros((1, gw), F32)
            for h in range(r):
                ln = d * r + h
                seg = seg_all[bb][:, ln * q:(ln + 1) * q]
                mats.append((cb[c] * jnp.exp(jnp.where(tri, seg, neg_inf)) * p_["dt"][ln:ln + 1, :])
                            .astype(BF16))
                diags.append(jnp.where(diag, p_["in_decay"][ln:ln + 1, :], 0.0).astype(BF16))
                xs_h.append(xm_s[bb, h, pl.ds(offs[d], q), :])
                cs_h.append(jnp.where(col_head == h, c_state[c], 0.0).astype(BF16))
                b_scaled.append((btf * p_["out_w"][ln:ln + 1, :]).astype(BF16))
                state_decay = jnp.where(col_head == h, p_["carry_decay"][ln:ln + 1, :], state_decay)
            lhs_y[c] = jnp.concatenate(mats + diags, axis=1)
            rhs_y[c] = jnp.concatenate(xs_h + cs_h, axis=0)
            lhs_s[c] = jnp.concatenate(b_scaled, axis=1)
            rhs_s[c] = jnp.concatenate(xs_h, axis=0)
            decays[c] = state_decay
        y_new = {c: jnp.dot(lhs_y[c], rhs_y[c], preferred_element_type=F32) for c in chains}
        upd = {c: jnp.dot(lhs_s[c], rhs_s[c], preferred_element_type=F32) for c in chains}
        for c in chains:
            bb, d = c
            y_s[bb, 1 + d, pl.ds(offs[d], q), :] = y_new[c]
            st_s[bb, d] = state[c] * decays[c] + upd[c]
        return carry

    lax.fori_loop(0, nc, chunk_step, 0, unroll=8)

    def gate_step(c, carry):
        off = pl.multiple_of(c * q, q)
        for bb in range(nbat):
            zf = z_ref[bb, pl.ds(off, q), :].astype(F32)
            y = (y_s[bb, 0, pl.ds(off, q), :] + y_s[bb, 1, pl.ds(off, q), :]
                 + y_s[bb, 2, pl.ds(off, q), :])
            y = y * _silu(zf)
            y = y * lax.rsqrt(jnp.mean(jnp.square(y), axis=-1, keepdims=True) + RMS_EPS)
            o_ref[bb, pl.ds(off, q), :] = (y * nw_ref[...]).astype(o_ref.dtype)
        return carry

    lax.fori_loop(0, nc, gate_step, 0, unroll=4)


def _ssd_mixer(pm, p_t, bt_off, cum_t, dt_t, conv_w, conv_b, dskip_cols, norm_w, d_ssm, out_dtype):
    bsz, seq, _ = pm.shape
    gw = d_ssm // SSD_GROUPS
    r = gw // SSD_HEADDIM
    n = SSD_STATE
    xo, co = d_ssm // gw, 2 * d_ssm // n
    cw = lambda width, off: pl.BlockSpec((SSD_CONV, width), lambda b, g: (0, off + g))
    cb = lambda width, off: pl.BlockSpec((1, width), lambda b, g: (0, off + g))
    nbat = _pick(bsz, 2, 1)
    heads_rows = pl.BlockSpec((nbat, 2 * r, seq), lambda b, g: (b, g, 0))
    return pl.pallas_call(
        _ssd_kernel,
        grid=(bsz // nbat, SSD_GROUPS),
        in_specs=[pl.BlockSpec((nbat, seq, gw), lambda b, g: (b, 0, g)),
                  pl.BlockSpec((nbat, seq, gw), lambda b, g: (b, 0, xo + g)),
                  pl.BlockSpec((nbat, n, seq), lambda b, g: (b, bt_off + g, 0)),
                  pl.BlockSpec((nbat, seq, n), lambda b, g: (b, 0, co + g)),
                  heads_rows, heads_rows,
                  cw(gw, 0),
                  pl.BlockSpec((n, SSD_CONV), lambda b, g: (d_ssm // n + g, 0)),
                  cw(n, d_ssm // n + SSD_GROUPS),
                  cb(gw, 0),
                  pl.BlockSpec((n, 1), lambda b, g: (d_ssm // n + g, 0)),
                  cb(n, d_ssm // n + SSD_GROUPS),
                  cb(gw, 0), cb(gw, 0)],
        out_specs=pl.BlockSpec((nbat, seq, gw), lambda b, g: (b, 0, g)),
        out_shape=jax.ShapeDtypeStruct((bsz, seq, d_ssm), out_dtype),
        scratch_shapes=[pltpu.VMEM((nbat, r, seq, gw), BF16),
                        pltpu.VMEM((nbat, n, seq), BF16), pltpu.VMEM((nbat, seq, n), BF16),
                        pltpu.VMEM((nbat, 3, seq, gw), F32), pltpu.VMEM((nbat, 2, n, gw), F32)],
        compiler_params=_params("parallel", "parallel"),
        name="ssd_mixer",
    )(pm, pm, p_t, pm, cum_t, dt_t, conv_w, conv_w.T, conv_w, conv_b, conv_b.reshape(-1, 1), conv_b,
      dskip_cols, norm_w)


def _layer_norm(v, g, b):
    mu = jnp.mean(v, axis=-1, keepdims=True)
    c = v - mu
    var = jnp.mean(jnp.square(c), axis=-1, keepdims=True)
    return c * lax.rsqrt(var + LN_EPS) * g + b


def _merge_kernel(yh_ref, ys_ref, gh_ref, gs_ref, x_ref, whb_ref, wsb_ref, wo_ref, bo_ref,
                  lg_ref, lb_ref, rwt_ref, h_ref, hp_ref, lt_ref):
    a = jnp.dot(yh_ref[...], whb_ref[...], preferred_element_type=F32)
    b = jnp.dot(ys_ref[...], wsb_ref[...], preferred_element_type=F32)
    m = _sigmoid(gh_ref[...].astype(F32)) * a + _sigmoid(gs_ref[...].astype(F32)) * b
    mix = jnp.dot(m.astype(BF16), wo_ref[...], preferred_element_type=F32) + bo_ref[...]
    h = _layer_norm(DN_ALPHA * x_ref[...] + mix, lg_ref[...], lb_ref[...])
    h_ref[...] = h
    hp_ref[...] = _pack_halves(h)
    lt_ref[...] = lax.dot_general(rwt_ref[...], h, (((1,), (1,)), ((), ())), precision=HI,
                                  preferred_element_type=F32)


def _merge(yhy, yssd, pm, gate_off, x2d, whb, wsb, wo, bo, lg, lb, rwt, tm):
    t, d = x2d.shape
    c = yhy.shape[1]
    d_ssm = yssd.shape[1]
    e = rwt.shape[0]
    row = lambda w: pl.BlockSpec((tm, w), lambda i: (i, 0))
    whole = lambda a: pl.BlockSpec(a.shape, lambda i: (0,) * a.ndim)
    return pl.pallas_call(
        _merge_kernel,
        grid=(t // tm,),
        in_specs=[row(c), row(d_ssm),
                  pl.BlockSpec((tm, d), lambda i: (i, gate_off)),
                  pl.BlockSpec((tm, d), lambda i: (i, gate_off + 1)),
                  row(d), whole(whb), whole(wsb), whole(wo), whole(bo), whole(lg), whole(lb), whole(rwt)],
        out_specs=[row(d), row(d // 2), pl.BlockSpec((e, tm), lambda i: (0, i))],
        out_shape=[jax.ShapeDtypeStruct((t, d), F32), jax.ShapeDtypeStruct((t, d // 2), jnp.int32),
                   jax.ShapeDtypeStruct((e, t), F32)],
        compiler_params=_params("parallel"),
        name="merge_ln",
    )(yhy, yssd, pm, pm, x2d, whb, wsb, wo, bo, lg, lb, rwt)


def _first_max(v, axis, size):
    idx = lax.broadcasted_iota(jnp.int32, v.shape, axis)
    top = jnp.max(v, axis=axis, keepdims=True)
    first = jnp.min(jnp.where(v == top, idx, size), axis=axis, keepdims=True)
    return idx == first, top


def _route_kernel(lt_ref, bias_ref, w_ref, sel_ref):
    n_e, tn = lt_ref.shape
    per = n_e // N_EXPERT_GROUPS
    scores = jax.nn.sigmoid(lt_ref[...])
    biased = scores + bias_ref[...]
    b3 = biased.reshape(N_EXPERT_GROUPS, per, tn)
    hit, top1 = _first_max(b3, 1, per)
    top2 = jnp.max(jnp.where(hit, -jnp.inf, b3), axis=1, keepdims=True)
    cur = (top1 + top2).reshape(N_EXPERT_GROUPS, tn)
    grp = jnp.zeros(cur.shape, jnp.bool_)
    for _ in range(TOPK_GROUPS):
        hit, _ = _first_max(cur, 0, N_EXPERT_GROUPS)
        grp = jnp.logical_or(grp, hit)
        cur = jnp.where(hit, -jnp.inf, cur)
    cur = jnp.where(grp.reshape(N_EXPERT_GROUPS, 1, tn), b3, -jnp.inf).reshape(n_e, tn)
    chosen = jnp.zeros(cur.shape, jnp.bool_)
    for _ in range(TOP_K):
        hit, _ = _first_max(cur, 0, n_e)
        chosen = jnp.logical_or(chosen, hit)
        cur = jnp.where(hit, -jnp.inf, cur)
    w = jnp.where(chosen, scores, 0.0)
    w_ref[...] = w / jnp.sum(w, axis=0, keepdims=True) * ROUTED_SCALE
    sel_ref[...] = jnp.where(chosen, 1.0, 0.0).astype(sel_ref.dtype)


def _route(logits_t, bias_col, tn):
    e, t = logits_t.shape
    blk = pl.BlockSpec((e, tn), lambda i: (0, i))
    return pl.pallas_call(
        _route_kernel,
        grid=(t // tn,),
        in_specs=[blk, pl.BlockSpec((e, 1), lambda i: (0, 0))],
        out_specs=[blk, blk],
        out_shape=[jax.ShapeDtypeStruct((e, t), F32), jax.ShapeDtypeStruct((e, t), BF16)],
        compiler_params=_params("parallel"),
        name="route_topk",
    )(logits_t, bias_col)


MOE_BLOCK = 512
MOE_TOKEN_RANGES = 2


def _rank_kernel(sel_ref, pos_ref, cnt_ref, carry_ref):
    n_e, tn = sel_ref.shape

    @pl.when(pl.program_id(0) == 0)
    def _():
        carry_ref[...] = jnp.zeros_like(carry_ref)

    sel = sel_ref[...]
    s = lax.broadcasted_iota(jnp.int32, (tn, tn), 0)
    t = lax.broadcasted_iota(jnp.int32, (tn, tn), 1)
    earlier = jnp.where(s < t, 1.0, 0.0).astype(BF16)
    before = jnp.dot(sel, earlier, preferred_element_type=F32)
    carry = carry_ref[...]
    pos_ref[...] = carry[:, 0:1] + before
    carry = carry + jnp.sum(sel.astype(F32), axis=1, keepdims=True)
    carry_ref[...] = carry
    cnt_ref[...] = carry


def _rank(sel_t, tn):
    e, t = sel_t.shape
    return pl.pallas_call(
        _rank_kernel,
        grid=(t // tn,),
        in_specs=[pl.BlockSpec((e, tn), lambda i: (0, i))],
        out_specs=[pl.BlockSpec((e, tn), lambda i: (0, i)), pl.BlockSpec((e, 128), lambda i: (0, 0))],
        out_shape=[jax.ShapeDtypeStruct((e, t), F32), jax.ShapeDtypeStruct((e, 128), F32)],
        scratch_shapes=[pltpu.VMEM((e, 128), F32)],
        compiler_params=_params("arbitrary"),
        name="moe_rank",
    )(sel_t)


def _segment_starts(cnt):
    n_e = cnt.shape[0]
    padded = jnp.floor((cnt + (MOE_BLOCK - 1)) * (1.0 / MOE_BLOCK)) * MOE_BLOCK
    a = lax.broadcasted_iota(jnp.int32, (n_e, n_e), 0)
    b = lax.broadcasted_iota(jnp.int32, (n_e, n_e), 1)
    below = jnp.where(b < a, 1.0, 0.0).astype(F32)
    return padded, jnp.dot(below, padded, precision=HI, preferred_element_type=F32)


def _slots_kernel(sel_ref, w_ref, pos_ref, cnt_ref, slot_ref, wk_ref):
    n_e, tn = sel_ref.shape
    _, start = _segment_starts(cnt_ref[...])
    slot = start[:, 0:1] + pos_ref[...]
    w = w_ref[...]
    cur = sel_ref[...].astype(F32)
    idx = lax.broadcasted_iota(jnp.int32, (n_e, tn), 0)
    slots, weights = [], []
    for _ in range(TOP_K):
        first = jnp.min(jnp.where(cur > 0.0, idx, n_e), axis=0, keepdims=True)
        hit = idx == first
        slots.append(jnp.sum(jnp.where(hit, slot, 0.0), axis=0, keepdims=True))
        weights.append(jnp.sum(jnp.where(hit, w, 0.0), axis=0, keepdims=True))
        cur = jnp.where(hit, 0.0, cur)
    slot_ref[...] = jnp.concatenate(slots, axis=0).astype(jnp.int32)
    wk_ref[...] = jnp.concatenate(weights, axis=0).T


def _slots(sel_t, w_t, pos_t, cnt, tn):
    e, t = sel_t.shape
    blk = pl.BlockSpec((e, tn), lambda i: (0, i))
    return pl.pallas_call(
        _slots_kernel,
        grid=(t // tn,),
        in_specs=[blk, blk, blk, pl.BlockSpec((e, 128), lambda i: (0, 0))],
        out_specs=[pl.BlockSpec((TOP_K, tn), lambda i: (0, i)), pl.BlockSpec((tn, TOP_K), lambda i: (i, 0))],
        out_shape=[jax.ShapeDtypeStruct((TOP_K, t), jnp.int32), jax.ShapeDtypeStruct((t, TOP_K), F32)],
        compiler_params=_params("parallel"),
        name="moe_slots",
    )(sel_t, w_t, pos_t, cnt)


def _plan_kernel(cnt_ref, expert_ref, valid_ref):
    n_e = cnt_ref.shape[0]
    nbp = expert_ref.shape[1]
    cnt = cnt_ref[...]
    padded, start = _segment_starts(cnt)
    end = (start + padded)[:, 0:1]
    last = (start + cnt)[:, 0:1]
    row0 = (lax.broadcasted_iota(jnp.int32, (1, nbp), 1) * MOE_BLOCK).astype(F32)
    expert = jnp.minimum(jnp.sum(jnp.where(end <= row0, 1.0, 0.0), axis=0, keepdims=True), n_e - 1.0)
    eid = lax.broadcasted_iota(jnp.int32, (n_e, nbp), 0).astype(F32)
    last_b = jnp.sum(jnp.where(eid == expert, last, 0.0), axis=0, keepdims=True)
    expert_ref[...] = expert.astype(jnp.int32)
    valid_ref[...] = jnp.clip(last_b - row0, 0.0, float(MOE_BLOCK)).astype(jnp.int32)


def _plan(cnt, n_blocks):
    nbp = -(-n_blocks // 128) * 128
    return pl.pallas_call(
        _plan_kernel,
        out_shape=[jax.ShapeDtypeStruct((1, nbp), jnp.int32)] * 2,
        name="moe_plan",
    )(cnt)


V7X_SC_CORES = 2
V7X_SC_SUBCORES = 16
SC_WINDOW = 128


def _sc_mesh():
    return plsc.VectorSubcoreMesh(core_axis_name="c", subcore_axis_name="s",
                                  num_cores=V7X_SC_CORES, num_subcores=V7X_SC_SUBCORES)


def _sc_scatter_rows(rows, row0, idx, n_out):
    width = rows.shape[1]
    k_n, t = idx.shape
    workers = V7X_SC_CORES * V7X_SC_SUBCORES
    per_worker = t // workers
    assert per_worker * workers == t and per_worker % SC_WINDOW == 0

    def body(rows_hbm, idx_hbm, out_hbm, idx_v, rows_v, sem):
        wid = lax.axis_index("s") * V7X_SC_CORES + lax.axis_index("c")

        @pl.loop(0, per_worker // SC_WINDOW)
        def _(j):
            base = wid * per_worker + j * SC_WINDOW
            pltpu.sync_copy(rows_hbm.at[pl.ds(row0 + base, SC_WINDOW)], rows_v)
            for k in range(k_n):
                pltpu.sync_copy(idx_hbm.at[pl.ds(k * t + base, SC_WINDOW)], idx_v.at[k])
            copies = [pltpu.async_copy(rows_v, out_hbm.at[idx_v.at[k]], sem) for k in range(k_n)]
            for copy in copies:
                copy.wait()

    return pl.kernel(
        body, out_type=jax.ShapeDtypeStruct((n_out, width), rows.dtype), mesh=_sc_mesh(),
        scratch_types=[pltpu.VMEM((k_n, SC_WINDOW), jnp.int32), pltpu.VMEM((SC_WINDOW, width), rows.dtype),
                       pltpu.SemaphoreType.DMA],
        name="moe_dispatch",
    )(rows, idx.reshape(-1))


def _sc_gather_rows(table, idx):
    width = table.shape[1]
    k_n, t = idx.shape
    workers = V7X_SC_CORES * V7X_SC_SUBCORES
    per_worker = t // workers
    assert per_worker * workers == t and per_worker % SC_WINDOW == 0

    def body(table_hbm, idx_hbm, out_hbm, idx_v, rows_v, sem):
        wid = lax.axis_index("s") * V7X_SC_CORES + lax.axis_index("c")

        @pl.loop(0, per_worker // SC_WINDOW)
        def _(j):
            base = wid * per_worker + j * SC_WINDOW
            for k in range(k_n):
                pltpu.sync_copy(idx_hbm.at[pl.ds(k * t + base, SC_WINDOW)], idx_v)
                pltpu.async_copy(table_hbm.at[idx_v], rows_v, sem).wait()
                pltpu.sync_copy(rows_v, out_hbm.at[pl.ds(k * t + base, SC_WINDOW)])

    out = pl.kernel(
        body, out_type=jax.ShapeDtypeStruct((k_n * t, width), table.dtype), mesh=_sc_mesh(),
        scratch_types=[pltpu.VMEM((SC_WINDOW,), jnp.int32), pltpu.VMEM((SC_WINDOW, width), table.dtype),
                       pltpu.SemaphoreType.DMA],
        name="moe_collect",
    )(table, idx.reshape(-1))
    return out.reshape(k_n, t, width)


def _swiglu(xb, wg, wu):
    gate = jnp.dot(xb, wg, preferred_element_type=F32)
    up = jnp.dot(xb, wu, preferred_element_type=F32)
    return _silu(gate) * up


def _pack_halves(v):
    w = v.shape[1] // 2
    hi = pltpu.bitcast(v[:, :w].astype(BF16).astype(F32), jnp.int32)
    lo = pltpu.bitcast(v[:, w:].astype(BF16).astype(F32), jnp.int32)
    return jnp.bitwise_or(hi, lax.shift_right_logical(lo, 16))


def _unpack_halves(p):
    hi = pltpu.bitcast(jnp.bitwise_and(p, jnp.int32(-65536)), F32)
    lo = pltpu.bitcast(lax.shift_left(p, 16), F32)
    return jnp.concatenate([hi, lo], axis=1)


def _experts_kernel(expert_ref, valid_ref, x_ref, wg_ref, wu_ref, wd_ref, y_ref, wg_s, wu_s, wd_s):
    b = pl.program_id(0)
    valid = valid_ref[0, b]
    fresh = jnp.logical_or(b == 0, expert_ref[0, b] != expert_ref[0, jnp.maximum(b - 1, 0)])

    @pl.when(jnp.logical_and(valid > 0, fresh))
    def _():
        wg_s[...] = wg_ref[0].astype(BF16)
        wu_s[...] = wu_ref[0].astype(BF16)
        wd_s[...] = wd_ref[0].astype(BF16)

    @pl.when(valid > 0)
    def _():
        rows = lax.broadcasted_iota(jnp.int32, (x_ref.shape[0], 1), 0)
        x = jnp.where(rows < valid, _unpack_halves(x_ref[...]), 0.0).astype(BF16)
        mid = _swiglu(x, wg_s[...], wu_s[...]).astype(BF16)
        y_ref[...] = _pack_halves(jnp.dot(mid, wd_s[...], preferred_element_type=F32))

    @pl.when(valid == 0)
    def _():
        y_ref[...] = jnp.zeros_like(y_ref)


def _experts(x_sorted, expert, valid, wg, wu, wd):
    n_rows, half = x_sorted.shape
    n_e, d, f = wg.shape
    by_expert = lambda shape: pl.BlockSpec((1,) + shape, lambda b, e_ref, v_ref: (e_ref[0, b], 0, 0))
    rows = pl.BlockSpec((MOE_BLOCK, half), lambda b, e_ref, v_ref: (b, 0))
    return pl.pallas_call(
        _experts_kernel,
        grid_spec=pltpu.PrefetchScalarGridSpec(
            num_scalar_prefetch=2, grid=(n_rows // MOE_BLOCK,),
            in_specs=[rows, by_expert((d, f)), by_expert((d, f)), by_expert((f, d))],
            out_specs=rows,
            scratch_shapes=[pltpu.VMEM((d, f), BF16), pltpu.VMEM((d, f), BF16), pltpu.VMEM((f, d), BF16)]),
        out_shape=jax.ShapeDtypeStruct((n_rows, half), jnp.int32),
        compiler_params=_params("arbitrary"),
        name="moe_experts",
    )(expert, valid, x_sorted, wg, wu, wd)


def _combine_kernel(y_ref, wk_ref, hp_ref, h_ref, sg_ref, su_ref, sd_ref, lg_ref, lb_ref, o_ref):
    wk = wk_ref[...]
    mid = _swiglu(_unpack_halves(hp_ref[...]).astype(BF16), sg_ref[...], su_ref[...]).astype(BF16)
    acc = jnp.dot(mid, sd_ref[...], preferred_element_type=F32)
    for k in range(y_ref.shape[0]):
        acc = acc + wk[:, k:k + 1] * _unpack_halves(y_ref[k])
    o_ref[...] = _layer_norm(DN_ALPHA * h_ref[...] + acc, lg_ref[...], lb_ref[...])


def _combine_kernel_into(y_ref, wk_ref, hp_ref, h_ref, sg_ref, su_ref, sd_ref, lg_ref, lb_ref, prev_ref,
                         o_ref):
    del prev_ref
    _combine_kernel(y_ref, wk_ref, hp_ref, h_ref, sg_ref, su_ref, sd_ref, lg_ref, lb_ref, o_ref)


def _combine(y_tok, wk, hp, h, row0, prev, sg, su, sd, lg, lb, tm):
    t, d = h.shape
    k_n, t_part, half = y_tok.shape
    b0 = row0 // tm
    part = lambda w: pl.BlockSpec((tm, w), lambda i: (i, 0))
    full = lambda w: pl.BlockSpec((tm, w), lambda i: (b0 + i, 0))
    whole = lambda a: pl.BlockSpec(a.shape, lambda i: (0,) * a.ndim)
    in_specs = [pl.BlockSpec((k_n, tm, half), lambda i: (0, i, 0)), part(k_n), full(half), full(d),
                whole(sg), whole(su), whole(sd), whole(lg), whole(lb)]
    args = [y_tok, wk, hp, h, sg, su, sd, lg, lb]
    if prev is not None:
        in_specs.append(pl.BlockSpec(memory_space=pl.ANY))
        args.append(prev)
    return pl.pallas_call(
        _combine_kernel if prev is None else _combine_kernel_into,
        grid=(t_part // tm,),
        in_specs=in_specs,
        out_specs=full(d),
        out_shape=jax.ShapeDtypeStruct((t, d), F32),
        input_output_aliases={} if prev is None else {len(args) - 1: 0},
        compiler_params=_params("parallel"),
        name="moe_combine_ln",
    )(*args)


def _pick(n, *cands):
    for c in cands:
        if n % c == 0:
            return c
    return n


def _layer(h, w_in, b_in, hy_conv_w, hy_conv_b, hy_f_w1, hy_f_b1, hy_f_freq1, hy_f_w2, hy_f_b2,
           hy_f_freq2, hy_f_w3, hy_f_b3, hy_f_freq3, hy_f_w4, hy_bias, ssd_conv_w, ssd_conv_b,
           ssd_dt_bias, ssd_a_log, ssd_d, ssd_norm_w, w_hy_branch, w_ssd_branch, w_out, b_out,
           ln1_g, ln1_b, router_w, router_bias, exp_w_gate, exp_w_up, exp_w_down,
           sh_w_gate, sh_w_up, sh_w_down, ln2_g, ln2_b):
    bsz, seq, d = h.shape
    t = bsz * seq
    d_hy = hy_bias.shape[-1]
    d_ssm = ssd_norm_w.shape[-1]
    conv_dim = ssd_conv_w.shape[-1]
    heads = ssd_d.shape[-1]
    hy_cols = (HY_ORDER + 1) * d_hy
    o_z, o_dt, o_gate = hy_cols, hy_cols + d_ssm + conv_dim, hy_cols + d_ssm + conv_dim + 2 * heads
    o_b = o_z + 2 * d_ssm
    o_c = o_b + SSD_GROUPS * SSD_STATE
    row = lambda v: v.astype(F32).reshape(1, -1)
    col = lambda v: v.astype(F32).reshape(-1, 1)

    ch = lambda v: jnp.concatenate([v[..., :o_z], v[..., o_b:o_c]], axis=-1)
    w_ch, b_ch = ch(w_in).T.astype(BF16), col(ch(b_in))
    tmaj = lambda v: jnp.concatenate([v[..., o_z:o_b], v[..., o_c:o_dt], v[..., o_gate:]], axis=-1)
    w_main, b_main = tmaj(w_in).astype(BF16), row(tmaj(b_in))
    pad = 128 - 2 * heads
    hpg = heads // SSD_GROUPS
    perm = [dr * heads + g * hpg + hh for g in range(SSD_GROUPS) for dr in range(2) for hh in range(hpg)]
    by_group = lambda v: jnp.pad(v.astype(F32).reshape(-1)[jnp.array(perm)], (0, pad)).reshape(1, -1)
    w_dt = jnp.pad(w_in[:, o_dt:o_gate][:, jnp.array(perm)], ((0, 0), (0, pad))).astype(BF16)
    b_dt = by_group(b_in[o_dt:o_gate])
    p = _pick(seq, 512, 256, 128, 64)
    r = _pick(d_hy, 128, 64)
    spectra = _hyena_spectra(seq, p, hy_f_w1, hy_f_b1, hy_f_freq1, hy_f_w2, hy_f_b2, hy_f_freq2,
                             hy_f_w3, hy_f_b3, hy_f_freq3, hy_f_w4, d_hy, r, BF16)
    branch_w = (w_hy_branch.astype(BF16), w_ssd_branch.astype(BF16), w_out.astype(BF16),
                row(b_out), row(ln1_g), row(ln1_b), router_w.T.astype(F32))
    shared = (sh_w_gate.astype(BF16), sh_w_up.astype(BF16), sh_w_down.astype(BF16))
    experts_w = (exp_w_gate.astype(F32), exp_w_up.astype(F32), exp_w_down.astype(F32))
    x_res = h.reshape(t, d).astype(F32)

    xb = h.astype(F32)
    x2d = x_res
    tm = _pick(t, 1024, 512, 256, 128)
    p_t = _matmul_channels_major(w_ch, xb, b_ch, BF16, _pick(hy_cols + o_c - o_b, 1024, 512, 384, 128))
    pm = _matmul_bias(x2d, w_main, b_main, BF16, tm, _pick(w_main.shape[1], 1024, 512, 256, 128))
    p_dt = _matmul_bias(x2d, w_dt, b_dt, F32, tm, 128)
    y_hy = _hyena_mixer(p_t, hy_conv_w.T.astype(F32), col(hy_conv_b), spectra, col(hy_bias),
                        d_hy, p, r, _pick(bsz, 2, 1), BF16)
    cum_t, dt_t = _ssd_prep(p_dt.reshape(bsz, seq, 128), by_group(ssd_dt_bias),
                            by_group(-jnp.exp(ssd_a_log.astype(F32))), hpg)
    y_ssd = _ssd_mixer(pm.reshape(bsz, seq, -1), p_t, hy_cols // SSD_STATE, cum_t, dt_t,
                       ssd_conv_w.astype(F32), row(ssd_conv_b),
                       row(jnp.repeat(ssd_d, SSD_HEADDIM)), row(ssd_norm_w), d_ssm, BF16)
    h1, h1p, logits_t = _merge(y_hy.reshape(t, d_hy), y_ssd.reshape(t, d_ssm), pm,
                               (o_dt - o_z - (o_c - o_b)) // d, x_res, *branch_w,
                               _pick(t, 512, 256, 128))

    sc_rows = V7X_SC_CORES * V7X_SC_SUBCORES * SC_WINDOW
    n_parts = MOE_TOKEN_RANGES if t % (MOE_TOKEN_RANGES * sc_rows) == 0 else 1
    tp = t // n_parts
    tn = _pick(tp, 512, 256, 128)
    w_t, sel_t = _route(logits_t, col(router_bias), tn)
    n_blocks = -(-tp * TOP_K // MOE_BLOCK) + router_w.shape[-1]
    out = None
    for part in range(n_parts):
        cols = slice(part * tp, (part + 1) * tp)
        sel_p, w_p = sel_t[:, cols], w_t[:, cols]
        pos_t, cnt = _rank(sel_p, tn)
        slot_kt, wk = _slots(sel_p, w_p, pos_t, cnt, tn)
        expert, valid = _plan(cnt, n_blocks)
        x_sorted = _sc_scatter_rows(h1p, part * tp, slot_kt, n_blocks * MOE_BLOCK)
        y_sorted = _experts(x_sorted, expert, valid, *experts_w)
        y_tok = _sc_gather_rows(y_sorted, slot_kt)
        out = _combine(y_tok, wk, h1p, h1, part * tp, out, *shared, row(ln2_g), row(ln2_b), tn)
    return out.reshape(bsz, seq, d)


def kernel(x, w_in, b_in, hy_conv_w, hy_conv_b, hy_f_w1, hy_f_b1, hy_f_freq1, hy_f_w2, hy_f_b2,
           hy_f_freq2, hy_f_w3, hy_f_b3, hy_f_freq3, hy_f_w4, hy_bias, ssd_conv_w, ssd_conv_b,
           ssd_dt_bias, ssd_a_log, ssd_d, ssd_norm_w, w_hy_branch, w_ssd_branch, w_out, b_out,
           ln1_g, ln1_b, router_w, router_bias, exp_w_gate, exp_w_up, exp_w_down,
           sh_w_gate, sh_w_up, sh_w_down, ln2_g, ln2_b):
    params = (w_in, b_in, hy_conv_w, hy_conv_b, hy_f_w1, hy_f_b1, hy_f_freq1, hy_f_w2, hy_f_b2,
              hy_f_freq2, hy_f_w3, hy_f_b3, hy_f_freq3, hy_f_w4, hy_bias, ssd_conv_w, ssd_conv_b,
              ssd_dt_bias, ssd_a_log, ssd_d, ssd_norm_w, w_hy_branch, w_ssd_branch, w_out, b_out,
              ln1_g, ln1_b, router_w, router_bias, exp_w_gate, exp_w_up, exp_w_down,
              sh_w_gate, sh_w_up, sh_w_down, ln2_g, ln2_b)
    h = x
    for i in range(DEPTH):
        h = _layer(h, *(p[i] for p in params))
    return h
```
